```python
import math
import jax, jax.numpy as jnp
from jax import lax
import numpy as np

D_MODEL = 2048
BATCH = 2
SEQ = 4096
DEPTH = 4

GRID_W = 64
N_HEADS = 8
N_KV_HEADS = 2
GQA = N_HEADS // N_KV_HEADS
HEAD_DIM = 128
ATTN_WIDTH = N_HEADS * HEAD_DIM
KV_WIDTH = N_KV_HEADS * HEAD_DIM
Q_BLOCK = 128
ROPE_THETA = 10000.0
ROPE_AXIS_PAIRS = HEAD_DIM // 4
SSM_WIDTH = D_MODEL // 2
SSM_CH = 16
SSM_GROUPS = SSM_WIDTH // SSM_CH
SSM_STATE = 64
SSM_DT_MIN = 1e-3
SSM_DT_MAX = 1e-1
D_MIX = ATTN_WIDTH + SSM_WIDTH
D_IN_PROJ = ATTN_WIDTH + 2 * KV_WIDTH + SSM_WIDTH
N_EXPERT_GROUPS = 4
EXPERTS_PER_GROUP = 4
N_EXPERTS = N_EXPERT_GROUPS * EXPERTS_PER_GROUP
TOP_K_INNER = 2
D_EXPERT = 512
PLE_DIM = 256
ALPHA = (2 * DEPTH) ** 0.25
BETA = (8 * DEPTH) ** -0.25
EPS = 1e-6

kernel_name = "hybrid_attn_s5_hmoe_encoder"


def layer_norm(x, g, b):
    xf = x.astype(jnp.float32)
    mu = jnp.mean(xf, axis=-1, keepdims=True)
    var = jnp.mean(jnp.square(xf - mu), axis=-1, keepdims=True)
    y = (xf - mu) * lax.rsqrt(var + EPS)
    return (y * g.astype(jnp.float32) + b.astype(jnp.float32)).astype(x.dtype)


def rms_norm(x, g):
    xf = x.astype(jnp.float32)
    y = xf * lax.rsqrt(jnp.mean(jnp.square(xf), axis=-1, keepdims=True) + EPS)
    return (y * g.astype(jnp.float32)).astype(x.dtype)


def axial_rope_tables(seq_len, dtype):
    rows = seq_len // GRID_W
    row = jnp.repeat(jnp.arange(rows, dtype=jnp.float32), GRID_W)
    col = jnp.tile(jnp.arange(GRID_W, dtype=jnp.float32), rows)
    inv_freq = ROPE_THETA ** (-jnp.arange(ROPE_AXIS_PAIRS, dtype=jnp.float32) / ROPE_AXIS_PAIRS)
    ang = jnp.concatenate([row[:, None] * inv_freq, col[:, None] * inv_freq], axis=-1)
    return jnp.cos(ang).astype(dtype), jnp.sin(ang).astype(dtype)


def apply_rope(x, cos, sin):
    half = HEAD_DIM // 2
    x1, x2 = x[..., :half], x[..., half:]
    c, s = cos[:, None, :], sin[:, None, :]
    return jnp.concatenate([x1 * c - x2 * s, x2 * c + x1 * s], axis=-1)


def block_attention(q, k, v):
    bsz, s_len, _, _ = q.shape
    nb = s_len // Q_BLOCK
    qb = q.reshape(bsz, nb, Q_BLOCK, N_KV_HEADS, GQA, HEAD_DIM).transpose(1, 0, 2, 3, 4, 5)
    scale = HEAD_DIM ** -0.5

    def one_block(qblk):
        s = jnp.einsum('bqgrd,bkgd->bgrqk', qblk, k).astype(jnp.float32) * scale
        pr = jax.nn.softmax(s, axis=-1).astype(v.dtype)
        return jnp.einsum('bgrqk,bkgd->bqgrd', pr, v)

    o = lax.map(one_block, qb)
    return o.transpose(1, 0, 2, 3, 4, 5).reshape(bsz, s_len, ATTN_WIDTH)


def s5_direction(u, a_re, a_im, log_dt, b_re, b_im, c_re, c_im, reverse):
    dt = jnp.exp(log_dt)[:, None]
    mag = jnp.exp(dt * a_re)
    ang = dt * a_im
    abar_re, abar_im = mag * jnp.cos(ang), mag * jnp.sin(ang)
    den = a_re * a_re + a_im * a_im
    nr, ni = abar_re - 1.0, abar_im
    coef_re = (nr * a_re + ni * a_im) / den
    coef_im = (ni * a_re - nr * a_im) / den
    bb_re = coef_re[..., None] * b_re - coef_im[..., None] * b_im
    bb_im = coef_re[..., None] * b_im + coef_im[..., None] * b_re
    bu_re = jnp.einsum('bsgh,gph->bsgp', u, bb_re)
    bu_im = jnp.einsum('bsgh,gph->bsgp', u, bb_im)
    ar = jnp.broadcast_to(abar_re, bu_re.shape)
    ai = jnp.broadcast_to(abar_im, bu_re.shape)

    def combine(e1, e2):
        a1r, a1i, b1r, b1i = e1
        a2r, a2i, b2r, b2i = e2
        return (a2r * a1r - a2i * a1i,
                a2r * a1i + a2i * a1r,
                a2r * b1r - a2i * b1i + b2r,
                a2r * b1i + a2i * b1r + b2i)

    _, _, xr, xi = lax.associative_scan(combine, (ar, ai, bu_re, bu_im), axis=1, reverse=reverse)
    return jnp.einsum('bsgp,ghp->bsgh', xr, c_re) - jnp.einsum('bsgp,ghp->bsgh', xi, c_im)


def hybrid_mixer(h, cos, sin, w_in, q_norm_g, k_norm_g, a_re, a_im, log_dt, b_re, b_im,
                 c_re, c_im, d_skip, w_glu, b_glu, attn_out_g, ssm_out_g, w_out):
    bsz, s_len, _ = h.shape
    z = h @ w_in
    q, k, v, u = jnp.split(z, [ATTN_WIDTH, ATTN_WIDTH + KV_WIDTH, ATTN_WIDTH + 2 * KV_WIDTH], axis=-1)
    q = apply_rope(rms_norm(q.reshape(bsz, s_len, N_HEADS, HEAD_DIM), q_norm_g), cos, sin)
    k = apply_rope(rms_norm(k.reshape(bsz, s_len, N_KV_HEADS, HEAD_DIM), k_norm_g), cos, sin)
    v = v.reshape(bsz, s_len, N_KV_HEADS, HEAD_DIM)
    attn = block_attention(q, k, v)
    us = u.reshape(bsz, s_len, SSM_GROUPS, SSM_CH)
    y = (s5_direction(us, a_re[0], a_im[0], log_dt[0], b_re[0], b_im[0], c_re[0], c_im[0], False)
         + s5_direction(us, a_re[1], a_im[1], log_dt[1], b_re[1], b_im[1], c_re[1], c_im[1], True)
         + us * d_skip)
    y = jax.nn.gelu(y.reshape(bsz, s_len, SSM_WIDTH))
    y = y * jax.nn.sigmoid(y @ w_glu + b_glu)
    cat = jnp.concatenate([rms_norm(attn, attn_out_g), rms_norm(y, ssm_out_g)], axis=-1)
    return cat @ w_out


def hierarchical_moe(x, rg_w, rg_b, re_w, re_b, w_gate, w_up, w_down):
    bsz, s_len, d = x.shape
    t = x.reshape(bsz * s_len, d)
    g_prob = jax.nn.softmax((t @ rg_w + rg_b).astype(jnp.float32), axis=-1)
    g_top, g_idx = lax.top_k(g_prob, 1)
    e_logits = (t @ re_w + re_b).astype(jnp.float32).reshape(-1, N_EXPERT_GROUPS, EXPERTS_PER_GROUP)
    sel = jnp.take_along_axis(e_logits, g_idx[:, :, None], axis=1)[:, 0]
    e_top, e_idx = lax.top_k(jax.nn.softmax(sel, axis=-1), TOP_K_INNER)
    e_w = e_top / jnp.sum(e_top, axis=-1, keepdims=True)
    gates = g_top * e_w
    global_idx = g_idx * EXPERTS_PER_GROUP + e_idx
    combine = jnp.sum(jax.nn.one_hot(global_idx, N_EXPERTS, dtype=jnp.float32) * gates[..., None], axis=1)
    combine = combine.astype(x.dtype)
    hid = jax.nn.silu(jnp.einsum('td,edf->etf', t, w_gate)) * jnp.einsum('td,edf->etf', t, w_up)
    hid = hid * combine.T[:, :, None]
    out = jnp.einsum('etf,efd->td', hid, w_down)
    return out.reshape(bsz, s_len, d)


def setup_inputs(seed: int = 0) -> dict:
    key = jax.random.key(seed)
    ks = jax.random.split(key, 40)
    f32 = jnp.float32
    L, D, G, P, H = DEPTH, D_MODEL, SSM_GROUPS, SSM_STATE, SSM_CH
    nrm = lambda k, shape, s: jax.random.normal(k, shape, f32) * s
    n_idx = jnp.arange(P, dtype=f32)
    a_re = -0.5 * jnp.exp(nrm(ks[6], (L, 2, G, P), 0.05))
    a_im = math.pi * n_idx + nrm(ks[7], (L, 2, G, P), 0.01)
    log_dt = jax.random.uniform(ks[8], (L, 2, G), f32, math.log(SSM_DT_MIN), math.log(SSM_DT_MAX))
    return {
        "x": nrm(ks[0], (BATCH, SEQ, D), 1.0),
        "p": nrm(ks[1], (DEPTH, BATCH, SEQ, PLE_DIM), 1.0),
        "ln0_g": 1.0 + nrm(ks[2], (D,), 0.02),
        "ln0_b": nrm(ks[3], (D,), 0.02),
        "w_in": nrm(ks[4], (L, D, D_IN_PROJ), D ** -0.5),
        "q_norm_g": 1.0 + nrm(ks[5], (L, HEAD_DIM), 0.02),
        "k_norm_g": 1.0 + nrm(ks[9], (L, HEAD_DIM), 0.02),
        "ssm_a_re": a_re,
        "ssm_a_im": a_im,
        "ssm_log_dt": log_dt,
        "ssm_b_re": nrm(ks[10], (L, 2, G, P, H), (2 * H) ** -0.5),
        "ssm_b_im": nrm(ks[11], (L, 2, G, P, H), (2 * H) ** -0.5),
        "ssm_c_re": nrm(ks[12], (L, 2, G, H, P), (2 * P) ** -0.5),
        "ssm_c_im": nrm(ks[13], (L, 2, G, H, P), (2 * P) ** -0.5),
        "ssm_d": nrm(ks[14], (L, G, H), 1.0),
        "w_glu": nrm(ks[15], (L, SSM_WIDTH, SSM_WIDTH), SSM_WIDTH ** -0.5),
        "b_glu": nrm(ks[16], (L, SSM_WIDTH), 0.02),
        "attn_out_g": 1.0 + nrm(ks[17], (L, ATTN_WIDTH), 0.02),
        "ssm_out_g": 1.0 + nrm(ks[18], (L, SSM_WIDTH), 0.02),
        "w_out": nrm(ks[19], (L, D_MIX, D), BETA * D_MIX ** -0.5),
        "ln1_g": 1.0 + nrm(ks[20], (L, D), 0.02),
        "ln1_b": nrm(ks[21], (L, D), 0.02),
        "router_group_w": nrm(ks[22], (L, D, N_EXPERT_GROUPS), D ** -0.5),
        "router_group_b": nrm(ks[23], (L, N_EXPERT_GROUPS), 0.01),
        "router_expert_w": nrm(ks[24], (L, D, N_EXPERTS), D ** -0.5),
        "router_expert_b": nrm(ks[25], (L, N_EXPERTS), 0.01),
        "w_gate_e": nrm(ks[26], (L, N_EXPERTS, D, D_EXPERT), D ** -0.5),
        "w_up_e": nrm(ks[27], (L, N_EXPERTS, D, D_EXPERT), D ** -0.5),
        "w_down_e": nrm(ks[28], (L, N_EXPERTS, D_EXPERT, D), BETA * D_EXPERT ** -0.5),
        "w_ple": nrm(ks[29], (L, PLE_DIM, D), BETA * PLE_DIM ** -0.5),
        "w_ple_gate": nrm(ks[30], (L, D, D), D ** -0.5),
        "b_ple_gate": nrm(ks[31], (L, D), 0.02),
        "ln2_g": 1.0 + nrm(ks[32], (L, D), 0.02),
        "ln2_b": nrm(ks[33], (L, D), 0.02),
    }


def reference(x, p, ln0_g, ln0_b, w_in, q_norm_g, k_norm_g, ssm_a_re, ssm_a_im, ssm_log_dt,
              ssm_b_re, ssm_b_im, ssm_c_re, ssm_c_im, ssm_d, w_glu, b_glu, attn_out_g, ssm_out_g,
              w_out, ln1_g, ln1_b, router_group_w, router_group_b, router_expert_w, router_expert_b,
              w_gate_e, w_up_e, w_down_e, w_ple, w_ple_gate, b_ple_gate, ln2_g, ln2_b):
    seq_len = x.shape[1]
    cos, sin = axial_rope_tables(seq_len, x.dtype)
    x = layer_norm(x, ln0_g, ln0_b)
    for i in range(DEPTH):
        m = hybrid_mixer(x, cos, sin, w_in[i], q_norm_g[i], k_norm_g[i],
                         ssm_a_re[i], ssm_a_im[i], ssm_log_dt[i], ssm_b_re[i], ssm_b_im[i],
                         ssm_c_re[i], ssm_c_im[i], ssm_d[i], w_glu[i], b_glu[i],
                         attn_out_g[i], ssm_out_g[i], w_out[i])
        x = layer_norm(ALPHA * x + m, ln1_g[i], ln1_b[i])
        f = hierarchical_moe(x, router_group_w[i], router_group_b[i], router_expert_w[i],
                             router_expert_b[i], w_gate_e[i], w_up_e[i], w_down_e[i])
        ple = (p[i] @ w_ple[i]) * jax.nn.sigmoid(x @ w_ple_gate[i] + b_ple_gate[i])
        x = layer_norm(ALPHA * x + f + ple, ln2_g[i], ln2_b[i])
    return x
```

```python
import functools
import math

import jax
import jax.numpy as jnp
from jax import lax
from jax.experimental import pallas as pl
from jax.experimental.pallas import tpu as pltpu

F32 = jnp.float32
BF16 = jnp.bfloat16

D_MODEL = 2048
DEPTH = 4
GRID_W = 64
N_HEADS = 8
N_KV_HEADS = 2
GQA = N_HEADS // N_KV_HEADS
HEAD_DIM = 128
ATTN_WIDTH = N_HEADS * HEAD_DIM
KV_WIDTH = N_KV_HEADS * HEAD_DIM
QK_WIDTH = ATTN_WIDTH + KV_WIDTH
ROPE_THETA = 10000.0
ROPE_AXIS_PAIRS = HEAD_DIM // 4
SSM_WIDTH = D_MODEL // 2
SSM_CH = 16
SSM_GROUPS = SSM_WIDTH // SSM_CH
SSM_STATE = 64
D_MIX = ATTN_WIDTH + SSM_WIDTH
D_IN_PROJ = ATTN_WIDTH + 2 * KV_WIDTH + SSM_WIDTH
N_EXPERT_GROUPS = 4
EXPERTS_PER_GROUP = 4
N_EXPERTS = N_EXPERT_GROUPS * EXPERTS_PER_GROUP
D_EXPERT = 512
PLE_DIM = 256
ALPHA = (2 * DEPTH) ** 0.25
EPS = 1e-6

LANES = 128
SUBLANES = 8
VMEM_LIMIT = 56 * 1024 * 1024

SSM_CHUNK = 16
SSM_CW = SSM_CHUNK * SSM_CH
SSM_GB = 8


def _cparams(sem):
    return pltpu.CompilerParams(dimension_semantics=sem, vmem_limit_bytes=VMEM_LIMIT)


def _s5_discretize(a_re, a_im, log_dt, b_re, b_im):
    dt = jnp.exp(log_dt)[:, None]
    lam_re, lam_im = dt * a_re, dt * a_im
    mag = jnp.exp(lam_re)
    abar_re, abar_im = mag * jnp.cos(lam_im), mag * jnp.sin(lam_im)
    den = a_re * a_re + a_im * a_im
    nr, ni = abar_re - 1.0, abar_im
    coef_re = (nr * a_re + ni * a_im) / den
    coef_im = (ni * a_re - nr * a_im) / den
    bb_re = coef_re[..., None] * b_re - coef_im[..., None] * b_im
    bb_im = coef_re[..., None] * b_im + coef_im[..., None] * b_re
    return lam_re, lam_im, bb_re, bb_im


def _cpow(lam_re, lam_im, k):
    mag = jnp.exp(lam_re * k)
    return mag * jnp.cos(lam_im * k), mag * jnp.sin(lam_im * k)


def _s5_operands(a_re, a_im, log_dt, b_re, b_im, c_re, c_im, d_skip):
    hp = lax.Precision.HIGHEST
    L, H = SSM_CHUNK, SSM_CH
    lam_f = _s5_discretize(a_re[0], a_im[0], log_dt[0], b_re[0], b_im[0])
    lam_b = _s5_discretize(a_re[1], a_im[1], log_dt[1], b_re[1], b_im[1])
    tau = jnp.arange(L, dtype=F32)[:, None, None]

    def conv_kernel(lam, cr, ci):
        lr, li, bbr, bbi = lam
        pr, pi = _cpow(lr[None], li[None], tau)
        wr = pr[..., None] * bbr[None] - pi[..., None] * bbi[None]
        wi = pr[..., None] * bbi[None] + pi[..., None] * bbr[None]
        return (jnp.einsum('gop,lgpi->lgoi', cr, wr, precision=hp)
                - jnp.einsum('gop,lgpi->lgoi', ci, wi, precision=hp))

    k_f = conv_kernel(lam_f, c_re[0], c_im[0])
    k_b = conv_kernel(lam_b, c_re[1], c_im[1])
    s_idx = jnp.arange(L)[:, None]
    t_idx = jnp.arange(L)[None, :]
    lag = t_idx - s_idx
    kf_st = jnp.where((lag >= 0)[..., None, None, None], k_f[jnp.clip(lag, 0, L - 1)], 0.0)
    kb_st = jnp.where((lag <= 0)[..., None, None, None], k_b[jnp.clip(-lag, 0, L - 1)], 0.0)
    m = kf_st + kb_st
    eye_t = (lag == 0).astype(F32)[..., None, None, None]
    eye_h = jnp.eye(H, dtype=F32)[None, None, None]
    m = m + eye_t * eye_h * d_skip[None, None, :, :, None]
    m = m.transpose(2, 0, 4, 1, 3).reshape(SSM_GROUPS, L * H, L * H)

    def end_map(lam, expo):
        lr, li, bbr, bbi = lam
        pr, pi = _cpow(lr[None], li[None], expo)
        er = pr[..., None] * bbr[None] - pi[..., None] * bbi[None]
        ei = pr[..., None] * bbi[None] + pi[..., None] * bbr[None]
        to_rows = lambda z: z.transpose(1, 0, 3, 2).reshape(SSM_GROUPS, L * H, SSM_STATE)
        return to_rows(er), to_rows(ei)

    ef_re, ef_im = end_map(lam_f, (L - 1) - tau)
    eb_re, eb_im = end_map(lam_b, tau)
    w1 = jnp.concatenate([m, ef_re, eb_re, ef_im, eb_im], axis=-1)

    def out_map(lam, cr, ci, expo):
        lr, li, _, _ = lam
        pr, pi = _cpow(lr[None], li[None], expo)
        qr = cr[None] * pr[:, :, None, :] - ci[None] * pi[:, :, None, :]
        qi = cr[None] * pi[:, :, None, :] + ci[None] * pr[:, :, None, :]
        to_cols = lambda z: z.transpose(1, 3, 0, 2).reshape(SSM_GROUPS, SSM_STATE, L * H)
        return to_cols(qr), to_cols(-qi)

    of_re, of_im = out_map(lam_f, c_re[0], c_im[0], tau + 1.0)
    ob_re, ob_im = out_map(lam_b, c_re[1], c_im[1], L - tau)
    w2 = jnp.concatenate([of_re, ob_re, of_im, ob_im], axis=1)

    afr, afi = _cpow(lam_f[0], lam_f[1], float(L))
    abr, abi = _cpow(lam_b[0], lam_b[1], float(L))
    ar = jnp.concatenate([afr, abr], axis=-1)
    ai = jnp.concatenate([afi, abi], axis=-1)
    return w1.astype(BF16), w2.astype(BF16), ar, ai


_S5_ROWS = 264


def _gelu_tanh(y):
    return 0.5 * y * (1.0 + jnp.tanh(math.sqrt(2.0 / math.pi) * (y + 0.044715 * (y * y * y))))


def _s5_kernel(u_ref, w1_ref, w2_ref, ar_ref, ai_ref, y_ref,
               er_s, ei_s, fr_s, fi_s, br_s, bi_s, yi_s, *, n_chunks):
    nc = n_chunks
    zero_tile = jnp.zeros((SUBLANES, LANES), F32)
    for g in range(SSM_GB):
        base = g * _S5_ROWS
        r = jnp.dot(u_ref[g], w1_ref[g], preferred_element_type=F32)
        yi_s[g] = r[:, :SSM_CW]
        er_s[pl.ds(base, nc), :] = r[:, SSM_CW:SSM_CW + LANES]
        ei_s[pl.ds(base, nc), :] = r[:, SSM_CW + LANES:]
        fr_s[pl.ds(base, SUBLANES), :] = zero_tile
        fi_s[pl.ds(base, SUBLANES), :] = zero_tile
        br_s[pl.ds(base + nc, SUBLANES), :] = zero_tile
        bi_s[pl.ds(base + nc, SUBLANES), :] = zero_tile

    ar = ar_ref[...]
    ai = ai_ref[...]
    fwd_lane = lax.broadcasted_iota(jnp.int32, (SUBLANES, LANES), 1) < SSM_STATE

    def step(i, carry):
        xr, xi = carry
        rows_f = pl.ds(i, SUBLANES, stride=_S5_ROWS)
        rows_b = pl.ds(nc - 1 - i, SUBLANES, stride=_S5_ROWS)
        er = jnp.where(fwd_lane, er_s[rows_f, :], er_s[rows_b, :])
        ei = jnp.where(fwd_lane, ei_s[rows_f, :], ei_s[rows_b, :])
        nr = ar * xr - ai * xi + er
        ni = ar * xi + ai * xr + ei
        to_f = pl.ds(i + 1, SUBLANES, stride=_S5_ROWS)
        to_b = pl.ds(nc + 6 - i, SUBLANES, stride=_S5_ROWS)
        fr_s[to_f, :] = nr
        fi_s[to_f, :] = ni
        br_s[to_b, :] = nr
        bi_s[to_b, :] = ni
        return nr, ni

    lax.fori_loop(0, nc, step, (zero_tile, zero_tile))

    fwd_col = lax.broadcasted_iota(jnp.int32, (nc, LANES), 1) < SSM_STATE
    for g in range(SSM_GB):
        base = g * _S5_ROWS
        cr = jnp.where(fwd_col, fr_s[pl.ds(base, nc), :], br_s[pl.ds(base + SUBLANES, nc), :])
        ci = jnp.where(fwd_col, fi_s[pl.ds(base, nc), :], bi_s[pl.ds(base + SUBLANES, nc), :])
        y = (yi_s[g]
             + jnp.dot(cr.astype(BF16), w2_ref[g, :LANES, :], preferred_element_type=F32)
             + jnp.dot(ci.astype(BF16), w2_ref[g, LANES:, :], preferred_element_type=F32))
        y_ref[g] = _gelu_tanh(y).astype(y_ref.dtype)


def _s5_mixer(u_chunks, w1, w2, ar, ai):
    bsz, g, nc, cw = u_chunks.shape
    gb = SSM_GB
    kern = functools.partial(_s5_kernel, n_chunks=nc)
    return pl.pallas_call(
        kern,
        grid=(bsz, g // gb),
        in_specs=[
            pl.BlockSpec((None, gb, nc, cw), lambda b, j: (b, j, 0, 0)),
            pl.BlockSpec((gb, cw, 2 * cw), lambda b, j: (j, 0, 0)),
            pl.BlockSpec((gb, cw, cw), lambda b, j: (j, 0, 0)),
            pl.BlockSpec((gb, LANES), lambda b, j: (j, 0)),
            pl.BlockSpec((gb, LANES), lambda b, j: (j, 0)),
        ],
        out_specs=pl.BlockSpec((None, gb, nc, cw), lambda b, j: (b, j, 0, 0)),
        out_shape=jax.ShapeDtypeStruct((bsz, g, nc, cw), BF16),
        scratch_shapes=[
            *[pltpu.VMEM((gb * _S5_ROWS, LANES), F32) for _ in range(6)],
            pltpu.VMEM((gb, nc, cw), F32),
        ],
        compiler_params=_cparams(("parallel", "parallel")),
        name="s5_mixer",
    )(u_chunks, w1, w2, ar, ai)


def _layer_norm(h, g, b):
    mu = jnp.mean(h, axis=-1, keepdims=True)
    c = h - mu
    var = jnp.mean(c * c, axis=-1, keepdims=True)
    return c * lax.rsqrt(var + EPS) * g + b


def _ln0_kernel(x_ref, g_ref, b_ref, xf_ref, xb_ref):
    y = _layer_norm(x_ref[...], g_ref[...], b_ref[...])
    xf_ref[...] = y
    xb_ref[...] = y.astype(BF16)


def _row_spec(tm, width):
    return pl.BlockSpec((tm, width), lambda m: (m, 0))


def _layer_vec_spec(layer, width):
    return pl.BlockSpec((None, 1, width), lambda m: (layer, 0, 0))


def _ln0(x, g, b, tm=256):
    t, d = x.shape
    tm = min(tm, t)
    return pl.pallas_call(
        _ln0_kernel,
        grid=(t // tm,),
        in_specs=[_row_spec(tm, d),
                  pl.BlockSpec((1, d), lambda m: (0, 0)),
                  pl.BlockSpec((1, d), lambda m: (0, 0))],
        out_specs=[_row_spec(tm, d), _row_spec(tm, d)],
        out_shape=[jax.ShapeDtypeStruct((t, d), F32), jax.ShapeDtypeStruct((t, d), BF16)],
        compiler_params=_cparams(("parallel",)),
        name="ln0",
    )(x, g.reshape(1, d), b.reshape(1, d))


def _inproj_kernel(x_ref, w_ref, qg_ref, kg_ref, cos_ref, sin_ref, q_ref, k_ref, v_ref, u_ref):
    x = x_ref[...]
    cos2 = cos_ref[...]
    sin2 = sin_ref[...]
    scale = HEAD_DIM ** -0.5

    def head(z, gain, mult):
        zn = z * lax.rsqrt(jnp.mean(z * z, axis=-1, keepdims=True) + EPS) * gain
        rot = pltpu.roll(zn, HEAD_DIM // 2, axis=1)
        out = zn * cos2 + rot * sin2
        return out * mult if mult != 1.0 else out

    pair = 2 * HEAD_DIM
    for c in range(D_IN_PROJ // pair):
        acc = jnp.dot(x, w_ref[:, c * pair:(c + 1) * pair], preferred_element_type=F32)
        col = c * pair
        if col < ATTN_WIDTH:
            for h in range(2):
                z = head(acc[:, h * HEAD_DIM:(h + 1) * HEAD_DIM], qg_ref[...], scale)
                q_ref[:, col + h * HEAD_DIM: col + (h + 1) * HEAD_DIM] = z.astype(BF16)
        elif col < QK_WIDTH:
            for h in range(2):
                z = head(acc[:, h * HEAD_DIM:(h + 1) * HEAD_DIM], kg_ref[...], 1.0)
                k_ref[:, h * HEAD_DIM:(h + 1) * HEAD_DIM] = z.astype(BF16)
        elif col < QK_WIDTH + KV_WIDTH:
            v_ref[...] = acc.astype(BF16)
        else:
            off = col - (QK_WIDTH + KV_WIDTH)
            u_ref[:, off:off + pair] = acc.astype(BF16)


def _inproj(xb, w_in, q_gain, k_gain, cos2, sin2, layer, seq_len, tm=256):
    t, d = xb.shape
    tm = min(tm, seq_len)
    sblocks = seq_len // tm
    return pl.pallas_call(
        _inproj_kernel,
        grid=(t // tm,),
        in_specs=[_row_spec(tm, d),
                  pl.BlockSpec((None, d, D_IN_PROJ), lambda m: (layer, 0, 0)),
                  _layer_vec_spec(layer, HEAD_DIM),
                  _layer_vec_spec(layer, HEAD_DIM),
                  pl.BlockSpec((tm, HEAD_DIM), lambda m: (m % sblocks, 0)),
                  pl.BlockSpec((tm, HEAD_DIM), lambda m: (m % sblocks, 0))],
        out_specs=[_row_spec(tm, ATTN_WIDTH), _row_spec(tm, KV_WIDTH),
                   _row_spec(tm, KV_WIDTH), _row_spec(tm, SSM_WIDTH)],
        out_shape=[jax.ShapeDtypeStruct((t, ATTN_WIDTH), BF16),
                   jax.ShapeDtypeStruct((t, KV_WIDTH), BF16),
                   jax.ShapeDtypeStruct((t, KV_WIDTH), BF16),
                   jax.ShapeDtypeStruct((t, SSM_WIDTH), BF16)],
        compiler_params=_cparams(("parallel",)),
        name="inproj",
    )(xb, w_in, q_gain, k_gain, cos2, sin2)


def _attn_kernel(q_ref, k_ref, v_ref, g_ref, o_ref):
    outs = []
    ssq = None
    for h in range(N_HEADS):
        kv = h // GQA
        q = q_ref[:, h * HEAD_DIM:(h + 1) * HEAD_DIM]
        k = k_ref[:, kv * HEAD_DIM:(kv + 1) * HEAD_DIM]
        v = v_ref[:, kv * HEAD_DIM:(kv + 1) * HEAD_DIM]
        s = lax.dot_general(q, k, (((1,), (1,)), ((), ())), preferred_element_type=F32)
        m = jnp.max(s, axis=-1, keepdims=True)
        p = jnp.exp(s - m)
        l = jnp.sum(p, axis=-1, keepdims=True)
        o = jnp.dot(p.astype(BF16), v, preferred_element_type=F32) / l
        outs.append(o)
        sq = jnp.sum(o * o, axis=-1, keepdims=True)
        ssq = sq if ssq is None else ssq + sq
    r = lax.rsqrt(ssq * (1.0 / ATTN_WIDTH) + EPS)
    for h in range(N_HEADS):
        sl = slice(h * HEAD_DIM, (h + 1) * HEAD_DIM)
        o_ref[:, sl] = (outs[h] * r * g_ref[:, sl]).astype(BF16)


def _attention(q, k, v, gain, layer, bsz, seq_len, tq=256):
    t = q.shape[0]
    tq = min(tq, seq_len)
    qblocks = seq_len // tq
    return pl.pallas_call(
        _attn_kernel,
        grid=(bsz, qblocks),
        in_specs=[pl.BlockSpec((tq, ATTN_WIDTH), lambda b, i: (b * qblocks + i, 0)),
                  pl.BlockSpec((seq_len, KV_WIDTH), lambda b, i: (b, 0)),
                  pl.BlockSpec((seq_len, KV_WIDTH), lambda b, i: (b, 0)),
                  pl.BlockSpec((None, 1, ATTN_WIDTH), lambda b, i: (layer, 0, 0))],
        out_specs=pl.BlockSpec((tq, ATTN_WIDTH), lambda b, i: (b * qblocks + i, 0)),
        out_shape=jax.ShapeDtypeStruct((t, ATTN_WIDTH), BF16),
        compiler_params=_cparams(("parallel", "parallel")),
        name="attention",
    )(q, k, v, gain)


def _glu_kernel(y_ref, w_ref, b_ref, g_ref, o_ref):
    y = y_ref[...]
    gate = jnp.dot(y, w_ref[...], preferred_element_type=F32) + b_ref[...]
    z = y.astype(F32) * jax.nn.sigmoid(gate)
    zn = z * lax.rsqrt(jnp.mean(z * z, axis=-1, keepdims=True) + EPS) * g_ref[...]
    o_ref[...] = zn.astype(BF16)


def _glu(y, w_glu, b_glu, gain, layer, tm=512):
    t, w = y.shape
    tm = min(tm, t)
    return pl.pallas_call(
        _glu_kernel,
        grid=(t // tm,),
        in_specs=[_row_spec(tm, w),
                  pl.BlockSpec((None, w, w), lambda m: (layer, 0, 0)),
                  _layer_vec_spec(layer, w),
                  _layer_vec_spec(layer, w)],
        out_specs=_row_spec(tm, w),
        out_shape=jax.ShapeDtypeStruct((t, w), BF16),
        compiler_params=_cparams(("parallel",)),
        name="glu",
    )(y, w_glu, b_glu, gain)


ROUTE_LANE_GROUP = N_EXPERTS


def _route(logits):
    lane = lax.broadcasted_iota(jnp.int32, logits.shape, 1)
    neg = jnp.float32(-1e30)
    big = jnp.int32(LANES)

    def masked_softmax(mask):
        z = jnp.where(mask, logits, neg)
        zmax = jnp.max(z, axis=-1, keepdims=True)
        e = jnp.where(mask, jnp.exp(z - zmax), 0.0)
        return e / jnp.sum(e, axis=-1, keepdims=True)

    def first_argmax(vals, mask):
        top = jnp.max(jnp.where(mask, vals, -1.0), axis=-1, keepdims=True)
        idx = jnp.min(jnp.where(mask & (vals == top), lane, big), axis=-1, keepdims=True)
        return top, idx

    gmask = (lane >= ROUTE_LANE_GROUP) & (lane < ROUTE_LANE_GROUP + N_EXPERT_GROUPS)
    g_top, g_lane = first_argmax(masked_softmax(gmask), gmask)
    e_lo = (g_lane - ROUTE_LANE_GROUP) * EXPERTS_PER_GROUP
    emask = (lane >= e_lo) & (lane < e_lo + EXPERTS_PER_GROUP)
    e_prob = masked_softmax(emask)
    v1, i1 = first_argmax(e_prob, emask)
    v2, i2 = first_argmax(e_prob, emask & (lane != i1))
    denom = v1 + v2
    return (jnp.where(lane == i1, g_top * (v1 / denom), 0.0)
            + jnp.where(lane == i2, g_top * (v2 / denom), 0.0))


def _outproj_kernel(a_ref, y_ref, wa_ref, wy_ref, x_ref, g_ref, b_ref, wr_ref, br_ref,
                    xf_ref, xb_ref, c_ref):
    m = (jnp.dot(a_ref[...], wa_ref[...], preferred_element_type=F32)
         + jnp.dot(y_ref[...], wy_ref[...], preferred_element_type=F32))
    xn = _layer_norm(ALPHA * x_ref[...] + m, g_ref[...], b_ref[...])
    xf_ref[...] = xn
    xb_ref[...] = xn.astype(BF16)
    logits = jnp.dot(xn, wr_ref[...], preferred_element_type=F32,
                     precision=lax.Precision.HIGHEST) + br_ref[...]
    c_ref[...] = _route(logits)


def _outproj(attn_n, y_n, w_out, xf, ln_g, ln_b, w_route, b_route, layer, tm=256):
    t, d = xf.shape
    tm = min(tm, t)
    half = D_MIX // 2
    return pl.pallas_call(
        _outproj_kernel,
        grid=(t // tm,),
        in_specs=[_row_spec(tm, half), _row_spec(tm, half),
                  pl.BlockSpec((None, half, d), lambda m: (layer, 0, 0)),
                  pl.BlockSpec((None, half, d), lambda m: (layer, 1, 0)),
                  _row_spec(tm, d),
                  _layer_vec_spec(layer, d), _layer_vec_spec(layer, d),
                  pl.BlockSpec((None, d, LANES), lambda m: (layer, 0, 0)),
                  _layer_vec_spec(layer, LANES)],
        out_specs=[_row_spec(tm, d), _row_spec(tm, d), _row_spec(tm, LANES)],
        out_shape=[jax.ShapeDtypeStruct((t, d), F32), jax.ShapeDtypeStruct((t, d), BF16),
                   jax.ShapeDtypeStruct((t, LANES), F32)],
        compiler_params=_cparams(("parallel",)),
        name="outproj_ln_route",
    )(attn_n, y_n, w_out, w_out, xf, ln_g, ln_b, w_route, b_route)


def _moe_kernel(x_ref, c_ref, wg_ref, wu_ref, wd_ref, o_ref):
    e = pl.program_id(1)
    x = x_ref[...]
    lane = lax.broadcasted_iota(jnp.int32, c_ref.shape, 1)
    gate = jnp.sum(jnp.where(lane == e, c_ref[...], 0.0), axis=-1, keepdims=True)
    hg = jnp.dot(x, wg_ref[...], preferred_element_type=F32)
    hu = jnp.dot(x, wu_ref[...], preferred_element_type=F32)
    hid = (hg * jax.nn.sigmoid(hg)) * hu * gate
    out = jnp.dot(hid.astype(BF16), wd_ref[...], preferred_element_type=F32)

    @pl.when(e == 0)
    def _():
        o_ref[...] = out

    @pl.when(e != 0)
    def _():
        o_ref[...] += out


def _moe(xb, combine, w_gate, w_up, w_down, layer, tm=512):
    t, d = xb.shape
    tm = min(tm, t)
    return pl.pallas_call(
        _moe_kernel,
        grid=(t // tm, N_EXPERTS),
        in_specs=[pl.BlockSpec((tm, d), lambda m, e: (m, 0)),
                  pl.BlockSpec((tm, LANES), lambda m, e: (m, 0)),
                  pl.BlockSpec((None, None, d, D_EXPERT), lambda m, e: (layer, e, 0, 0)),
                  pl.BlockSpec((None, None, d, D_EXPERT), lambda m, e: (layer, e, 0, 0)),
                  pl.BlockSpec((None, None, D_EXPERT, d), lambda m, e: (layer, e, 0, 0))],
        out_specs=pl.BlockSpec((tm, d), lambda m, e: (m, 0)),
        out_shape=jax.ShapeDtypeStruct((t, d), F32),
        compiler_params=_cparams(("parallel", "arbitrary")),
        name="moe_experts",
    )(xb, combine, w_gate, w_up, w_down)


def _ple_kernel(xb_ref, xf_ref, f_ref, p_ref, wp_ref, wg_ref, bg_ref, g_ref, b_ref, of_ref, ob_ref):
    gate = jnp.dot(xb_ref[...], wg_ref[...], preferred_element_type=F32) + bg_ref[...]
    ple = jnp.dot(p_ref[...].astype(BF16), wp_ref[...], preferred_element_type=F32) * jax.nn.sigmoid(gate)
    xn = _layer_norm(ALPHA * xf_ref[...] + f_ref[...] + ple, g_ref[...], b_ref[...])
    of_ref[...] = xn
    ob_ref[...] = xn.astype(BF16)


def _ple_ln(xb, xf, f, p, w_ple, w_pg, b_pg, ln_g, ln_b, layer, tm=256):
    t, d = xf.shape
    tm = min(tm, t)
    return pl.pallas_call(
        _ple_kernel,
        grid=(t // tm,),
        in_specs=[_row_spec(tm, d), _row_spec(tm, d), _row_spec(tm, d),
                  pl.BlockSpec((None, tm, PLE_DIM), lambda m: (layer, m, 0)),
                  pl.BlockSpec((None, PLE_DIM, d), lambda m: (layer, 0, 0)),
                  pl.BlockSpec((None, d, d), lambda m: (layer, 0, 0)),
                  _layer_vec_spec(layer, d), _layer_vec_spec(layer, d), _layer_vec_spec(layer, d)],
        out_specs=[_row_spec(tm, d), _row_spec(tm, d)],
        out_shape=[jax.ShapeDtypeStruct((t, d), F32), jax.ShapeDtypeStruct((t, d), BF16)],
        compiler_params=_cparams(("parallel",)),
        name="ple_ln",
    )(xb, xf, f, p, w_ple, w_pg, b_pg, ln_g, ln_b)


def _rope_tables(seq_len):
    rows = seq_len // GRID_W
    row = jnp.repeat(jnp.arange(rows, dtype=F32), GRID_W)
    col = jnp.tile(jnp.arange(GRID_W, dtype=F32), rows)
    inv_freq = ROPE_THETA ** (-jnp.arange(ROPE_AXIS_PAIRS, dtype=F32) / ROPE_AXIS_PAIRS)
    ang = jnp.concatenate([row[:, None] * inv_freq, col[:, None] * inv_freq], axis=-1)
    cos, sin = jnp.cos(ang), jnp.sin(ang)
    return jnp.concatenate([cos, cos], axis=-1), jnp.concatenate([-sin, sin], axis=-1)


def kernel(x, p, ln0_g, ln0_b, w_in, q_norm_g, k_norm_g, ssm_a_re, ssm_a_im, ssm_log_dt, ssm_b_re, ssm_b_im, ssm_c_re, ssm_c_im, ssm_d, w_glu, b_glu, attn_out_g, ssm_out_g, w_out, ln1_g, ln1_b, router_group_w, router_group_b, router_expert_w, router_expert_b, w_gate_e, w_up_e, w_down_e, w_ple, w_ple_gate, b_ple_gate, ln2_g, ln2_b):
    bsz, seq_len, d = x.shape
    t = bsz * seq_len
    n_layers = w_in.shape[0]
    nc = seq_len // SSM_CHUNK
    vec = lambda a: a.reshape(n_layers, 1, a.shape[-1])

    cos2, sin2 = _rope_tables(seq_len)
    w_in_b, w_glu_b, w_out_b = w_in.astype(BF16), w_glu.astype(BF16), w_out.astype(BF16)
    w_gate_b, w_up_b, w_down_b = w_gate_e.astype(BF16), w_up_e.astype(BF16), w_down_e.astype(BF16)
    w_ple_b, w_pg_b = w_ple.astype(BF16), w_ple_gate.astype(BF16)
    pad = LANES - N_EXPERTS - N_EXPERT_GROUPS
    w_route = jnp.pad(jnp.concatenate([router_expert_w, router_group_w], axis=-1), ((0, 0), (0, 0), (0, pad)))
    b_route = jnp.pad(jnp.concatenate([router_expert_b, router_group_b], axis=-1), ((0, 0), (0, pad)))
    s5_ops = jax.vmap(_s5_operands)(ssm_a_re, ssm_a_im, ssm_log_dt, ssm_b_re, ssm_b_im,
                                    ssm_c_re, ssm_c_im, ssm_d)
    p2 = p.reshape(n_layers, t, PLE_DIM)

    xf, xb = _ln0(x.reshape(t, d), ln0_g, ln0_b)
    for i in range(n_layers):
        q, k, v, u = _inproj(xb, w_in_b, vec(q_norm_g), vec(k_norm_g), cos2, sin2, i, seq_len)
        attn_n = _attention(q, k, v, vec(attn_out_g), i, bsz, seq_len)
        u_chunks = (u.reshape(bsz, nc, SSM_CHUNK, SSM_GROUPS, SSM_CH)
                    .transpose(0, 3, 1, 2, 4).reshape(bsz, SSM_GROUPS, nc, SSM_CW))
        y_chunks = _s5_mixer(u_chunks, s5_ops[0][i], s5_ops[1][i], s5_ops[2][i], s5_ops[3][i])
        y = (y_chunks.reshape(bsz, SSM_GROUPS, nc, SSM_CHUNK, SSM_CH)
             .transpose(0, 2, 3, 1, 4).reshape(t, SSM_WIDTH))
        y_n = _glu(y, w_glu_b, vec(b_glu), vec(ssm_out_g), i)
        xf, xb, combine = _outproj(attn_n, y_n, w_out_b, xf, vec(ln1_g), vec(ln1_b),
                                   w_route, vec(b_route), i)
        f = _moe(xb, combine, w_gate_b, w_up_b, w_down_b, i)
        xf, xb = _ple_ln(xb, xf, f, p2, w_ple_b, w_pg_b, vec(b_ple_gate), vec(ln2_g), vec(ln2_b), i)
    return xf.reshape(bsz, seq_len, d)
```

```python
import functools
import math

import jax
import jax.numpy as jnp
from jax import lax
from jax.experimental import pallas as pl
from jax.experimental.pallas import tpu as pltpu

F32 = jnp.float32
BF16 = jnp.bfloat16

D_MODEL = 2048
DEPTH = 4
GRID_W = 64
N_HEADS = 8
N_KV_HEADS = 2
GQA = N_HEADS // N_KV_HEADS
HEAD_DIM = 128
ATTN_WIDTH = N_HEADS * HEAD_DIM
KV_WIDTH = N_KV_HEADS * HEAD_DIM
QK_WIDTH = ATTN_WIDTH + KV_WIDTH
ROPE_THETA = 10000.0
ROPE_AXIS_PAIRS = HEAD_DIM // 4
SSM_WIDTH = D_MODEL // 2
SSM_CH = 16
SSM_GROUPS = SSM_WIDTH // SSM_CH
SSM_STATE = 64
D_MIX = ATTN_WIDTH + SSM_WIDTH
D_IN_PROJ = ATTN_WIDTH + 2 * KV_WIDTH + SSM_WIDTH
N_EXPERT_GROUPS = 4
EXPERTS_PER_GROUP = 4
N_EXPERTS = N_EXPERT_GROUPS * EXPERTS_PER_GROUP
D_EXPERT = 512
PLE_DIM = 256
ALPHA = (2 * DEPTH) ** 0.25
EPS = 1e-6

LANES = 128
SUBLANES = 8
VMEM_LIMIT = 56 * 1024 * 1024

SSM_CHUNK = 16
SSM_CW = SSM_CHUNK * SSM_CH
SSM_GB = 8


def _cparams(sem):
    return pltpu.CompilerParams(dimension_semantics=sem, vmem_limit_bytes=VMEM_LIMIT)


def _s5_discretize(a_re, a_im, log_dt, b_re, b_im):
    dt = jnp.exp(log_dt)[:, None]
    lam_re, lam_im = dt * a_re, dt * a_im
    mag = jnp.exp(lam_re)
    abar_re, abar_im = mag * jnp.cos(lam_im), mag * jnp.sin(lam_im)
    den = a_re * a_re + a_im * a_im
    nr, ni = abar_re - 1.0, abar_im
    coef_re = (nr * a_re + ni * a_im) / den
    coef_im = (ni * a_re - nr * a_im) / den
    bb_re = coef_re[..., None] * b_re - coef_im[..., None] * b_im
    bb_im = coef_re[..., None] * b_im + coef_im[..., None] * b_re
    return lam_re, lam_im, bb_re, bb_im


def _cpow(lam_re, lam_im, k):
    mag = jnp.exp(lam_re * k)
    return mag * jnp.cos(lam_im * k), mag * jnp.sin(lam_im * k)


def _s5_operands(a_re, a_im, log_dt, b_re, b_im, c_re, c_im, d_skip):
    hp = lax.Precision.HIGHEST
    L, H = SSM_CHUNK, SSM_CH
    lam_f = _s5_discretize(a_re[0], a_im[0], log_dt[0], b_re[0], b_im[0])
    lam_b = _s5_discretize(a_re[1], a_im[1], log_dt[1], b_re[1], b_im[1])
    tau = jnp.arange(L, dtype=F32)[:, None, None]

    def conv_kernel(lam, cr, ci):
        lr, li, bbr, bbi = lam
        pr, pi = _cpow(lr[None], li[None], tau)
        wr = pr[..., None] * bbr[None] - pi[..., None] * bbi[None]
        wi = pr[..., None] * bbi[None] + pi[..., None] * bbr[None]
        return (jnp.einsum('gop,lgpi->lgoi', cr, wr, precision=hp)
                - jnp.einsum('gop,lgpi->lgoi', ci, wi, precision=hp))

    k_f = conv_kernel(lam_f, c_re[0], c_im[0])
    k_b = conv_kernel(lam_b, c_re[1], c_im[1])
    s_idx = jnp.arange(L)[:, None]
    t_idx = jnp.arange(L)[None, :]
    lag = t_idx - s_idx
    kf_st = jnp.where((lag >= 0)[..., None, None, None], k_f[jnp.clip(lag, 0, L - 1)], 0.0)
    kb_st = jnp.where((lag <= 0)[..., None, None, None], k_b[jnp.clip(-lag, 0, L - 1)], 0.0)
    m = kf_st + kb_st
    eye_t = (lag == 0).astype(F32)[..., None, None, None]
    eye_h = jnp.eye(H, dtype=F32)[None, None, None]
    m = m + eye_t * eye_h * d_skip[None, None, :, :, None]
    m = m.transpose(2, 0, 4, 1, 3).reshape(SSM_GROUPS, L * H, L * H)

    def end_map(lam, expo):
        lr, li, bbr, bbi = lam
        pr, pi = _cpow(lr[None], li[None], expo)
        er = pr[..., None] * bbr[None] - pi[..., None] * bbi[None]
        ei = pr[..., None] * bbi[None] + pi[..., None] * bbr[None]
        to_rows = lambda z: z.transpose(1, 0, 3, 2).reshape(SSM_GROUPS, L * H, SSM_STATE)
        return to_rows(er), to_rows(ei)

    ef_re, ef_im = end_map(lam_f, (L - 1) - tau)
    eb_re, eb_im = end_map(lam_b, tau)
    w1 = jnp.concatenate([m, ef_re, eb_re, ef_im, eb_im], axis=-1)

    def out_map(lam, cr, ci, expo):
        lr, li, _, _ = lam
        pr, pi = _cpow(lr[None], li[None], expo)
        qr = cr[None] * pr[:, :, None, :] - ci[None] * pi[:, :, None, :]
        qi = cr[None] * pi[:, :, None, :] + ci[None] * pr[:, :, None, :]
        to_cols = lambda z: z.transpose(1, 3, 0, 2).reshape(SSM_GROUPS, SSM_STATE, L * H)
        return to_cols(qr), to_cols(-qi)

    of_re, of_im = out_map(lam_f, c_re[0], c_im[0], tau + 1.0)
    ob_re, ob_im = out_map(lam_b, c_re[1], c_im[1], L - tau)
    w2 = jnp.concatenate([of_re, ob_re, of_im, ob_im], axis=1)

    afr, afi = _cpow(lam_f[0], lam_f[1], float(L))
    abr, abi = _cpow(lam_b[0], lam_b[1], float(L))
    ar = jnp.concatenate([afr, abr], axis=-1)
    ai = jnp.concatenate([afi, abi], axis=-1)
    return w1.astype(BF16), w2.astype(BF16), ar, ai


_S5_ROWS = 264


def _gelu_tanh(y):
    return 0.5 * y * (1.0 + jnp.tanh(math.sqrt(2.0 / math.pi) * (y + 0.044715 * (y * y * y))))


def _s5_kernel(u_ref, w1_ref, w2_ref, ar_ref, ai_ref, y_ref,
               er_s, ei_s, fr_s, fi_s, br_s, bi_s, yi_s, *, n_chunks):
    nc = n_chunks
    zero_tile = jnp.zeros((SUBLANES, LANES), F32)
    for g in range(SSM_GB):
        base = g * _S5_ROWS
        r = jnp.dot(u_ref[g], w1_ref[g], preferred_element_type=F32)
        yi_s[g] = r[:, :SSM_CW]
        er_s[pl.ds(base, nc), :] = r[:, SSM_CW:SSM_CW + LANES]
        ei_s[pl.ds(base, nc), :] = r[:, SSM_CW + LANES:]
        fr_s[pl.ds(base, SUBLANES), :] = zero_tile
        fi_s[pl.ds(base, SUBLANES), :] = zero_tile
        br_s[pl.ds(base + nc, SUBLANES), :] = zero_tile
        bi_s[pl.ds(base + nc, SUBLANES), :] = zero_tile

    ar = ar_ref[...]
    ai = ai_ref[...]
    fwd_lane = lax.broadcasted_iota(jnp.int32, (SUBLANES, LANES), 1) < SSM_STATE

    def step(i, carry):
        xr, xi = carry
        rows_f = pl.ds(i, SUBLANES, stride=_S5_ROWS)
        rows_b = pl.ds(nc - 1 - i, SUBLANES, stride=_S5_ROWS)
        er = jnp.where(fwd_lane, er_s[rows_f, :], er_s[rows_b, :])
        ei = jnp.where(fwd_lane, ei_s[rows_f, :], ei_s[rows_b, :])
        nr = ar * xr - ai * xi + er
        ni = ar * xi + ai * xr + ei
        to_f = pl.ds(i + 1, SUBLANES, stride=_S5_ROWS)
        to_b = pl.ds(nc + 6 - i, SUBLANES, stride=_S5_ROWS)
        fr_s[to_f, :] = nr
        fi_s[to_f, :] = ni
        br_s[to_b, :] = nr
        bi_s[to_b, :] = ni
        return nr, ni

    lax.fori_loop(0, nc, step, (zero_tile, zero_tile))

    fwd_col = lax.broadcasted_iota(jnp.int32, (nc, LANES), 1) < SSM_STATE
    for g in range(SSM_GB):
        base = g * _S5_ROWS
        cr = jnp.where(fwd_col, fr_s[pl.ds(base, nc), :], br_s[pl.ds(base + SUBLANES, nc), :])
        ci = jnp.where(fwd_col, fi_s[pl.ds(base, nc), :], bi_s[pl.ds(base + SUBLANES, nc), :])
        y = (yi_s[g]
             + jnp.dot(cr.astype(BF16), w2_ref[g, :LANES, :], preferred_element_type=F32)
             + jnp.dot(ci.astype(BF16), w2_ref[g, LANES:, :], preferred_element_type=F32))
        y_ref[g] = _gelu_tanh(y).astype(y_ref.dtype)


def _s5_mixer(u_chunks, w1, w2, ar, ai):
    bsz, g, nc, cw = u_chunks.shape
    gb = SSM_GB
    kern = functools.partial(_s5_kernel, n_chunks=nc)
    return pl.pallas_call(
        kern,
        grid=(bsz, g // gb),
        in_specs=[
            pl.BlockSpec((None, gb, nc, cw), lambda b, j: (b, j, 0, 0)),
            pl.BlockSpec((gb, cw, 2 * cw), lambda b, j: (j, 0, 0)),
            pl.BlockSpec((gb, cw, cw), lambda b, j: (j, 0, 0)),
            pl.BlockSpec((gb, LANES), lambda b, j: (j, 0)),
            pl.BlockSpec((gb, LANES), lambda b, j: (j, 0)),
        ],
        out_specs=pl.BlockSpec((None, gb, nc, cw), lambda b, j: (b, j, 0, 0)),
        out_shape=jax.ShapeDtypeStruct((bsz, g, nc, cw), BF16),
        scratch_shapes=[
            *[pltpu.VMEM((gb * _S5_ROWS, LANES), F32) for _ in range(6)],
            pltpu.VMEM((gb, nc, cw), F32),
        ],
        compiler_params=_cparams(("parallel", "parallel")),
        name="s5_mixer",
    )(u_chunks, w1, w2, ar, ai)


def _layer_norm(h, g, b):
    mu = jnp.mean(h, axis=-1, keepdims=True)
    c = h - mu
    var = jnp.mean(c * c, axis=-1, keepdims=True)
    return c * lax.rsqrt(var + EPS) * g + b


def _ln0_kernel(x_ref, g_ref, b_ref, xf_ref, xb_ref):
    y = _layer_norm(x_ref[...], g_ref[...], b_ref[...])
    xf_ref[...] = y
    xb_ref[...] = y.astype(BF16)


def _row_spec(tm, width):
    return pl.BlockSpec((tm, width), lambda m: (m, 0))


def _layer_vec_spec(layer, width):
    return pl.BlockSpec((None, 1, width), lambda m: (layer, 0, 0))


def _ln0(x, g, b, tm=256):
    t, d = x.shape
    tm = min(tm, t)
    return pl.pallas_call(
        _ln0_kernel,
        grid=(t // tm,),
        in_specs=[_row_spec(tm, d),
                  pl.BlockSpec((1, d), lambda m: (0, 0)),
                  pl.BlockSpec((1, d), lambda m: (0, 0))],
        out_specs=[_row_spec(tm, d), _row_spec(tm, d)],
        out_shape=[jax.ShapeDtypeStruct((t, d), F32), jax.ShapeDtypeStruct((t, d), BF16)],
        compiler_params=_cparams(("parallel",)),
        name="ln0",
    )(x, g.reshape(1, d), b.reshape(1, d))


def _inproj_kernel(x_ref, w_ref, qg_ref, kg_ref, cos_ref, sin_ref, q_ref, k_ref, v_ref, u_ref):
    x = x_ref[...]
    cos2 = cos_ref[...]
    sin2 = sin_ref[...]
    scale = HEAD_DIM ** -0.5

    def head(z, gain, mult):
        zn = z * lax.rsqrt(jnp.mean(z * z, axis=-1, keepdims=True) + EPS) * gain
        rot = pltpu.roll(zn, HEAD_DIM // 2, axis=1)
        out = zn * cos2 + rot * sin2
        return out * mult if mult != 1.0 else out

    pair = 2 * HEAD_DIM
    for c in range(D_IN_PROJ // pair):
        acc = jnp.dot(x, w_ref[:, c * pair:(c + 1) * pair], preferred_element_type=F32)
        col = c * pair
        if col < ATTN_WIDTH:
            for h in range(2):
                z = head(acc[:, h * HEAD_DIM:(h + 1) * HEAD_DIM], qg_ref[...], scale)
                q_ref[:, col + h * HEAD_DIM: col + (h + 1) * HEAD_DIM] = z.astype(BF16)
        elif col < QK_WIDTH:
            for h in range(2):
                z = head(acc[:, h * HEAD_DIM:(h + 1) * HEAD_DIM], kg_ref[...], 1.0)
                k_ref[:, h * HEAD_DIM:(h + 1) * HEAD_DIM] = z.astype(BF16)
        elif col < QK_WIDTH + KV_WIDTH:
            v_ref[...] = acc.astype(BF16)
        else:
            off = col - (QK_WIDTH + KV_WIDTH)
            u_ref[:, off:off + pair] = acc.astype(BF16)


def _inproj(xb, w_in, q_gain, k_gain, cos2, sin2, layer, seq_len, tm=256):
    t, d = xb.shape
    tm = min(tm, seq_len)
    sblocks = seq_len // tm
    return pl.pallas_call(
        _inproj_kernel,
        grid=(t // tm,),
        in_specs=[_row_spec(tm, d),
                  pl.BlockSpec((None, d, D_IN_PROJ), lambda m: (layer, 0, 0)),
                  _layer_vec_spec(layer, HEAD_DIM),
                  _layer_vec_spec(layer, HEAD_DIM),
                  pl.BlockSpec((tm, HEAD_DIM), lambda m: (m % sblocks, 0)),
                  pl.BlockSpec((tm, HEAD_DIM), lambda m: (m % sblocks, 0))],
        out_specs=[_row_spec(tm, ATTN_WIDTH), _row_spec(tm, KV_WIDTH),
                   _row_spec(tm, KV_WIDTH), _row_spec(tm, SSM_WIDTH)],
        out_shape=[jax.ShapeDtypeStruct((t, ATTN_WIDTH), BF16),
                   jax.ShapeDtypeStruct((t, KV_WIDTH), BF16),
                   jax.ShapeDtypeStruct((t, KV_WIDTH), BF16),
                   jax.ShapeDtypeStruct((t, SSM_WIDTH), BF16)],
        compiler_params=_cparams(("parallel",)),
        name="inproj",
    )(xb, w_in, q_gain, k_gain, cos2, sin2)


def _attn_kernel(q_ref, k_ref, v_ref, g_ref, o_ref):
    outs = []
    ssq = None
    for h in range(N_HEADS):
        kv = h // GQA
        q = q_ref[:, h * HEAD_DIM:(h + 1) * HEAD_DIM]
        k = k_ref[:, kv * HEAD_DIM:(kv + 1) * HEAD_DIM]
        v = v_ref[:, kv * HEAD_DIM:(kv + 1) * HEAD_DIM]
        s = lax.dot_general(q, k, (((1,), (1,)), ((), ())), preferred_element_type=F32)
        m = jnp.max(s, axis=-1, keepdims=True)
        p = jnp.exp(s - m)
        l = jnp.sum(p, axis=-1, keepdims=True)
        o = jnp.dot(p.astype(BF16), v, preferred_element_type=F32) / l
        outs.append(o)
        sq = jnp.sum(o * o, axis=-1, keepdims=True)
        ssq = sq if ssq is None else ssq + sq
    r = lax.rsqrt(ssq * (1.0 / ATTN_WIDTH) + EPS)
    for h in range(N_HEADS):
        sl = slice(h * HEAD_DIM, (h + 1) * HEAD_DIM)
        o_ref[:, sl] = (outs[h] * r * g_ref[:, sl]).astype(BF16)


def _attention(q, k, v, gain, layer, bsz, seq_len, tq=256):
    t = q.shape[0]
    tq = min(tq, seq_len)
    qblocks = seq_len // tq
    return pl.pallas_call(
        _attn_kernel,
        grid=(bsz, qblocks),
        in_specs=[pl.BlockSpec((tq, ATTN_WIDTH), lambda b, i: (b * qblocks + i, 0)),
                  pl.BlockSpec((seq_len, KV_WIDTH), lambda b, i: (b, 0)),
                  pl.BlockSpec((seq_len, KV_WIDTH), lambda b, i: (b, 0)),
                  pl.BlockSpec((None, 1, ATTN_WIDTH), lambda b, i: (layer, 0, 0))],
        out_specs=pl.BlockSpec((tq, ATTN_WIDTH), lambda b, i: (b * qblocks + i, 0)),
        out_shape=jax.ShapeDtypeStruct((t, ATTN_WIDTH), BF16),
        compiler_params=_cparams(("parallel", "parallel")),
        name="attention",
    )(q, k, v, gain)


def _glu_kernel(y_ref, w_ref, b_ref, g_ref, o_ref):
    y = y_ref[...]
    gate = jnp.dot(y, w_ref[...], preferred_element_type=F32) + b_ref[...]
    z = y.astype(F32) * jax.nn.sigmoid(gate)
    zn = z * lax.rsqrt(jnp.mean(z * z, axis=-1, keepdims=True) + EPS) * g_ref[...]
    o_ref[...] = zn.astype(BF16)


def _glu(y, w_glu, b_glu, gain, layer, tm=512):
    t, w = y.shape
    tm = min(tm, t)
    return pl.pallas_call(
        _glu_kernel,
        grid=(t // tm,),
        in_specs=[_row_spec(tm, w),
                  pl.BlockSpec((None, w, w), lambda m: (layer, 0, 0)),
                  _layer_vec_spec(layer, w),
                  _layer_vec_spec(layer, w)],
        out_specs=_row_spec(tm, w),
        out_shape=jax.ShapeDtypeStruct((t, w), BF16),
        compiler_params=_cparams(("parallel",)),
        name="glu",
    )(y, w_glu, b_glu, gain)


ROUTE_LANE_GROUP = N_EXPERTS


def _route(logits):
    lane = lax.broadcasted_iota(jnp.int32, logits.shape, 1)
    neg = jnp.float32(-1e30)
    big = jnp.int32(LANES)

    def masked_softmax(mask):
        z = jnp.where(mask, logits, neg)
        zmax = jnp.max(z, axis=-1, keepdims=True)
        e = jnp.where(mask, jnp.exp(z - zmax), 0.0)
        return e / jnp.sum(e, axis=-1, keepdims=True)

    def first_argmax(vals, mask):
        top = jnp.max(jnp.where(mask, vals, -1.0), axis=-1, keepdims=True)
        idx = jnp.min(jnp.where(mask & (vals == top), lane, big), axis=-1, keepdims=True)
        return top, idx

    gmask = (lane >= ROUTE_LANE_GROUP) & (lane < ROUTE_LANE_GROUP + N_EXPERT_GROUPS)
    g_top, g_lane = first_argmax(masked_softmax(gmask), gmask)
    e_lo = (g_lane - ROUTE_LANE_GROUP) * EXPERTS_PER_GROUP
    emask = (lane >= e_lo) & (lane < e_lo + EXPERTS_PER_GROUP)
    e_prob = masked_softmax(emask)
    v1, i1 = first_argmax(e_prob, emask)
    v2, i2 = first_argmax(e_prob, emask & (lane != i1))
    denom = v1 + v2
    return i1, i2, g_top * (v1 / denom), g_top * (v2 / denom)


TOK_ROWS = D_MODEL // LANES
MOE_TM = 256
RL_E1, RL_E2, RL_G1, RL_G2, RL_R1, RL_R2 = 0, 1, 2, 3, 4, 5


def _to_token_major(ref, x):
    tm = x.shape[0]
    for s in range(TOK_ROWS):
        ref[pl.ds(s, tm, stride=TOK_ROWS), :] = x[:, s * LANES:(s + 1) * LANES]


def _from_token_major(ref, tm, s):
    return ref[pl.ds(s, tm, stride=TOK_ROWS), :]


def _outproj_kernel(a_ref, y_ref, wa_ref, wy_ref, x_ref, g_ref, b_ref, wr_ref, br_ref,
                    xb_ref, x16_ref, r_ref, cnt_ref, run_s):
    tm = a_ref.shape[0]

    @pl.when(pl.program_id(0) == 0)
    def _():
        run_s[...] = jnp.zeros_like(run_s)

    m = (jnp.dot(a_ref[...], wa_ref[...], preferred_element_type=F32)
         + jnp.dot(y_ref[...], wy_ref[...], preferred_element_type=F32))
    xn = _layer_norm(ALPHA * x_ref[...] + m, g_ref[...], b_ref[...])
    xb_ref[...] = xn.astype(BF16)
    _to_token_major(x16_ref, xn)
    logits = jnp.dot(xn, wr_ref[...], preferred_element_type=F32,
                     precision=lax.Precision.HIGHEST) + br_ref[...]
    i1, i2, g1, g2 = _route(logits)

    lane = lax.broadcasted_iota(jnp.int32, (tm, LANES), 1)
    hit1, hit2 = lane == i1, lane == i2
    onehot = (hit1 | hit2).astype(BF16)
    earlier = (lax.broadcasted_iota(jnp.int32, (tm, tm), 0)
               > lax.broadcasted_iota(jnp.int32, (tm, tm), 1)).astype(BF16)
    rank = jnp.dot(earlier, onehot, preferred_element_type=F32) + run_s[0:1, :]
    r1 = jnp.sum(jnp.where(hit1, rank, 0.0), axis=-1, keepdims=True)
    r2 = jnp.sum(jnp.where(hit2, rank, 0.0), axis=-1, keepdims=True)
    total = run_s[0:1, :] + jnp.sum(onehot.astype(F32), axis=0, keepdims=True)
    run_s[...] = jnp.broadcast_to(total, run_s.shape)
    cnt_ref[...] = jnp.broadcast_to(total, cnt_ref.shape)
    out = jnp.zeros((tm, LANES), F32)
    for col, val in ((RL_E1, i1.astype(F32)), (RL_E2, i2.astype(F32)), (RL_G1, g1), (RL_G2, g2),
                     (RL_R1, r1), (RL_R2, r2)):
        out = jnp.where(lane == col, val, out)
    r_ref[...] = out


def _outproj(attn_n, y_n, w_out, xf, ln_g, ln_b, w_route, b_route, layer, tm=256):
    t, d = xf.shape
    tm = min(tm, t)
    half = D_MIX // 2
    return pl.pallas_call(
        _outproj_kernel,
        grid=(t // tm,),
        in_specs=[_row_spec(tm, half), _row_spec(tm, half),
                  pl.BlockSpec((None, half, d), lambda m: (layer, 0, 0)),
                  pl.BlockSpec((None, half, d), lambda m: (layer, 1, 0)),
                  _row_spec(tm, d),
                  _layer_vec_spec(layer, d), _layer_vec_spec(layer, d),
                  pl.BlockSpec((None, d, LANES), lambda m: (layer, 0, 0)),
                  _layer_vec_spec(layer, LANES)],
        out_specs=[_row_spec(tm, d), _row_spec(tm * TOK_ROWS, LANES), _row_spec(tm, LANES),
                   pl.BlockSpec((SUBLANES, LANES), lambda m: (0, 0))],
        out_shape=[jax.ShapeDtypeStruct((t, d), BF16),
                   jax.ShapeDtypeStruct((t * TOK_ROWS, LANES), F32),
                   jax.ShapeDtypeStruct((t, LANES), F32),
                   jax.ShapeDtypeStruct((SUBLANES, LANES), F32)],
        scratch_shapes=[pltpu.VMEM((SUBLANES, LANES), F32)],
        compiler_params=_cparams(("arbitrary",)),
        name="outproj_ln_route",
    )(attn_n, y_n, w_out, w_out, xf, ln_g, ln_b, w_route, b_route)


def _token_copy(src, src_tok, dst, dst_tok, sem):
    return pltpu.make_async_copy(
        src.at[pl.ds(pl.multiple_of(src_tok * TOK_ROWS, TOK_ROWS), TOK_ROWS), :],
        dst.at[pl.ds(pl.multiple_of(dst_tok * TOK_ROWS, TOK_ROWS), TOK_ROWS), :],
        sem)


def _dispatch_kernel(d1_ref, d2_ref, x16_ref, xs_in_ref, xs_ref, sem, *, tm):
    del xs_in_ref

    def start(t, c):
        _token_copy(x16_ref, t, xs_ref, d1_ref[t], sem.at[0]).start()
        _token_copy(x16_ref, t, xs_ref, d2_ref[t], sem.at[1]).start()
        return c

    def wait(t, c):
        _token_copy(x16_ref, t, xs_ref, d1_ref[t], sem.at[0]).wait()
        _token_copy(x16_ref, t, xs_ref, d2_ref[t], sem.at[1]).wait()
        return c

    lax.fori_loop(0, tm, start, 0)
    lax.fori_loop(0, tm, wait, 0)


def _smem_tile_spec(tm):
    return pl.BlockSpec((tm,), lambda m: (m,), memory_space=pltpu.SMEM)


def _dispatch(x16, d1, d2, xs_buf, tm=256):
    t = d1.shape[0]
    tm = min(tm, t)
    return pl.pallas_call(
        functools.partial(_dispatch_kernel, tm=tm),
        grid=(t // tm,),
        in_specs=[_smem_tile_spec(tm), _smem_tile_spec(tm), _row_spec(tm * TOK_ROWS, LANES),
                  pl.BlockSpec(memory_space=pl.ANY)],
        out_specs=pl.BlockSpec(memory_space=pl.ANY),
        out_shape=jax.ShapeDtypeStruct(xs_buf.shape, F32),
        input_output_aliases={3: 0},
        scratch_shapes=[pltpu.SemaphoreType.DMA((2,))],
        compiler_params=_cparams(("arbitrary",)),
        name="moe_dispatch",
    )(d1, d2, x16, xs_buf)


def _expert_kernel(te_ref, tr_ref, xs_ref, wg_ref, wu_ref, wd_ref, ys_ref, lhs_s, *, tm):
    j = pl.program_id(0)
    rows = tr_ref[j]

    @pl.when(rows == 0)
    def _():
        ys_ref[...] = jnp.zeros_like(ys_ref)

    @pl.when(rows > 0)
    def _():
        valid = lax.broadcasted_iota(jnp.int32, (tm, LANES), 0) < rows
        for s in range(TOK_ROWS):
            blk = jnp.where(valid, _from_token_major(xs_ref, tm, s), 0.0)
            lhs_s[:, s * LANES:(s + 1) * LANES] = blk.astype(BF16)
        x = lhs_s[...]
        hg = jnp.dot(x, wg_ref[...], preferred_element_type=F32)
        hu = jnp.dot(x, wu_ref[...], preferred_element_type=F32)
        hid = (hg * jax.nn.sigmoid(hg)) * hu
        out = jnp.dot(hid.astype(BF16), wd_ref[...], preferred_element_type=F32)
        _to_token_major(ys_ref, out)


def _moe_n_tiles(t, tm):
    return (2 * t) // tm + N_EXPERTS


def _moe_tiles(counts, t, tm):
    n_tiles = _moe_n_tiles(t, tm)
    per_e = (counts + tm - 1) // tm
    ends = jnp.cumsum(per_e)
    starts = ends - per_e
    j = jnp.arange(n_tiles, dtype=jnp.int32)
    e = jnp.minimum(jnp.sum((j[:, None] >= ends[None, :]).astype(jnp.int32), axis=1), N_EXPERTS - 1)
    rows = jnp.clip(counts[e] - (j - starts[e]) * tm, 0, tm)
    rows = jnp.where(j < ends[-1], rows, 0)
    return e.astype(jnp.int32), rows.astype(jnp.int32), (starts * tm).astype(jnp.int32)


def _experts(xs, te, tr, w_gate, w_up, w_down, layer, tm):
    d = D_MODEL
    blk = pl.BlockSpec((tm * TOK_ROWS, LANES), lambda j, te, tr: (j, 0))
    grid_spec = pltpu.PrefetchScalarGridSpec(
        num_scalar_prefetch=2,
        grid=(te.shape[0],),
        in_specs=[blk,
                  pl.BlockSpec((None, None, d, D_EXPERT), lambda j, te, tr: (layer, te[j], 0, 0)),
                  pl.BlockSpec((None, None, d, D_EXPERT), lambda j, te, tr: (layer, te[j], 0, 0)),
                  pl.BlockSpec((None, None, D_EXPERT, d), lambda j, te, tr: (layer, te[j], 0, 0))],
        out_specs=blk,
        scratch_shapes=[pltpu.VMEM((tm, d), BF16)],
    )
    return pl.pallas_call(
        functools.partial(_expert_kernel, tm=tm),
        grid_spec=grid_spec,
        out_shape=jax.ShapeDtypeStruct(xs.shape, F32),
        compiler_params=_cparams(("arbitrary",)),
        name="moe_experts",
    )(te, tr, xs, w_gate, w_up, w_down)


def _combine_kernel(d1_ref, d2_ref, xb_ref, x16_ref, r_ref, p_ref, wp_ref, wg_ref, bg_ref, g_ref, b_ref,
                    ys_ref, of_ref, ob_ref, y1_s, y2_s, h_s, sem, *, tm):
    def start(t, c):
        _token_copy(ys_ref, d1_ref[t], y1_s, t, sem.at[0]).start()
        _token_copy(ys_ref, d2_ref[t], y2_s, t, sem.at[1]).start()
        return c

    def wait(t, c):
        _token_copy(ys_ref, d1_ref[t], y1_s, t, sem.at[0]).wait()
        _token_copy(ys_ref, d2_ref[t], y2_s, t, sem.at[1]).wait()
        return c

    lax.fori_loop(0, tm, start, 0)
    gate = jnp.dot(xb_ref[...], wg_ref[...], preferred_element_type=F32) + bg_ref[...]
    ple = jnp.dot(p_ref[...].astype(BF16), wp_ref[...], preferred_element_type=F32) * jax.nn.sigmoid(gate)
    lax.fori_loop(0, tm, wait, 0)
    g1 = r_ref[:, RL_G1:RL_G1 + 1]
    g2 = r_ref[:, RL_G2:RL_G2 + 1]
    for s in range(TOK_ROWS):
        sl = slice(s * LANES, (s + 1) * LANES)
        f = g1 * _from_token_major(y1_s, tm, s) + g2 * _from_token_major(y2_s, tm, s)
        h_s[:, sl] = ALPHA * _from_token_major(x16_ref, tm, s) + f + ple[:, sl]
    xn = _layer_norm(h_s[...], g_ref[...], b_ref[...])
    of_ref[...] = xn
    ob_ref[...] = xn.astype(BF16)


def _combine_ple_ln(xb, x16, route, d1, d2, ys, p, w_ple, w_pg, b_pg, ln_g, ln_b, layer, tm=256):
    t, d = xb.shape
    tm = min(tm, t)
    return pl.pallas_call(
        functools.partial(_combine_kernel, tm=tm),
        grid=(t // tm,),
        in_specs=[_smem_tile_spec(tm), _smem_tile_spec(tm),
                  _row_spec(tm, d), _row_spec(tm * TOK_ROWS, LANES), _row_spec(tm, LANES),
                  pl.BlockSpec((None, tm, PLE_DIM), lambda m: (layer, m, 0)),
                  pl.BlockSpec((None, PLE_DIM, d), lambda m: (layer, 0, 0)),
                  pl.BlockSpec((None, d, d), lambda m: (layer, 0, 0)),
                  _layer_vec_spec(layer, d), _layer_vec_spec(layer, d), _layer_vec_spec(layer, d),
                  pl.BlockSpec(memory_space=pl.ANY)],
        out_specs=[_row_spec(tm, d), _row_spec(tm, d)],
        out_shape=[jax.ShapeDtypeStruct((t, d), F32), jax.ShapeDtypeStruct((t, d), BF16)],
        scratch_shapes=[pltpu.VMEM((tm * TOK_ROWS, LANES), F32),
                        pltpu.VMEM((tm * TOK_ROWS, LANES), F32),
                        pltpu.VMEM((tm, d), F32),
                        pltpu.SemaphoreType.DMA((2,))],
        compiler_params=_cparams(("arbitrary",)),
        name="moe_combine_ple_ln",
    )(d1, d2, xb, x16, route, p, w_ple, w_pg, b_pg, ln_g, ln_b, ys)


def _rope_tables(seq_len):
    rows = seq_len // GRID_W
    row = jnp.repeat(jnp.arange(rows, dtype=F32), GRID_W)
    col = jnp.tile(jnp.arange(GRID_W, dtype=F32), rows)
    inv_freq = ROPE_THETA ** (-jnp.arange(ROPE_AXIS_PAIRS, dtype=F32) / ROPE_AXIS_PAIRS)
    ang = jnp.concatenate([row[:, None] * inv_freq, col[:, None] * inv_freq], axis=-1)
    cos, sin = jnp.cos(ang), jnp.sin(ang)
    return jnp.concatenate([cos, cos], axis=-1), jnp.concatenate([-sin, sin], axis=-1)


def kernel(x, p, ln0_g, ln0_b, w_in, q_norm_g, k_norm_g, ssm_a_re, ssm_a_im, ssm_log_dt, ssm_b_re, ssm_b_im, ssm_c_re, ssm_c_im, ssm_d, w_glu, b_glu, attn_out_g, ssm_out_g, w_out, ln1_g, ln1_b, router_group_w, router_group_b, router_expert_w, router_expert_b, w_gate_e, w_up_e, w_down_e, w_ple, w_ple_gate, b_ple_gate, ln2_g, ln2_b):
    bsz, seq_len, d = x.shape
    t = bsz * seq_len
    n_layers = w_in.shape[0]
    nc = seq_len // SSM_CHUNK
    vec = lambda a: a.reshape(n_layers, 1, a.shape[-1])

    cos2, sin2 = _rope_tables(seq_len)
    w_in_b, w_glu_b, w_out_b = w_in.astype(BF16), w_glu.astype(BF16), w_out.astype(BF16)
    w_gate_b, w_up_b, w_down_b = w_gate_e.astype(BF16), w_up_e.astype(BF16), w_down_e.astype(BF16)
    w_ple_b, w_pg_b = w_ple.astype(BF16), w_ple_gate.astype(BF16)
    pad = LANES - N_EXPERTS - N_EXPERT_GROUPS
    w_route = jnp.pad(jnp.concatenate([router_expert_w, router_group_w], axis=-1), ((0, 0), (0, 0), (0, pad)))
    b_route = jnp.pad(jnp.concatenate([router_expert_b, router_group_b], axis=-1), ((0, 0), (0, pad)))
    s5_ops = jax.vmap(_s5_operands)(ssm_a_re, ssm_a_im, ssm_log_dt, ssm_b_re, ssm_b_im,
                                    ssm_c_re, ssm_c_im, ssm_d)
    p2 = p.reshape(n_layers, t, PLE_DIM)

    tm_e = min(MOE_TM, t)
    xs = jnp.zeros((_moe_n_tiles(t, tm_e) * tm_e * TOK_ROWS, LANES), F32)
    xf, xb = _ln0(x.reshape(t, d), ln0_g, ln0_b)
    for i in range(n_layers):
        q, k, v, u = _inproj(xb, w_in_b, vec(q_norm_g), vec(k_norm_g), cos2, sin2, i, seq_len)
        attn_n = _attention(q, k, v, vec(attn_out_g), i, bsz, seq_len)
        u_chunks = (u.reshape(bsz, nc, SSM_CHUNK, SSM_GROUPS, SSM_CH)
                    .transpose(0, 3, 1, 2, 4).reshape(bsz, SSM_GROUPS, nc, SSM_CW))
        y_chunks = _s5_mixer(u_chunks, s5_ops[0][i], s5_ops[1][i], s5_ops[2][i], s5_ops[3][i])
        y = (y_chunks.reshape(bsz, SSM_GROUPS, nc, SSM_CHUNK, SSM_CH)
             .transpose(0, 2, 3, 1, 4).reshape(t, SSM_WIDTH))
        y_n = _glu(y, w_glu_b, vec(b_glu), vec(ssm_out_g), i)
        xb, x16, route, counts = _outproj(attn_n, y_n, w_out_b, xf, vec(ln1_g), vec(ln1_b),
                                          w_route, vec(b_route), i)
        te, tr, e_start = _moe_tiles(counts[0, :N_EXPERTS].astype(jnp.int32), t, tm_e)
        meta = route[:, :RL_R2 + 1].astype(jnp.int32)
        d1 = e_start[meta[:, RL_E1]] + meta[:, RL_R1]
        d2 = e_start[meta[:, RL_E2]] + meta[:, RL_R2]
        xs = _dispatch(x16, d1, d2, xs)
        ys = _experts(xs, te, tr, w_gate_b, w_up_b, w_down_b, i, tm_e)
        xf, xb = _combine_ple_ln(xb, x16, route, d1, d2, ys, p2, w_ple_b, w_pg_b,
                                 vec(b_ple_gate), vec(ln2_g), vec(ln2_b), i)
    return xf.reshape(bsz, seq_len, d)
```

```python
import functools
import math

import jax
import jax.numpy as jnp
from jax import lax
from jax.experimental import pallas as pl
from jax.experimental.pallas import tpu as pltpu

F32 = jnp.float32
BF16 = jnp.bfloat16

D_MODEL = 2048
DEPTH = 4
GRID_W = 64
N_HEADS = 8
N_KV_HEADS = 2
GQA = N_HEADS // N_KV_HEADS
HEAD_DIM = 128
ATTN_WIDTH = N_HEADS * HEAD_DIM
KV_WIDTH = N_KV_HEADS * HEAD_DIM
QK_WIDTH = ATTN_WIDTH + KV_WIDTH
ROPE_THETA = 10000.0
ROPE_AXIS_PAIRS = HEAD_DIM // 4
SSM_WIDTH = D_MODEL // 2
SSM_CH = 16
SSM_GROUPS = SSM_WIDTH // SSM_CH
SSM_STATE = 64
D_MIX = ATTN_WIDTH + SSM_WIDTH
D_IN_PROJ = ATTN_WIDTH + 2 * KV_WIDTH + SSM_WIDTH
N_EXPERT_GROUPS = 4
EXPERTS_PER_GROUP = 4
N_EXPERTS = N_EXPERT_GROUPS * EXPERTS_PER_GROUP
D_EXPERT = 512
PLE_DIM = 256
ALPHA = (2 * DEPTH) ** 0.25
EPS = 1e-6

LANES = 128
SUBLANES = 8
VMEM_LIMIT = 56 * 1024 * 1024

SSM_CHUNK = 16
SSM_CW = SSM_CHUNK * SSM_CH
SSM_GB = 8


def _cparams(sem):
    return pltpu.CompilerParams(dimension_semantics=sem, vmem_limit_bytes=VMEM_LIMIT)


def _s5_discretize(a_re, a_im, log_dt, b_re, b_im):
    dt = jnp.exp(log_dt)[:, None]
    lam_re, lam_im = dt * a_re, dt * a_im
    mag = jnp.exp(lam_re)
    abar_re, abar_im = mag * jnp.cos(lam_im), mag * jnp.sin(lam_im)
    den = a_re * a_re + a_im * a_im
    nr, ni = abar_re - 1.0, abar_im
    coef_re = (nr * a_re + ni * a_im) / den
    coef_im = (ni * a_re - nr * a_im) / den
    bb_re = coef_re[..., None] * b_re - coef_im[..., None] * b_im
    bb_im = coef_re[..., None] * b_im + coef_im[..., None] * b_re
    return lam_re, lam_im, bb_re, bb_im


def _cpow(lam_re, lam_im, k):
    mag = jnp.exp(lam_re * k)
    return mag * jnp.cos(lam_im * k), mag * jnp.sin(lam_im * k)


def _s5_operands(a_re, a_im, log_dt, b_re, b_im, c_re, c_im, d_skip):
    hp = lax.Precision.HIGHEST
    L, H = SSM_CHUNK, SSM_CH
    lam_f = _s5_discretize(a_re[0], a_im[0], log_dt[0], b_re[0], b_im[0])
    lam_b = _s5_discretize(a_re[1], a_im[1], log_dt[1], b_re[1], b_im[1])
    tau = jnp.arange(L, dtype=F32)[:, None, None]

    def conv_kernel(lam, cr, ci):
        lr, li, bbr, bbi = lam
        pr, pi = _cpow(lr[None], li[None], tau)
        wr = pr[..., None] * bbr[None] - pi[..., None] * bbi[None]
        wi = pr[..., None] * bbi[None] + pi[..., None] * bbr[None]
        return (jnp.einsum('gop,lgpi->lgoi', cr, wr, precision=hp)
                - jnp.einsum('gop,lgpi->lgoi', ci, wi, precision=hp))

    k_f = conv_kernel(lam_f, c_re[0], c_im[0])
    k_b = conv_kernel(lam_b, c_re[1], c_im[1])
    s_idx = jnp.arange(L)[:, None]
    t_idx = jnp.arange(L)[None, :]
    lag = t_idx - s_idx
    l_idx = jnp.arange(L)[None, None, :]
    sel_f = (lag[..., None] == l_idx).astype(F32)
    sel_b = (-lag[..., None] == l_idx).astype(F32)
    m = (jnp.einsum('stl,lgoi->stgoi', sel_f, k_f, precision=hp)
         + jnp.einsum('stl,lgoi->stgoi', sel_b, k_b, precision=hp))
    eye_t = (lag == 0).astype(F32)[..., None, None, None]
    eye_h = jnp.eye(H, dtype=F32)[None, None, None]
    m = m + eye_t * eye_h * d_skip[None, None, :, :, None]
    m = m.transpose(2, 0, 4, 1, 3).reshape(SSM_GROUPS, L * H, L * H)

    def end_map(lam, expo):
        lr, li, bbr, bbi = lam
        pr, pi = _cpow(lr[None], li[None], expo)
        er = pr[..., None] * bbr[None] - pi[..., None] * bbi[None]
        ei = pr[..., None] * bbi[None] + pi[..., None] * bbr[None]
        to_rows = lambda z: z.transpose(1, 0, 3, 2).reshape(SSM_GROUPS, L * H, SSM_STATE)
        return to_rows(er), to_rows(ei)

    ef_re, ef_im = end_map(lam_f, (L - 1) - tau)
    eb_re, eb_im = end_map(lam_b, tau)
    w1 = jnp.concatenate([m, ef_re, eb_re, ef_im, eb_im], axis=-1)

    def out_map(lam, cr, ci, expo):
        lr, li, _, _ = lam
        pr, pi = _cpow(lr[None], li[None], expo)
        qr = cr[None] * pr[:, :, None, :] - ci[None] * pi[:, :, None, :]
        qi = cr[None] * pi[:, :, None, :] + ci[None] * pr[:, :, None, :]
        to_cols = lambda z: z.transpose(1, 3, 0, 2).reshape(SSM_GROUPS, SSM_STATE, L * H)
        return to_cols(qr), to_cols(-qi)

    of_re, of_im = out_map(lam_f, c_re[0], c_im[0], tau + 1.0)
    ob_re, ob_im = out_map(lam_b, c_re[1], c_im[1], L - tau)
    w2 = jnp.concatenate([of_re, ob_re, of_im, ob_im], axis=1)

    afr, afi = _cpow(lam_f[0], lam_f[1], float(L))
    abr, abi = _cpow(lam_b[0], lam_b[1], float(L))
    ar = jnp.concatenate([afr, abr], axis=-1)
    ai = jnp.concatenate([afi, abi], axis=-1)
    return w1.astype(BF16), w2.astype(BF16), ar, ai


_S5_ROWS = 264


def _gelu_tanh(y):
    return 0.5 * y * (1.0 + jnp.tanh(math.sqrt(2.0 / math.pi) * (y + 0.044715 * (y * y * y))))


def _s5_perm():
    n = SSM_GB * SSM_CW
    r = lax.broadcasted_iota(jnp.int32, (n, n), 0)
    c = lax.broadcasted_iota(jnp.int32, (n, n), 1)
    r_step, r_grp, r_ch = r // LANES, (r % LANES) // SSM_CH, r % SSM_CH
    c_grp, c_step, c_ch = c // SSM_CW, (c % SSM_CW) // SSM_CH, c % SSM_CH
    return ((r_step == c_step) & (r_grp == c_grp) & (r_ch == c_ch)).astype(BF16)


def _s5_kernel(u_ref, perm_ref, w1_ref, w2_ref, ar_ref, ai_ref, y_ref,
               er_s, ei_s, fr_s, fi_s, br_s, bi_s, yi_s, yg_s, *, n_chunks):
    nc = n_chunks
    zero_tile = jnp.zeros((SUBLANES, LANES), F32)
    ug = jnp.dot(u_ref[...], perm_ref[...], preferred_element_type=F32).astype(BF16)
    for g in range(SSM_GB):
        base = g * _S5_ROWS
        r = jnp.dot(ug[:, g * SSM_CW:(g + 1) * SSM_CW], w1_ref[g], preferred_element_type=F32)
        yi_s[g] = r[:, :SSM_CW]
        er_s[pl.ds(base, nc), :] = r[:, SSM_CW:SSM_CW + LANES]
        ei_s[pl.ds(base, nc), :] = r[:, SSM_CW + LANES:]
        fr_s[pl.ds(base, SUBLANES), :] = zero_tile
        fi_s[pl.ds(base, SUBLANES), :] = zero_tile
        br_s[pl.ds(base + nc, SUBLANES), :] = zero_tile
        bi_s[pl.ds(base + nc, SUBLANES), :] = zero_tile

    ar = ar_ref[...]
    ai = ai_ref[...]
    fwd_lane = lax.broadcasted_iota(jnp.int32, (SUBLANES, LANES), 1) < SSM_STATE

    def step(i, carry):
        xr, xi = carry
        rows_f = pl.ds(i, SUBLANES, stride=_S5_ROWS)
        rows_b = pl.ds(nc - 1 - i, SUBLANES, stride=_S5_ROWS)
        er = jnp.where(fwd_lane, er_s[rows_f, :], er_s[rows_b, :])
        ei = jnp.where(fwd_lane, ei_s[rows_f, :], ei_s[rows_b, :])
        nr = ar * xr - ai * xi + er
        ni = ar * xi + ai * xr + ei
        to_f = pl.ds(i + 1, SUBLANES, stride=_S5_ROWS)
        to_b = pl.ds(nc + 6 - i, SUBLANES, stride=_S5_ROWS)
        fr_s[to_f, :] = nr
        fi_s[to_f, :] = ni
        br_s[to_b, :] = nr
        bi_s[to_b, :] = ni
        return nr, ni

    lax.fori_loop(0, nc, step, (zero_tile, zero_tile))

    fwd_col = lax.broadcasted_iota(jnp.int32, (nc, LANES), 1) < SSM_STATE
    for g in range(SSM_GB):
        base = g * _S5_ROWS
        cr = jnp.where(fwd_col, fr_s[pl.ds(base, nc), :], br_s[pl.ds(base + SUBLANES, nc), :])
        ci = jnp.where(fwd_col, fi_s[pl.ds(base, nc), :], bi_s[pl.ds(base + SUBLANES, nc), :])
        y = (yi_s[g]
             + jnp.dot(cr.astype(BF16), w2_ref[g, :LANES, :], preferred_element_type=F32)
             + jnp.dot(ci.astype(BF16), w2_ref[g, LANES:, :], preferred_element_type=F32))
        yg_s[:, g * SSM_CW:(g + 1) * SSM_CW] = _gelu_tanh(y).astype(BF16)

    z = lax.dot_general(yg_s[...], perm_ref[...], (((1,), (1,)), ((), ())), preferred_element_type=F32)
    for j in range(SSM_CHUNK):
        y_ref[pl.ds(j, nc, stride=SSM_CHUNK), :] = z[:, j * LANES:(j + 1) * LANES]


def _s5_mixer(u3, perm, w1, w2, ar, ai, bsz):
    nblk, rows, width = u3.shape
    nc = rows // bsz
    gb = SSM_GB
    cw = SSM_CW
    kern = functools.partial(_s5_kernel, n_chunks=nc)
    return pl.pallas_call(
        kern,
        grid=(bsz, nblk),
        in_specs=[
            pl.BlockSpec((None, nc, width), lambda b, j: (j, b, 0)),
            pl.BlockSpec((width, width), lambda b, j: (0, 0)),
            pl.BlockSpec((gb, cw, 2 * cw), lambda b, j: (j, 0, 0)),
            pl.BlockSpec((gb, cw, cw), lambda b, j: (j, 0, 0)),
            pl.BlockSpec((gb, LANES), lambda b, j: (j, 0)),
            pl.BlockSpec((gb, LANES), lambda b, j: (j, 0)),
        ],
        out_specs=pl.BlockSpec((nc * SSM_CHUNK, LANES), lambda b, j: (b, j)),
        out_shape=jax.ShapeDtypeStruct((bsz * nc * SSM_CHUNK, nblk * LANES), F32),
        scratch_shapes=[
            *[pltpu.VMEM((gb * _S5_ROWS, LANES), F32) for _ in range(6)],
            pltpu.VMEM((gb, nc, cw), F32),
            pltpu.VMEM((nc, width), BF16),
        ],
        compiler_params=_cparams(("parallel", "parallel")),
        name="s5_mixer",
    )(u3, perm, w1, w2, ar, ai)


def _layer_norm(h, g, b):
    mu = jnp.mean(h, axis=-1, keepdims=True)
    c = h - mu
    var = jnp.mean(c * c, axis=-1, keepdims=True)
    return c * lax.rsqrt(var + EPS) * g + b


def _ln0_kernel(x_ref, g_ref, b_ref, xf_ref, xb_ref):
    y = _layer_norm(x_ref[...], g_ref[...], b_ref[...])
    xf_ref[...] = y
    xb_ref[...] = y.astype(BF16)


def _row_spec(tm, width):
    return pl.BlockSpec((tm, width), lambda m: (m, 0))


def _layer_vec_spec(layer, width):
    return pl.BlockSpec((None, 1, width), lambda m: (layer, 0, 0))


def _ln0(x, g, b, tm=256):
    t, d = x.shape
    tm = min(tm, t)
    return pl.pallas_call(
        _ln0_kernel,
        grid=(t // tm,),
        in_specs=[_row_spec(tm, d),
                  pl.BlockSpec((1, d), lambda m: (0, 0)),
                  pl.BlockSpec((1, d), lambda m: (0, 0))],
        out_specs=[_row_spec(tm, d), _row_spec(tm, d)],
        out_shape=[jax.ShapeDtypeStruct((t, d), F32), jax.ShapeDtypeStruct((t, d), BF16)],
        compiler_params=_cparams(("parallel",)),
        name="ln0",
    )(x, g.reshape(1, d), b.reshape(1, d))


def _inproj_kernel(x_ref, w_ref, qg_ref, kg_ref, cos_ref, sin_ref, q_ref, k_ref, v_ref, u_ref, u_s):
    x = x_ref[...]
    cos2 = cos_ref[...]
    sin2 = sin_ref[...]
    scale = HEAD_DIM ** -0.5

    def head(z, gain, mult):
        zn = z * lax.rsqrt(jnp.mean(z * z, axis=-1, keepdims=True) + EPS) * gain
        rot = pltpu.roll(zn, HEAD_DIM // 2, axis=1)
        out = zn * cos2 + rot * sin2
        return out * mult if mult != 1.0 else out

    pair = 2 * HEAD_DIM
    for c in range(D_IN_PROJ // pair):
        acc = jnp.dot(x, w_ref[:, c * pair:(c + 1) * pair], preferred_element_type=F32)
        col = c * pair
        if col < ATTN_WIDTH:
            for h in range(2):
                z = head(acc[:, h * HEAD_DIM:(h + 1) * HEAD_DIM], qg_ref[...], scale)
                q_ref[:, col + h * HEAD_DIM: col + (h + 1) * HEAD_DIM] = z.astype(BF16)
        elif col < QK_WIDTH:
            for h in range(2):
                z = head(acc[:, h * HEAD_DIM:(h + 1) * HEAD_DIM], kg_ref[...], 1.0)
                k_ref[:, h * HEAD_DIM:(h + 1) * HEAD_DIM] = z.astype(BF16)
        elif col < QK_WIDTH + KV_WIDTH:
            v_ref[...] = acc.astype(BF16)
        else:
            tm = acc.shape[0]
            for half in range(2):
                blk = (col - (QK_WIDTH + KV_WIDTH)) // LANES + half
                u_s[...] = acc[:, half * LANES:(half + 1) * LANES]
                for j in range(SSM_CHUNK):
                    step_rows = u_s[pl.ds(j, tm // SSM_CHUNK, stride=SSM_CHUNK), :]
                    u_ref[blk, :, j * LANES:(j + 1) * LANES] = step_rows.astype(BF16)


def _inproj(xb, w_in, q_gain, k_gain, cos2, sin2, layer, seq_len, tm=256):
    t, d = xb.shape
    tm = min(tm, seq_len)
    sblocks = seq_len // tm
    nblk = SSM_GROUPS // SSM_GB
    return pl.pallas_call(
        _inproj_kernel,
        grid=(t // tm,),
        in_specs=[_row_spec(tm, d),
                  pl.BlockSpec((None, d, D_IN_PROJ), lambda m: (layer, 0, 0)),
                  _layer_vec_spec(layer, HEAD_DIM),
                  _layer_vec_spec(layer, HEAD_DIM),
                  pl.BlockSpec((tm, HEAD_DIM), lambda m: (m % sblocks, 0)),
                  pl.BlockSpec((tm, HEAD_DIM), lambda m: (m % sblocks, 0))],
        out_specs=[_row_spec(tm, ATTN_WIDTH), _row_spec(tm, KV_WIDTH), _row_spec(tm, KV_WIDTH),
                   pl.BlockSpec((nblk, tm // SSM_CHUNK, SSM_GB * SSM_CW), lambda m: (0, m, 0))],
        out_shape=[jax.ShapeDtypeStruct((t, ATTN_WIDTH), BF16),
                   jax.ShapeDtypeStruct((t, KV_WIDTH), BF16),
                   jax.ShapeDtypeStruct((t, KV_WIDTH), BF16),
                   jax.ShapeDtypeStruct((nblk, t // SSM_CHUNK, SSM_GB * SSM_CW), BF16)],
        scratch_shapes=[pltpu.VMEM((tm, LANES), F32)],
        compiler_params=_cparams(("parallel",)),
        name="inproj",
    )(xb, w_in, q_gain, k_gain, cos2, sin2)


def _attn_kernel(q_ref, k_ref, v_ref, g_ref, o_ref):
    outs = []
    ssq = None
    for h in range(N_HEADS):
        kv = h // GQA
        q = q_ref[:, h * HEAD_DIM:(h + 1) * HEAD_DIM]
        k = k_ref[:, kv * HEAD_DIM:(kv + 1) * HEAD_DIM]
        v = v_ref[:, kv * HEAD_DIM:(kv + 1) * HEAD_DIM]
        s = lax.dot_general(q, k, (((1,), (1,)), ((), ())), preferred_element_type=F32)
        m = jnp.max(s, axis=-1, keepdims=True)
        p = jnp.exp(s - m)
        l = jnp.sum(p, axis=-1, keepdims=True)
        o = jnp.dot(p.astype(BF16), v, preferred_element_type=F32) / l
        outs.append(o)
        sq = jnp.sum(o * o, axis=-1, keepdims=True)
        ssq = sq if ssq is None else ssq + sq
    r = lax.rsqrt(ssq * (1.0 / ATTN_WIDTH) + EPS)
    for h in range(N_HEADS):
        sl = slice(h * HEAD_DIM, (h + 1) * HEAD_DIM)
        o_ref[:, sl] = (outs[h] * r * g_ref[:, sl]).astype(BF16)


def _attention(q, k, v, gain, layer, bsz, seq_len, tq=256):
    t = q.shape[0]
    tq = min(tq, seq_len)
    qblocks = seq_len // tq
    return pl.pallas_call(
        _attn_kernel,
        grid=(bsz, qblocks),
        in_specs=[pl.BlockSpec((tq, ATTN_WIDTH), lambda b, i: (b * qblocks + i, 0)),
                  pl.BlockSpec((seq_len, KV_WIDTH), lambda b, i: (b, 0)),
                  pl.BlockSpec((seq_len, KV_WIDTH), lambda b, i: (b, 0)),
                  pl.BlockSpec((None, 1, ATTN_WIDTH), lambda b, i: (layer, 0, 0))],
        out_specs=pl.BlockSpec((tq, ATTN_WIDTH), lambda b, i: (b * qblocks + i, 0)),
        out_shape=jax.ShapeDtypeStruct((t, ATTN_WIDTH), BF16),
        compiler_params=_cparams(("parallel", "parallel")),
        name="attention",
    )(q, k, v, gain)


def _glu_kernel(y_ref, w_ref, b_ref, g_ref, o_ref):
    y = y_ref[...]
    gate = jnp.dot(y.astype(BF16), w_ref[...], preferred_element_type=F32) + b_ref[...]
    z = y * jax.nn.sigmoid(gate)
    zn = z * lax.rsqrt(jnp.mean(z * z, axis=-1, keepdims=True) + EPS) * g_ref[...]
    o_ref[...] = zn.astype(BF16)


def _glu(y, w_glu, b_glu, gain, layer, tm=512):
    t, w = y.shape
    tm = min(tm, t)
    return pl.pallas_call(
        _glu_kernel,
        grid=(t // tm,),
        in_specs=[_row_spec(tm, w),
                  pl.BlockSpec((None, w, w), lambda m: (layer, 0, 0)),
                  _layer_vec_spec(layer, w),
                  _layer_vec_spec(layer, w)],
        out_specs=_row_spec(tm, w),
        out_shape=jax.ShapeDtypeStruct((t, w), BF16),
        compiler_params=_cparams(("parallel",)),
        name="glu",
    )(y, w_glu, b_glu, gain)


ROUTE_LANE_GROUP = N_EXPERTS


def _route(logits):
    lane = lax.broadcasted_iota(jnp.int32, logits.shape, 1)
    neg = jnp.float32(-1e30)
    big = jnp.int32(LANES)

    def masked_softmax(mask):
        z = jnp.where(mask, logits, neg)
        zmax = jnp.max(z, axis=-1, keepdims=True)
        e = jnp.where(mask, jnp.exp(z - zmax), 0.0)
        return e / jnp.sum(e, axis=-1, keepdims=True)

    def first_argmax(vals, mask):
        top = jnp.max(jnp.where(mask, vals, -1.0), axis=-1, keepdims=True)
        idx = jnp.min(jnp.where(mask & (vals == top), lane, big), axis=-1, keepdims=True)
        return top, idx

    gmask = (lane >= ROUTE_LANE_GROUP) & (lane < ROUTE_LANE_GROUP + N_EXPERT_GROUPS)
    g_top, g_lane = first_argmax(masked_softmax(gmask), gmask)
    e_lo = (g_lane - ROUTE_LANE_GROUP) * EXPERTS_PER_GROUP
    emask = (lane >= e_lo) & (lane < e_lo + EXPERTS_PER_GROUP)
    e_prob = masked_softmax(emask)
    v1, i1 = first_argmax(e_prob, emask)
    v2, i2 = first_argmax(e_prob, emask & (lane != i1))
    denom = v1 + v2
    return i1, i2, g_top * (v1 / denom), g_top * (v2 / denom)


TOK_ROWS = D_MODEL // LANES
MOE_TM = 256
RL_E1, RL_E2, RL_G1, RL_G2, RL_R1, RL_R2 = 0, 1, 2, 3, 4, 5


def _to_token_major(ref, x):
    tm = x.shape[0]
    for s in range(TOK_ROWS):
        ref[pl.ds(s, tm, stride=TOK_ROWS), :] = x[:, s * LANES:(s + 1) * LANES]


def _from_token_major(ref, tm, s):
    return ref[pl.ds(s, tm, stride=TOK_ROWS), :]


def _outproj_kernel(a_ref, y_ref, wa_ref, wy_ref, x_ref, g_ref, b_ref, wr_ref, br_ref,
                    xb_ref, x16_ref, r_ref, cnt_ref, run_s):
    tm = a_ref.shape[0]

    @pl.when(pl.program_id(0) == 0)
    def _():
        run_s[...] = jnp.zeros_like(run_s)

    m = (jnp.dot(a_ref[...], wa_ref[...], preferred_element_type=F32)
         + jnp.dot(y_ref[...], wy_ref[...], preferred_element_type=F32))
    xn = _layer_norm(ALPHA * x_ref[...] + m, g_ref[...], b_ref[...])
    xb_ref[...] = xn.astype(BF16)
    _to_token_major(x16_ref, xn)
    logits = jnp.dot(xn, wr_ref[...], preferred_element_type=F32,
                     precision=lax.Precision.HIGHEST) + br_ref[...]
    i1, i2, g1, g2 = _route(logits)

    lane = lax.broadcasted_iota(jnp.int32, (tm, LANES), 1)
    hit1, hit2 = lane == i1, lane == i2
    onehot = (hit1 | hit2).astype(BF16)
    earlier = (lax.broadcasted_iota(jnp.int32, (tm, tm), 0)
               > lax.broadcasted_iota(jnp.int32, (tm, tm), 1)).astype(BF16)
    rank = jnp.dot(earlier, onehot, preferred_element_type=F32) + run_s[0:1, :]
    r1 = jnp.sum(jnp.where(hit1, rank, 0.0), axis=-1, keepdims=True)
    r2 = jnp.sum(jnp.where(hit2, rank, 0.0), axis=-1, keepdims=True)
    total = run_s[0:1, :] + jnp.sum(onehot.astype(F32), axis=0, keepdims=True)
    run_s[...] = jnp.broadcast_to(total, run_s.shape)
    cnt_ref[...] = jnp.broadcast_to(total, cnt_ref.shape)
    out = jnp.zeros((tm, LANES), F32)
    for col, val in ((RL_E1, i1.astype(F32)), (RL_E2, i2.astype(F32)), (RL_G1, g1), (RL_G2, g2),
                     (RL_R1, r1), (RL_R2, r2)):
        out = jnp.where(lane == col, val, out)
    r_ref[...] = out


def _outproj(attn_n, y_n, w_out, xf, ln_g, ln_b, w_route, b_route, layer, tm=256):
    t, d = xf.shape
    tm = min(tm, t)
    half = D_MIX // 2
    return pl.pallas_call(
        _outproj_kernel,
        grid=(t // tm,),
        in_specs=[_row_spec(tm, half), _row_spec(tm, half),
                  pl.BlockSpec((None, half, d), lambda m: (layer, 0, 0)),
                  pl.BlockSpec((None, half, d), lambda m: (layer, 1, 0)),
                  _row_spec(tm, d),
                  _layer_vec_spec(layer, d), _layer_vec_spec(layer, d),
                  pl.BlockSpec((None, d, LANES), lambda m: (layer, 0, 0)),
                  _layer_vec_spec(layer, LANES)],
        out_specs=[_row_spec(tm, d), _row_spec(tm * TOK_ROWS, LANES), _row_spec(tm, LANES),
                   pl.BlockSpec((SUBLANES, LANES), lambda m: (0, 0))],
        out_shape=[jax.ShapeDtypeStruct((t, d), BF16),
                   jax.ShapeDtypeStruct((t * TOK_ROWS, LANES), F32),
                   jax.ShapeDtypeStruct((t, LANES), F32),
                   jax.ShapeDtypeStruct((SUBLANES, LANES), F32)],
        scratch_shapes=[pltpu.VMEM((SUBLANES, LANES), F32)],
        compiler_params=_cparams(("arbitrary",)),
        name="outproj_ln_route",
    )(attn_n, y_n, w_out, w_out, xf, ln_g, ln_b, w_route, b_route)


def _token_copy(src, src_tok, dst, dst_tok, sem):
    return pltpu.make_async_copy(
        src.at[pl.ds(pl.multiple_of(src_tok * TOK_ROWS, TOK_ROWS), TOK_ROWS), :],
        dst.at[pl.ds(pl.multiple_of(dst_tok * TOK_ROWS, TOK_ROWS), TOK_ROWS), :],
        sem)


def _dispatch_kernel(d1_ref, d2_ref, x16_ref, xs_in_ref, xs_ref, sem, *, tm):
    del xs_in_ref

    def start(t, c):
        _token_copy(x16_ref, t, xs_ref, d1_ref[t], sem.at[0]).start()
        _token_copy(x16_ref, t, xs_ref, d2_ref[t], sem.at[1]).start()
        return c

    def wait(t, c):
        _token_copy(x16_ref, t, xs_ref, d1_ref[t], sem.at[0]).wait()
        _token_copy(x16_ref, t, xs_ref, d2_ref[t], sem.at[1]).wait()
        return c

    lax.fori_loop(0, tm, start, 0)
    lax.fori_loop(0, tm, wait, 0)


def _smem_tile_spec(tm):
    return pl.BlockSpec((tm,), lambda m: (m,), memory_space=pltpu.SMEM)


def _dispatch(x16, d1, d2, xs_buf, tm=256):
    t = d1.shape[0]
    tm = min(tm, t)
    return pl.pallas_call(
        functools.partial(_dispatch_kernel, tm=tm),
        grid=(t // tm,),
        in_specs=[_smem_tile_spec(tm), _smem_tile_spec(tm), _row_spec(tm * TOK_ROWS, LANES),
                  pl.BlockSpec(memory_space=pl.ANY)],
        out_specs=pl.BlockSpec(memory_space=pl.ANY),
        out_shape=jax.ShapeDtypeStruct(xs_buf.shape, F32),
        input_output_aliases={3: 0},
        scratch_shapes=[pltpu.SemaphoreType.DMA((2,))],
        compiler_params=_cparams(("arbitrary",)),
        name="moe_dispatch",
    )(d1, d2, x16, xs_buf)


def _expert_kernel(te_ref, tr_ref, xs_ref, wg_ref, wu_ref, wd_ref, ys_ref, lhs_s, wg_s, wu_s, wd_s, *, tm):
    j = pl.program_id(0)
    rows = tr_ref[j]

    @pl.when(rows == 0)
    def _():
        ys_ref[...] = jnp.zeros_like(ys_ref)

    @pl.when((rows > 0) & ((j == 0) | (te_ref[j] != te_ref[jnp.maximum(j - 1, 0)])))
    def _():
        wg_s[...] = wg_ref[...].astype(BF16)
        wu_s[...] = wu_ref[...].astype(BF16)
        wd_s[...] = wd_ref[...].astype(BF16)

    @pl.when(rows > 0)
    def _():
        valid = lax.broadcasted_iota(jnp.int32, (tm, LANES), 0) < rows
        for s in range(TOK_ROWS):
            blk = jnp.where(valid, _from_token_major(xs_ref, tm, s), 0.0)
            lhs_s[:, s * LANES:(s + 1) * LANES] = blk.astype(BF16)
        x = lhs_s[...]
        hg = jnp.dot(x, wg_s[...], preferred_element_type=F32)
        hu = jnp.dot(x, wu_s[...], preferred_element_type=F32)
        hid = (hg * jax.nn.sigmoid(hg)) * hu
        out = jnp.dot(hid.astype(BF16), wd_s[...], preferred_element_type=F32)
        _to_token_major(ys_ref, out)


def _moe_n_tiles(t, tm):
    return (2 * t) // tm + N_EXPERTS


def _moe_tiles(counts, t, tm):
    n_tiles = _moe_n_tiles(t, tm)
    per_e = (counts + tm - 1) // tm
    ends = jnp.cumsum(per_e)
    starts = ends - per_e
    j = jnp.arange(n_tiles, dtype=jnp.int32)
    e = jnp.minimum(jnp.sum((j[:, None] >= ends[None, :]).astype(jnp.int32), axis=1), N_EXPERTS - 1)
    rows = jnp.clip(counts[e] - (j - starts[e]) * tm, 0, tm)
    rows = jnp.where(j < ends[-1], rows, 0)
    return e.astype(jnp.int32), rows.astype(jnp.int32), (starts * tm).astype(jnp.int32)


def _experts(xs, te, tr, w_gate, w_up, w_down, layer, tm):
    d = D_MODEL
    blk = pl.BlockSpec((tm * TOK_ROWS, LANES), lambda j, te, tr: (j, 0))
    grid_spec = pltpu.PrefetchScalarGridSpec(
        num_scalar_prefetch=2,
        grid=(te.shape[0],),
        in_specs=[blk,
                  pl.BlockSpec((None, None, d, D_EXPERT), lambda j, te, tr: (layer, te[j], 0, 0)),
                  pl.BlockSpec((None, None, d, D_EXPERT), lambda j, te, tr: (layer, te[j], 0, 0)),
                  pl.BlockSpec((None, None, D_EXPERT, d), lambda j, te, tr: (layer, te[j], 0, 0))],
        out_specs=blk,
        scratch_shapes=[pltpu.VMEM((tm, d), BF16),
                        pltpu.VMEM((d, D_EXPERT), BF16),
                        pltpu.VMEM((d, D_EXPERT), BF16),
                        pltpu.VMEM((D_EXPERT, d), BF16)],
    )
    return pl.pallas_call(
        functools.partial(_expert_kernel, tm=tm),
        grid_spec=grid_spec,
        out_shape=jax.ShapeDtypeStruct(xs.shape, F32),
        compiler_params=_cparams(("arbitrary",)),
        name="moe_experts",
    )(te, tr, xs, w_gate, w_up, w_down)


def _combine_kernel(d1_ref, d2_ref, xb_ref, x16_ref, r_ref, p_ref, wp_ref, wg_ref, bg_ref, g_ref, b_ref,
                    ys_ref, of_ref, ob_ref, y1_s, y2_s, h_s, sem, *, tm):
    def start(t, c):
        _token_copy(ys_ref, d1_ref[t], y1_s, t, sem.at[0]).start()
        _token_copy(ys_ref, d2_ref[t], y2_s, t, sem.at[1]).start()
        return c

    def wait(t, c):
        _token_copy(ys_ref, d1_ref[t], y1_s, t, sem.at[0]).wait()
        _token_copy(ys_ref, d2_ref[t], y2_s, t, sem.at[1]).wait()
        return c

    lax.fori_loop(0, tm, start, 0)
    gate = jnp.dot(xb_ref[...], wg_ref[...], preferred_element_type=F32) + bg_ref[...]
    ple = jnp.dot(p_ref[...].astype(BF16), wp_ref[...], preferred_element_type=F32) * jax.nn.sigmoid(gate)
    lax.fori_loop(0, tm, wait, 0)
    g1 = r_ref[:, RL_G1:RL_G1 + 1]
    g2 = r_ref[:, RL_G2:RL_G2 + 1]
    for s in range(TOK_ROWS):
        sl = slice(s * LANES, (s + 1) * LANES)
        f = g1 * _from_token_major(y1_s, tm, s) + g2 * _from_token_major(y2_s, tm, s)
        h_s[:, sl] = ALPHA * _from_token_major(x16_ref, tm, s) + f + ple[:, sl]
    xn = _layer_norm(h_s[...], g_ref[...], b_ref[...])
    of_ref[...] = xn
    ob_ref[...] = xn.astype(BF16)


def _combine_ple_ln(xb, x16, route, d1, d2, ys, p, w_ple, w_pg, b_pg, ln_g, ln_b, layer, tm=256):
    t, d = xb.shape
    tm = min(tm, t)
    return pl.pallas_call(
        functools.partial(_combine_kernel, tm=tm),
        grid=(t // tm,),
        in_specs=[_smem_tile_spec(tm), _smem_tile_spec(tm),
                  _row_spec(tm, d), _row_spec(tm * TOK_ROWS, LANES), _row_spec(tm, LANES),
                  pl.BlockSpec((None, tm, PLE_DIM), lambda m: (layer, m, 0)),
                  pl.BlockSpec((None, PLE_DIM, d), lambda m: (layer, 0, 0)),
                  pl.BlockSpec((None, d, d), lambda m: (layer, 0, 0)),
                  _layer_vec_spec(layer, d), _layer_vec_spec(layer, d), _layer_vec_spec(layer, d),
                  pl.BlockSpec(memory_space=pl.ANY)],
        out_specs=[_row_spec(tm, d), _row_spec(tm, d)],
        out_shape=[jax.ShapeDtypeStruct((t, d), F32), jax.ShapeDtypeStruct((t, d), BF16)],
        scratch_shapes=[pltpu.VMEM((tm * TOK_ROWS, LANES), F32),
                        pltpu.VMEM((tm * TOK_ROWS, LANES), F32),
                        pltpu.VMEM((tm, d), F32),
                        pltpu.SemaphoreType.DMA((2,))],
        compiler_params=_cparams(("arbitrary",)),
        name="moe_combine_ple_ln",
    )(d1, d2, xb, x16, route, p, w_ple, w_pg, b_pg, ln_g, ln_b, ys)


def _rope_tables(seq_len):
    rows = seq_len // GRID_W
    row = jnp.repeat(jnp.arange(rows, dtype=F32), GRID_W)
    col = jnp.tile(jnp.arange(GRID_W, dtype=F32), rows)
    inv_freq = ROPE_THETA ** (-jnp.arange(ROPE_AXIS_PAIRS, dtype=F32) / ROPE_AXIS_PAIRS)
    ang = jnp.concatenate([row[:, None] * inv_freq, col[:, None] * inv_freq], axis=-1)
    cos, sin = jnp.cos(ang), jnp.sin(ang)
    return jnp.concatenate([cos, cos], axis=-1), jnp.concatenate([-sin, sin], axis=-1)


def kernel(x, p, ln0_g, ln0_b, w_in, q_norm_g, k_norm_g, ssm_a_re, ssm_a_im, ssm_log_dt, ssm_b_re, ssm_b_im, ssm_c_re, ssm_c_im, ssm_d, w_glu, b_glu, attn_out_g, ssm_out_g, w_out, ln1_g, ln1_b, router_group_w, router_group_b, router_expert_w, router_expert_b, w_gate_e, w_up_e, w_down_e, w_ple, w_ple_gate, b_ple_gate, ln2_g, ln2_b):
    bsz, seq_len, d = x.shape
    t = bsz * seq_len
    n_layers = w_in.shape[0]
    vec = lambda a: a.reshape(n_layers, 1, a.shape[-1])

    cos2, sin2 = _rope_tables(seq_len)
    w_in_b, w_glu_b, w_out_b = w_in.astype(BF16), w_glu.astype(BF16), w_out.astype(BF16)
    w_ple_b, w_pg_b = w_ple.astype(BF16), w_ple_gate.astype(BF16)
    pad = LANES - N_EXPERTS - N_EXPERT_GROUPS
    w_route = jnp.pad(jnp.concatenate([router_expert_w, router_group_w], axis=-1), ((0, 0), (0, 0), (0, pad)))
    b_route = jnp.pad(jnp.concatenate([router_expert_b, router_group_b], axis=-1), ((0, 0), (0, pad)))
    s5_ops = jax.vmap(_s5_operands)(ssm_a_re, ssm_a_im, ssm_log_dt, ssm_b_re, ssm_b_im,
                                    ssm_c_re, ssm_c_im, ssm_d)
    p2 = p.reshape(n_layers, t, PLE_DIM)
    perm = _s5_perm()

    tm_e = min(MOE_TM, t)
    xs = jnp.zeros((_moe_n_tiles(t, tm_e) * tm_e * TOK_ROWS, LANES), F32)
    xf, xb = _ln0(x.reshape(t, d), ln0_g, ln0_b)
    for i in range(n_layers):
        q, k, v, u = _inproj(xb, w_in_b, vec(q_norm_g), vec(k_norm_g), cos2, sin2, i, seq_len)
        attn_n = _attention(q, k, v, vec(attn_out_g), i, bsz, seq_len)
        y = _s5_mixer(u, perm, s5_ops[0][i], s5_ops[1][i], s5_ops[2][i], s5_ops[3][i], bsz)
        y_n = _glu(y, w_glu_b, vec(b_glu), vec(ssm_out_g), i)
        xb, x16, route, counts = _outproj(attn_n, y_n, w_out_b, xf, vec(ln1_g), vec(ln1_b),
                                          w_route, vec(b_route), i)
        te, tr, e_start = _moe_tiles(counts[0, :N_EXPERTS].astype(jnp.int32), t, tm_e)
        meta = route[:, :RL_R2 + 1].astype(jnp.int32)
        d1 = e_start[meta[:, RL_E1]] + meta[:, RL_R1]
        d2 = e_start[meta[:, RL_E2]] + meta[:, RL_R2]
        xs = _dispatch(x16, d1, d2, xs)
        ys = _experts(xs, te, tr, w_gate_e, w_up_e, w_down_e, i, tm_e)
        xf, xb = _combine_ple_ln(xb, x16, route, d1, d2, ys, p2, w_ple_b, w_pg_b,
                                 vec(b_ple_gate), vec(ln2_g), vec(ln2_b), i)
    return xf.reshape(bsz, seq_len, d)
```

```python
import functools
import math

import jax
import jax.numpy as jnp
from jax import lax
from jax.experimental import pallas as pl
from jax.experimental.pallas import tpu as pltpu

F32 = jnp.float32
BF16 = jnp.bfloat16

D_MODEL = 2048
DEPTH = 4
GRID_W = 64
N_HEADS = 8
N_KV_HEADS = 2
GQA = N_HEADS // N_KV_HEADS
HEAD_DIM = 128
ATTN_WIDTH = N_HEADS * HEAD_DIM
KV_WIDTH = N_KV_HEADS * HEAD_DIM
QK_WIDTH = ATTN_WIDTH + KV_WIDTH
ROPE_THETA = 10000.0
ROPE_AXIS_PAIRS = HEAD_DIM // 4
SSM_WIDTH = D_MODEL // 2
SSM_CH = 16
SSM_GROUPS = SSM_WIDTH // SSM_CH
SSM_STATE = 64
D_MIX = ATTN_WIDTH + SSM_WIDTH
D_IN_PROJ = ATTN_WIDTH + 2 * KV_WIDTH + SSM_WIDTH
N_EXPERT_GROUPS = 4
EXPERTS_PER_GROUP = 4
N_EXPERTS = N_EXPERT_GROUPS * EXPERTS_PER_GROUP
D_EXPERT = 512
PLE_DIM = 256
ALPHA = (2 * DEPTH) ** 0.25
EPS = 1e-6

LANES = 128
SUBLANES = 8
VMEM_LIMIT = 56 * 1024 * 1024

SSM_CHUNK = 16
SSM_CW = SSM_CHUNK * SSM_CH
SSM_GB = 8


def _cparams(sem):
    return pltpu.CompilerParams(dimension_semantics=sem, vmem_limit_bytes=VMEM_LIMIT)


def _s5_discretize(a_re, a_im, log_dt, b_re, b_im):
    dt = jnp.exp(log_dt)[:, None]
    lam_re, lam_im = dt * a_re, dt * a_im
    mag = jnp.exp(lam_re)
    abar_re, abar_im = mag * jnp.cos(lam_im), mag * jnp.sin(lam_im)
    den = a_re * a_re + a_im * a_im
    nr, ni = abar_re - 1.0, abar_im
    coef_re = (nr * a_re + ni * a_im) / den
    coef_im = (ni * a_re - nr * a_im) / den
    bb_re = coef_re[..., None] * b_re - coef_im[..., None] * b_im
    bb_im = coef_re[..., None] * b_im + coef_im[..., None] * b_re
    return lam_re, lam_im, bb_re, bb_im


def _cpow(lam_re, lam_im, k):
    mag = jnp.exp(lam_re * k)
    return mag * jnp.cos(lam_im * k), mag * jnp.sin(lam_im * k)


def _s5_operands(a_re, a_im, log_dt, b_re, b_im, c_re, c_im, d_skip):
    hp = lax.Precision.HIGHEST
    L, H = SSM_CHUNK, SSM_CH
    lam_f = _s5_discretize(a_re[0], a_im[0], log_dt[0], b_re[0], b_im[0])
    lam_b = _s5_discretize(a_re[1], a_im[1], log_dt[1], b_re[1], b_im[1])
    tau = jnp.arange(L, dtype=F32)[:, None, None]

    def conv_kernel(lam, cr, ci):
        lr, li, bbr, bbi = lam
        pr, pi = _cpow(lr[None], li[None], tau)
        wr = pr[..., None] * bbr[None] - pi[..., None] * bbi[None]
        wi = pr[..., None] * bbi[None] + pi[..., None] * bbr[None]
        return (jnp.einsum('gop,lgpi->lgoi', cr, wr, precision=hp)
                - jnp.einsum('gop,lgpi->lgoi', ci, wi, precision=hp))

    k_f = conv_kernel(lam_f, c_re[0], c_im[0])
    k_b = conv_kernel(lam_b, c_re[1], c_im[1])
    s_idx = jnp.arange(L)[:, None]
    t_idx = jnp.arange(L)[None, :]
    lag = t_idx - s_idx
    l_idx = jnp.arange(L)[None, None, :]
    sel_f = (lag[..., None] == l_idx).astype(F32)
    sel_b = (-lag[..., None] == l_idx).astype(F32)
    m = (jnp.einsum('stl,lgoi->stgoi', sel_f, k_f, precision=hp)
         + jnp.einsum('stl,lgoi->stgoi', sel_b, k_b, precision=hp))
    eye_t = (lag == 0).astype(F32)[..., None, None, None]
    eye_h = jnp.eye(H, dtype=F32)[None, None, None]
    m = m + eye_t * eye_h * d_skip[None, None, :, :, None]
    m = m.transpose(2, 0, 4, 1, 3).reshape(SSM_GROUPS, L * H, L * H)

    def end_map(lam, expo):
        lr, li, bbr, bbi = lam
        pr, pi = _cpow(lr[None], li[None], expo)
        er = pr[..., None] * bbr[None] - pi[..., None] * bbi[None]
        ei = pr[..., None] * bbi[None] + pi[..., None] * bbr[None]
        to_rows = lambda z: z.transpose(1, 0, 3, 2).reshape(SSM_GROUPS, L * H, SSM_STATE)
        return to_rows(er), to_rows(ei)

    ef_re, ef_im = end_map(lam_f, (L - 1) - tau)
    eb_re, eb_im = end_map(lam_b, tau)
    w1 = jnp.concatenate([m, ef_re, eb_re, ef_im, eb_im], axis=-1)

    def out_map(lam, cr, ci, expo):
        lr, li, _, _ = lam
        pr, pi = _cpow(lr[None], li[None], expo)
        qr = cr[None] * pr[:, :, None, :] - ci[None] * pi[:, :, None, :]
        qi = cr[None] * pi[:, :, None, :] + ci[None] * pr[:, :, None, :]
        to_cols = lambda z: z.transpose(1, 3, 0, 2).reshape(SSM_GROUPS, SSM_STATE, L * H)
        return to_cols(qr), to_cols(-qi)

    of_re, of_im = out_map(lam_f, c_re[0], c_im[0], tau + 1.0)
    ob_re, ob_im = out_map(lam_b, c_re[1], c_im[1], L - tau)
    w2 = jnp.concatenate([of_re, ob_re, of_im, ob_im], axis=1)

    afr, afi = _cpow(lam_f[0], lam_f[1], float(L))
    abr, abi = _cpow(lam_b[0], lam_b[1], float(L))
    ar = jnp.concatenate([afr, abr], axis=-1)
    ai = jnp.concatenate([afi, abi], axis=-1)
    return w1.astype(BF16), w2.astype(BF16), ar, ai


_S5_ROWS = 264


def _gelu_tanh(y):
    return 0.5 * y * (1.0 + jnp.tanh(math.sqrt(2.0 / math.pi) * (y + 0.044715 * (y * y * y))))


def _s5_perm():
    n = SSM_GB * SSM_CW
    r = lax.broadcasted_iota(jnp.int32, (n, n), 0)
    c = lax.broadcasted_iota(jnp.int32, (n, n), 1)
    r_step, r_grp, r_ch = r // LANES, (r % LANES) // SSM_CH, r % SSM_CH
    c_grp, c_step, c_ch = c // SSM_CW, (c % SSM_CW) // SSM_CH, c % SSM_CH
    return ((r_step == c_step) & (r_grp == c_grp) & (r_ch == c_ch)).astype(BF16)


def _s5_kernel(u_ref, perm_ref, w1_ref, w2_ref, ar_ref, ai_ref, y_ref,
               er_s, ei_s, fr_s, fi_s, br_s, bi_s, yi_s, yg_s, *, n_chunks):
    nc = n_chunks
    zero_tile = jnp.zeros((SUBLANES, LANES), F32)
    ug = jnp.dot(u_ref[...], perm_ref[...], preferred_element_type=F32).astype(BF16)
    for g in range(SSM_GB):
        base = g * _S5_ROWS
        r = jnp.dot(ug[:, g * SSM_CW:(g + 1) * SSM_CW], w1_ref[g], preferred_element_type=F32)
        yi_s[g] = r[:, :SSM_CW]
        er_s[pl.ds(base, nc), :] = r[:, SSM_CW:SSM_CW + LANES]
        ei_s[pl.ds(base, nc), :] = r[:, SSM_CW + LANES:]
        fr_s[pl.ds(base, SUBLANES), :] = zero_tile
        fi_s[pl.ds(base, SUBLANES), :] = zero_tile
        br_s[pl.ds(base + nc, SUBLANES), :] = zero_tile
        bi_s[pl.ds(base + nc, SUBLANES), :] = zero_tile

    ar = ar_ref[...]
    ai = ai_ref[...]
    fwd_lane = lax.broadcasted_iota(jnp.int32, (SUBLANES, LANES), 1) < SSM_STATE

    def step(i, carry):
        xr, xi = carry
        rows_f = pl.ds(i, SUBLANES, stride=_S5_ROWS)
        rows_b = pl.ds(nc - 1 - i, SUBLANES, stride=_S5_ROWS)
        er = jnp.where(fwd_lane, er_s[rows_f, :], er_s[rows_b, :])
        ei = jnp.where(fwd_lane, ei_s[rows_f, :], ei_s[rows_b, :])
        nr = ar * xr - ai * xi + er
        ni = ar * xi + ai * xr + ei
        to_f = pl.ds(i + 1, SUBLANES, stride=_S5_ROWS)
        to_b = pl.ds(nc + 6 - i, SUBLANES, stride=_S5_ROWS)
        fr_s[to_f, :] = nr
        fi_s[to_f, :] = ni
        br_s[to_b, :] = nr
        bi_s[to_b, :] = ni
        return nr, ni

    lax.fori_loop(0, nc, step, (zero_tile, zero_tile))

    fwd_col = lax.broadcasted_iota(jnp.int32, (nc, LANES), 1) < SSM_STATE
    for g in range(SSM_GB):
        base = g * _S5_ROWS
        cr = jnp.where(fwd_col, fr_s[pl.ds(base, nc), :], br_s[pl.ds(base + SUBLANES, nc), :])
        ci = jnp.where(fwd_col, fi_s[pl.ds(base, nc), :], bi_s[pl.ds(base + SUBLANES, nc), :])
        y = (yi_s[g]
             + jnp.dot(cr.astype(BF16), w2_ref[g, :LANES, :], preferred_element_type=F32)
             + jnp.dot(ci.astype(BF16), w2_ref[g, LANES:, :], preferred_element_type=F32))
        yg_s[:, g * SSM_CW:(g + 1) * SSM_CW] = _gelu_tanh(y).astype(BF16)

    z = lax.dot_general(yg_s[...], perm_ref[...], (((1,), (1,)), ((), ())), preferred_element_type=F32)
    for j in range(SSM_CHUNK):
        y_ref[pl.ds(j, nc, stride=SSM_CHUNK), :] = z[:, j * LANES:(j + 1) * LANES]


def _s5_mixer(u3, perm, w1, w2, ar, ai, bsz):
    nblk, rows, width = u3.shape
    nc = rows // bsz
    gb = SSM_GB
    cw = SSM_CW
    kern = functools.partial(_s5_kernel, n_chunks=nc)
    return pl.pallas_call(
        kern,
        grid=(bsz, nblk),
        in_specs=[
            pl.BlockSpec((None, nc, width), lambda b, j: (j, b, 0)),
            pl.BlockSpec((width, width), lambda b, j: (0, 0)),
            pl.BlockSpec((gb, cw, 2 * cw), lambda b, j: (j, 0, 0)),
            pl.BlockSpec((gb, cw, cw), lambda b, j: (j, 0, 0)),
            pl.BlockSpec((gb, LANES), lambda b, j: (j, 0)),
            pl.BlockSpec((gb, LANES), lambda b, j: (j, 0)),
        ],
        out_specs=pl.BlockSpec((nc * SSM_CHUNK, LANES), lambda b, j: (b, j)),
        out_shape=jax.ShapeDtypeStruct((bsz * nc * SSM_CHUNK, nblk * LANES), F32),
        scratch_shapes=[
            *[pltpu.VMEM((gb * _S5_ROWS, LANES), F32) for _ in range(6)],
            pltpu.VMEM((gb, nc, cw), F32),
            pltpu.VMEM((nc, width), BF16),
        ],
        compiler_params=_cparams(("parallel", "parallel")),
        name="s5_mixer",
    )(u3, perm, w1, w2, ar, ai)


def _layer_norm(h, g, b):
    mu = jnp.mean(h, axis=-1, keepdims=True)
    c = h - mu
    var = jnp.mean(c * c, axis=-1, keepdims=True)
    return c * lax.rsqrt(var + EPS) * g + b


def _ln0_kernel(x_ref, g_ref, b_ref, xf_ref, xb_ref):
    y = _layer_norm(x_ref[...], g_ref[...], b_ref[...])
    xf_ref[...] = y
    xb_ref[...] = y.astype(BF16)


def _row_spec(tm, width):
    return pl.BlockSpec((tm, width), lambda m: (m, 0))


def _resident_spec(block_shape, index_map):
    return pl.BlockSpec(block_shape, index_map, pipeline_mode=pl.Buffered(1))


def _layer_vec_spec(layer, width):
    return pl.BlockSpec((None, 1, width), lambda m: (layer, 0, 0))


def _ln0(x, g, b, tm=256):
    t, d = x.shape
    tm = min(tm, t)
    return pl.pallas_call(
        _ln0_kernel,
        grid=(t // tm,),
        in_specs=[_row_spec(tm, d),
                  pl.BlockSpec((1, d), lambda m: (0, 0)),
                  pl.BlockSpec((1, d), lambda m: (0, 0))],
        out_specs=[_row_spec(tm, d), _row_spec(tm, d)],
        out_shape=[jax.ShapeDtypeStruct((t, d), F32), jax.ShapeDtypeStruct((t, d), BF16)],
        compiler_params=_cparams(("parallel",)),
        name="ln0",
    )(x, g.reshape(1, d), b.reshape(1, d))


def _inproj_kernel(x_ref, w_ref, qg_ref, kg_ref, cos_ref, sin_ref, q_ref, k_ref, v_ref, u_ref, u_s):
    x = x_ref[...]
    cos2 = cos_ref[...]
    sin2 = sin_ref[...]
    scale = HEAD_DIM ** -0.5

    def head(z, gain, mult):
        zn = z * lax.rsqrt(jnp.mean(z * z, axis=-1, keepdims=True) + EPS) * gain
        rot = pltpu.roll(zn, HEAD_DIM // 2, axis=1)
        out = zn * cos2 + rot * sin2
        return out * mult if mult != 1.0 else out

    pair = 2 * HEAD_DIM
    for c in range(D_IN_PROJ // pair):
        acc = jnp.dot(x, w_ref[:, c * pair:(c + 1) * pair], preferred_element_type=F32)
        col = c * pair
        if col < ATTN_WIDTH:
            for h in range(2):
                z = head(acc[:, h * HEAD_DIM:(h + 1) * HEAD_DIM], qg_ref[...], scale)
                q_ref[:, col + h * HEAD_DIM: col + (h + 1) * HEAD_DIM] = z.astype(BF16)
        elif col < QK_WIDTH:
            for h in range(2):
                z = head(acc[:, h * HEAD_DIM:(h + 1) * HEAD_DIM], kg_ref[...], 1.0)
                k_ref[:, h * HEAD_DIM:(h + 1) * HEAD_DIM] = z.astype(BF16)
        elif col < QK_WIDTH + KV_WIDTH:
            v_ref[...] = acc.astype(BF16)
        else:
            tm = acc.shape[0]
            for half in range(2):
                blk = (col - (QK_WIDTH + KV_WIDTH)) // LANES + half
                u_s[...] = acc[:, half * LANES:(half + 1) * LANES]
                for j in range(SSM_CHUNK):
                    step_rows = u_s[pl.ds(j, tm // SSM_CHUNK, stride=SSM_CHUNK), :]
                    u_ref[blk, :, j * LANES:(j + 1) * LANES] = step_rows.astype(BF16)


def _inproj(xb, w_in, q_gain, k_gain, cos2, sin2, layer, seq_len, tm=256):
    t, d = xb.shape
    tm = min(tm, seq_len)
    sblocks = seq_len // tm
    nblk = SSM_GROUPS // SSM_GB
    return pl.pallas_call(
        _inproj_kernel,
        grid=(t // tm,),
        in_specs=[_row_spec(tm, d),
                  pl.BlockSpec((None, d, D_IN_PROJ), lambda m: (layer, 0, 0)),
                  _layer_vec_spec(layer, HEAD_DIM),
                  _layer_vec_spec(layer, HEAD_DIM),
                  pl.BlockSpec((tm, HEAD_DIM), lambda m: (m % sblocks, 0)),
                  pl.BlockSpec((tm, HEAD_DIM), lambda m: (m % sblocks, 0))],
        out_specs=[_row_spec(tm, ATTN_WIDTH), _row_spec(tm, KV_WIDTH), _row_spec(tm, KV_WIDTH),
                   pl.BlockSpec((nblk, tm // SSM_CHUNK, SSM_GB * SSM_CW), lambda m: (0, m, 0))],
        out_shape=[jax.ShapeDtypeStruct((t, ATTN_WIDTH), BF16),
                   jax.ShapeDtypeStruct((t, KV_WIDTH), BF16),
                   jax.ShapeDtypeStruct((t, KV_WIDTH), BF16),
                   jax.ShapeDtypeStruct((nblk, t // SSM_CHUNK, SSM_GB * SSM_CW), BF16)],
        scratch_shapes=[pltpu.VMEM((tm, LANES), F32)],
        compiler_params=_cparams(("parallel",)),
        name="inproj",
    )(xb, w_in, q_gain, k_gain, cos2, sin2)


def _attn_kernel(q_ref, k_ref, v_ref, g_ref, o_ref):
    outs = []
    ssq = None
    for h in range(N_HEADS):
        kv = h // GQA
        q = q_ref[:, h * HEAD_DIM:(h + 1) * HEAD_DIM]
        k = k_ref[:, kv * HEAD_DIM:(kv + 1) * HEAD_DIM]
        v = v_ref[:, kv * HEAD_DIM:(kv + 1) * HEAD_DIM]
        s = lax.dot_general(q, k, (((1,), (1,)), ((), ())), preferred_element_type=F32)
        m = jnp.max(s, axis=-1, keepdims=True)
        p = jnp.exp(s - m)
        l = jnp.sum(p, axis=-1, keepdims=True)
        o = jnp.dot(p.astype(BF16), v, preferred_element_type=F32) / l
        outs.append(o)
        sq = jnp.sum(o * o, axis=-1, keepdims=True)
        ssq = sq if ssq is None else ssq + sq
    r = lax.rsqrt(ssq * (1.0 / ATTN_WIDTH) + EPS)
    for h in range(N_HEADS):
        sl = slice(h * HEAD_DIM, (h + 1) * HEAD_DIM)
        o_ref[:, sl] = (outs[h] * r * g_ref[:, sl]).astype(BF16)


def _attention(q, k, v, gain, layer, bsz, seq_len, tq=256):
    t = q.shape[0]
    tq = min(tq, seq_len)
    qblocks = seq_len // tq
    return pl.pallas_call(
        _attn_kernel,
        grid=(bsz, qblocks),
        in_specs=[pl.BlockSpec((tq, ATTN_WIDTH), lambda b, i: (b * qblocks + i, 0)),
                  pl.BlockSpec((seq_len, KV_WIDTH), lambda b, i: (b, 0)),
                  pl.BlockSpec((seq_len, KV_WIDTH), lambda b, i: (b, 0)),
                  pl.BlockSpec((None, 1, ATTN_WIDTH), lambda b, i: (layer, 0, 0))],
        out_specs=pl.BlockSpec((tq, ATTN_WIDTH), lambda b, i: (b * qblocks + i, 0)),
        out_shape=jax.ShapeDtypeStruct((t, ATTN_WIDTH), BF16),
        compiler_params=_cparams(("parallel", "parallel")),
        name="attention",
    )(q, k, v, gain)


def _glu_kernel(y_ref, w_ref, b_ref, g_ref, o_ref):
    y = y_ref[...]
    gate = jnp.dot(y.astype(BF16), w_ref[...], preferred_element_type=F32) + b_ref[...]
    z = y * jax.nn.sigmoid(gate)
    zn = z * lax.rsqrt(jnp.mean(z * z, axis=-1, keepdims=True) + EPS) * g_ref[...]
    o_ref[...] = zn.astype(BF16)


def _glu(y, w_glu, b_glu, gain, layer, tm=512):
    t, w = y.shape
    tm = min(tm, t)
    return pl.pallas_call(
        _glu_kernel,
        grid=(t // tm,),
        in_specs=[_row_spec(tm, w),
                  pl.BlockSpec((None, w, w), lambda m: (layer, 0, 0)),
                  _layer_vec_spec(layer, w),
                  _layer_vec_spec(layer, w)],
        out_specs=_row_spec(tm, w),
        out_shape=jax.ShapeDtypeStruct((t, w), BF16),
        compiler_params=_cparams(("parallel",)),
        name="glu",
    )(y, w_glu, b_glu, gain)


ROUTE_LANE_GROUP = N_EXPERTS


def _route(logits):
    lane = lax.broadcasted_iota(jnp.int32, logits.shape, 1)
    neg = jnp.float32(-1e30)
    big = jnp.int32(LANES)

    def masked_softmax(mask):
        z = jnp.where(mask, logits, neg)
        zmax = jnp.max(z, axis=-1, keepdims=True)
        e = jnp.where(mask, jnp.exp(z - zmax), 0.0)
        return e / jnp.sum(e, axis=-1, keepdims=True)

    def first_argmax(vals, mask):
        top = jnp.max(jnp.where(mask, vals, -1.0), axis=-1, keepdims=True)
        idx = jnp.min(jnp.where(mask & (vals == top), lane, big), axis=-1, keepdims=True)
        return top, idx

    gmask = (lane >= ROUTE_LANE_GROUP) & (lane < ROUTE_LANE_GROUP + N_EXPERT_GROUPS)
    g_top, g_lane = first_argmax(masked_softmax(gmask), gmask)
    e_lo = (g_lane - ROUTE_LANE_GROUP) * EXPERTS_PER_GROUP
    emask = (lane >= e_lo) & (lane < e_lo + EXPERTS_PER_GROUP)
    e_prob = masked_softmax(emask)
    v1, i1 = first_argmax(e_prob, emask)
    v2, i2 = first_argmax(e_prob, emask & (lane != i1))
    denom = v1 + v2
    return i1, i2, g_top * (v1 / denom), g_top * (v2 / denom)


TOK_ROWS = D_MODEL // LANES
MOE_TM = 256
DMA_UNROLL = 8
RL_E1, RL_E2, RL_G1, RL_G2, RL_R1, RL_R2 = 0, 1, 2, 3, 4, 5


def _to_token_major(ref, x):
    tm = x.shape[0]
    for s in range(TOK_ROWS):
        ref[pl.ds(s, tm, stride=TOK_ROWS), :] = x[:, s * LANES:(s + 1) * LANES]


def _from_token_major(ref, tm, s):
    return ref[pl.ds(s, tm, stride=TOK_ROWS), :]


def _outproj_kernel(a_ref, y_ref, wa_ref, wy_ref, x_ref, g_ref, b_ref, wr_ref, br_ref,
                    xb_ref, x16_ref, r_ref, cnt_ref, run_s):
    tm = a_ref.shape[0]

    @pl.when(pl.program_id(0) == 0)
    def _():
        run_s[...] = jnp.zeros_like(run_s)

    m = (jnp.dot(a_ref[...], wa_ref[...], preferred_element_type=F32)
         + jnp.dot(y_ref[...], wy_ref[...], preferred_element_type=F32))
    xn = _layer_norm(ALPHA * x_ref[...] + m, g_ref[...], b_ref[...])
    xn_hi = xn.astype(BF16)
    xb_ref[...] = xn_hi
    _to_token_major(x16_ref, xn)
    xn_lo = (xn - xn_hi.astype(F32)).astype(BF16)
    logits = (jnp.dot(xn_hi, wr_ref[0], preferred_element_type=F32)
              + jnp.dot(xn_hi, wr_ref[1], preferred_element_type=F32)
              + jnp.dot(xn_lo, wr_ref[0], preferred_element_type=F32)) + br_ref[...]
    i1, i2, g1, g2 = _route(logits)

    lane = lax.broadcasted_iota(jnp.int32, (tm, LANES), 1)
    hit1, hit2 = lane == i1, lane == i2
    onehot = (hit1 | hit2).astype(BF16)
    earlier = (lax.broadcasted_iota(jnp.int32, (tm, tm), 0)
               > lax.broadcasted_iota(jnp.int32, (tm, tm), 1)).astype(BF16)
    rank = jnp.dot(earlier, onehot, preferred_element_type=F32) + run_s[0:1, :]
    r1 = jnp.sum(jnp.where(hit1, rank, 0.0), axis=-1, keepdims=True)
    r2 = jnp.sum(jnp.where(hit2, rank, 0.0), axis=-1, keepdims=True)
    total = run_s[0:1, :] + jnp.sum(onehot.astype(F32), axis=0, keepdims=True)
    run_s[...] = jnp.broadcast_to(total, run_s.shape)
    cnt_ref[...] = jnp.broadcast_to(total, cnt_ref.shape)
    out = jnp.zeros((tm, LANES), F32)
    for col, val in ((RL_E1, i1.astype(F32)), (RL_E2, i2.astype(F32)), (RL_G1, g1), (RL_G2, g2),
                     (RL_R1, r1), (RL_R2, r2)):
        out = jnp.where(lane == col, val, out)
    r_ref[...] = out


def _outproj(attn_n, y_n, w_out, xf, ln_g, ln_b, w_route, b_route, layer, tm=512):
    t, d = xf.shape
    tm = min(tm, t)
    half = D_MIX // 2
    return pl.pallas_call(
        _outproj_kernel,
        grid=(t // tm,),
        in_specs=[_row_spec(tm, half), _row_spec(tm, half),
                  _resident_spec((None, half, d), lambda m: (layer, 0, 0)),
                  _resident_spec((None, half, d), lambda m: (layer, 1, 0)),
                  _row_spec(tm, d),
                  _layer_vec_spec(layer, d), _layer_vec_spec(layer, d),
                  _resident_spec((None, 2, d, LANES), lambda m: (layer, 0, 0, 0)),
                  _layer_vec_spec(layer, LANES)],
        out_specs=[_row_spec(tm, d), _row_spec(tm * TOK_ROWS, LANES), _row_spec(tm, LANES),
                   pl.BlockSpec((SUBLANES, LANES), lambda m: (0, 0))],
        out_shape=[jax.ShapeDtypeStruct((t, d), BF16),
                   jax.ShapeDtypeStruct((t * TOK_ROWS, LANES), F32),
                   jax.ShapeDtypeStruct((t, LANES), F32),
                   jax.ShapeDtypeStruct((SUBLANES, LANES), F32)],
        scratch_shapes=[pltpu.VMEM((SUBLANES, LANES), F32)],
        compiler_params=_cparams(("arbitrary",)),
        name="outproj_ln_route",
    )(attn_n, y_n, w_out, w_out, xf, ln_g, ln_b, w_route, b_route)


def _token_copy(src, src_tok, dst, dst_tok, sem):
    return pltpu.make_async_copy(
        src.at[pl.ds(pl.multiple_of(src_tok * TOK_ROWS, TOK_ROWS), TOK_ROWS), :],
        dst.at[pl.ds(pl.multiple_of(dst_tok * TOK_ROWS, TOK_ROWS), TOK_ROWS), :],
        sem)


def _dispatch_kernel(d1_ref, d2_ref, x16_ref, xs_in_ref, xs_ref, sem, *, tm):
    del xs_in_ref

    def start(t, c):
        _token_copy(x16_ref, t, xs_ref, d1_ref[t], sem.at[0]).start()
        _token_copy(x16_ref, t, xs_ref, d2_ref[t], sem.at[1]).start()
        return c

    lax.fori_loop(0, tm, start, 0, unroll=DMA_UNROLL)
    whole = pl.ds(0, tm * TOK_ROWS)
    pltpu.make_async_copy(x16_ref, xs_ref.at[whole, :], sem.at[0]).wait()
    pltpu.make_async_copy(x16_ref, xs_ref.at[whole, :], sem.at[1]).wait()


def _smem_tile_spec(tm):
    return pl.BlockSpec((tm,), lambda m: (m,), memory_space=pltpu.SMEM)


def _dispatch(x16, d1, d2, xs_buf, tm=256):
    t = d1.shape[0]
    tm = min(tm, t)
    return pl.pallas_call(
        functools.partial(_dispatch_kernel, tm=tm),
        grid=(t // tm,),
        in_specs=[_smem_tile_spec(tm), _smem_tile_spec(tm), _row_spec(tm * TOK_ROWS, LANES),
                  pl.BlockSpec(memory_space=pl.ANY)],
        out_specs=pl.BlockSpec(memory_space=pl.ANY),
        out_shape=jax.ShapeDtypeStruct(xs_buf.shape, F32),
        input_output_aliases={3: 0},
        scratch_shapes=[pltpu.SemaphoreType.DMA((2,))],
        compiler_params=_cparams(("arbitrary",)),
        name="moe_dispatch",
    )(d1, d2, x16, xs_buf)


def _expert_kernel(te_ref, tr_ref, xs_ref, wg_ref, wu_ref, wd_ref, ys_ref, lhs_s, wg_s, wu_s, wd_s, *, tm):
    j = pl.program_id(0)
    rows = tr_ref[j]

    @pl.when(rows == 0)
    def _():
        ys_ref[...] = jnp.zeros_like(ys_ref)

    @pl.when((rows > 0) & ((j == 0) | (te_ref[j] != te_ref[jnp.maximum(j - 1, 0)])))
    def _():
        wg_s[...] = wg_ref[...].astype(BF16)
        wu_s[...] = wu_ref[...].astype(BF16)
        wd_s[...] = wd_ref[...].astype(BF16)

    @pl.when(rows > 0)
    def _():
        valid = lax.broadcasted_iota(jnp.int32, (tm, LANES), 0) < rows
        for s in range(TOK_ROWS):
            blk = jnp.where(valid, _from_token_major(xs_ref, tm, s), 0.0)
            lhs_s[:, s * LANES:(s + 1) * LANES] = blk.astype(BF16)
        x = lhs_s[...]
        hg = jnp.dot(x, wg_s[...], preferred_element_type=F32)
        hu = jnp.dot(x, wu_s[...], preferred_element_type=F32)
        hid = ((hg * jax.nn.sigmoid(hg)) * hu).astype(BF16)
        for c in range(D_MODEL // D_EXPERT):
            out = jnp.dot(hid, wd_s[:, c * D_EXPERT:(c + 1) * D_EXPERT], preferred_element_type=F32)
            for h in range(D_EXPERT // LANES):
                s = c * (D_EXPERT // LANES) + h
                ys_ref[pl.ds(s, tm, stride=TOK_ROWS), :] = out[:, h * LANES:(h + 1) * LANES]


def _moe_n_tiles(t, tm):
    return (2 * t) // tm + N_EXPERTS


def _moe_tiles(counts, t, tm):
    n_tiles = _moe_n_tiles(t, tm)
    per_e = (counts + tm - 1) // tm
    ends = jnp.cumsum(per_e)
    starts = ends - per_e
    j = jnp.arange(n_tiles, dtype=jnp.int32)
    e = jnp.minimum(jnp.sum((j[:, None] >= ends[None, :]).astype(jnp.int32), axis=1), N_EXPERTS - 1)
    rows = jnp.clip(counts[e] - (j - starts[e]) * tm, 0, tm)
    rows = jnp.where(j < ends[-1], rows, 0)
    return e.astype(jnp.int32), rows.astype(jnp.int32), (starts * tm).astype(jnp.int32)


def _experts(xs, te, tr, w_gate, w_up, w_down, layer, tm):
    d = D_MODEL
    blk = pl.BlockSpec((tm * TOK_ROWS, LANES), lambda j, te, tr: (j, 0))
    grid_spec = pltpu.PrefetchScalarGridSpec(
        num_scalar_prefetch=2,
        grid=(te.shape[0],),
        in_specs=[blk,
                  pl.BlockSpec((None, None, d, D_EXPERT), lambda j, te, tr: (layer, te[j], 0, 0)),
                  pl.BlockSpec((None, None, d, D_EXPERT), lambda j, te, tr: (layer, te[j], 0, 0)),
                  pl.BlockSpec((None, None, D_EXPERT, d), lambda j, te, tr: (layer, te[j], 0, 0))],
        out_specs=blk,
        scratch_shapes=[pltpu.VMEM((tm, d), BF16),
                        pltpu.VMEM((d, D_EXPERT), BF16),
                        pltpu.VMEM((d, D_EXPERT), BF16),
                        pltpu.VMEM((D_EXPERT, d), BF16)],
    )
    return pl.pallas_call(
        functools.partial(_expert_kernel, tm=tm),
        grid_spec=grid_spec,
        out_shape=jax.ShapeDtypeStruct(xs.shape, F32),
        compiler_params=_cparams(("arbitrary",)),
        name="moe_experts",
    )(te, tr, xs, w_gate, w_up, w_down)


def _combine_kernel(d1_ref, d2_ref, xb_ref, x16_ref, r_ref, p_ref, wp_ref, wg_ref, bg_ref, g_ref, b_ref,
                    ys_ref, of_ref, ob_ref, y1_s, y2_s, h_s, sem, *, tm):
    def start(t, c):
        _token_copy(ys_ref, d1_ref[t], y1_s, t, sem.at[0]).start()
        _token_copy(ys_ref, d2_ref[t], y2_s, t, sem.at[1]).start()
        return c

    lax.fori_loop(0, tm, start, 0, unroll=DMA_UNROLL)
    gate = jnp.dot(xb_ref[...], wg_ref[...], preferred_element_type=F32) + bg_ref[...]
    ple = jnp.dot(p_ref[...].astype(BF16), wp_ref[...], preferred_element_type=F32) * jax.nn.sigmoid(gate)
    whole = pl.ds(0, tm * TOK_ROWS)
    pltpu.make_async_copy(ys_ref.at[whole, :], y1_s, sem.at[0]).wait()
    pltpu.make_async_copy(ys_ref.at[whole, :], y2_s, sem.at[1]).wait()
    g1 = r_ref[:, RL_G1:RL_G1 + 1]
    g2 = r_ref[:, RL_G2:RL_G2 + 1]
    for s in range(TOK_ROWS):
        sl = slice(s * LANES, (s + 1) * LANES)
        f = g1 * _from_token_major(y1_s, tm, s) + g2 * _from_token_major(y2_s, tm, s)
        h_s[:, sl] = ALPHA * _from_token_major(x16_ref, tm, s) + f + ple[:, sl]
    xn = _layer_norm(h_s[...], g_ref[...], b_ref[...])
    of_ref[...] = xn
    ob_ref[...] = xn.astype(BF16)


def _combine_ple_ln(xb, x16, route, d1, d2, ys, p, w_ple, w_pg, b_pg, ln_g, ln_b, layer, tm=512):
    t, d = xb.shape
    tm = min(tm, t)
    return pl.pallas_call(
        functools.partial(_combine_kernel, tm=tm),
        grid=(t // tm,),
        in_specs=[_smem_tile_spec(tm), _smem_tile_spec(tm),
                  _row_spec(tm, d), _row_spec(tm * TOK_ROWS, LANES), _row_spec(tm, LANES),
                  pl.BlockSpec((None, tm, PLE_DIM), lambda m: (layer, m, 0)),
                  _resident_spec((None, PLE_DIM, d), lambda m: (layer, 0, 0)),
                  _resident_spec((None, d, d), lambda m: (layer, 0, 0)),
                  _layer_vec_spec(layer, d), _layer_vec_spec(layer, d), _layer_vec_spec(layer, d),
                  pl.BlockSpec(memory_space=pl.ANY)],
        out_specs=[_row_spec(tm, d), _row_spec(tm, d)],
        out_shape=[jax.ShapeDtypeStruct((t, d), F32), jax.ShapeDtypeStruct((t, d), BF16)],
        scratch_shapes=[pltpu.VMEM((tm * TOK_ROWS, LANES), F32),
                        pltpu.VMEM((tm * TOK_ROWS, LANES), F32),
                        pltpu.VMEM((tm, d), F32),
                        pltpu.SemaphoreType.DMA((2,))],
        compiler_params=_cparams(("arbitrary",)),
        name="moe_combine_ple_ln",
    )(d1, d2, xb, x16, route, p, w_ple, w_pg, b_pg, ln_g, ln_b, ys)


def _rope_tables(seq_len):
    rows = seq_len // GRID_W
    row = jnp.repeat(jnp.arange(rows, dtype=F32), GRID_W)
    col = jnp.tile(jnp.arange(GRID_W, dtype=F32), rows)
    inv_freq = ROPE_THETA ** (-jnp.arange(ROPE_AXIS_PAIRS, dtype=F32) / ROPE_AXIS_PAIRS)
    ang = jnp.concatenate([row[:, None] * inv_freq, col[:, None] * inv_freq], axis=-1)
    cos, sin = jnp.cos(ang), jnp.sin(ang)
    return jnp.concatenate([cos, cos], axis=-1), jnp.concatenate([-sin, sin], axis=-1)


def kernel(x, p, ln0_g, ln0_b, w_in, q_norm_g, k_norm_g, ssm_a_re, ssm_a_im, ssm_log_dt, ssm_b_re, ssm_b_im, ssm_c_re, ssm_c_im, ssm_d, w_glu, b_glu, attn_out_g, ssm_out_g, w_out, ln1_g, ln1_b, router_group_w, router_group_b, router_expert_w, router_expert_b, w_gate_e, w_up_e, w_down_e, w_ple, w_ple_gate, b_ple_gate, ln2_g, ln2_b):
    bsz, seq_len, d = x.shape
    t = bsz * seq_len
    n_layers = w_in.shape[0]
    vec = lambda a: a.reshape(n_layers, 1, a.shape[-1])

    cos2, sin2 = _rope_tables(seq_len)
    w_in_b, w_glu_b, w_out_b = w_in.astype(BF16), w_glu.astype(BF16), w_out.astype(BF16)
    w_ple_b, w_pg_b = w_ple.astype(BF16), w_ple_gate.astype(BF16)
    pad = LANES - N_EXPERTS - N_EXPERT_GROUPS
    w_route = jnp.pad(jnp.concatenate([router_expert_w, router_group_w], axis=-1), ((0, 0), (0, 0), (0, pad)))
    w_route_hi = w_route.astype(BF16)
    w_route_lo = (w_route - w_route_hi.astype(F32)).astype(BF16)
    w_route = jnp.stack([w_route_hi, w_route_lo], axis=1)
    b_route = jnp.pad(jnp.concatenate([router_expert_b, router_group_b], axis=-1), ((0, 0), (0, pad)))
    s5_ops = jax.vmap(_s5_operands)(ssm_a_re, ssm_a_im, ssm_log_dt, ssm_b_re, ssm_b_im,
                                    ssm_c_re, ssm_c_im, ssm_d)
    p2 = p.reshape(n_layers, t, PLE_DIM)
    perm = _s5_perm()

    tm_e = min(MOE_TM, t)
    xs = jnp.zeros((_moe_n_tiles(t, tm_e) * tm_e * TOK_ROWS, LANES), F32)
    xf, xb = _ln0(x.reshape(t, d), ln0_g, ln0_b)
    for i in range(n_layers):
        q, k, v, u = _inproj(xb, w_in_b, vec(q_norm_g), vec(k_norm_g), cos2, sin2, i, seq_len)
        attn_n = _attention(q, k, v, vec(attn_out_g), i, bsz, seq_len)
        y = _s5_mixer(u, perm, s5_ops[0][i], s5_ops[1][i], s5_ops[2][i], s5_ops[3][i], bsz)
        y_n = _glu(y, w_glu_b, vec(b_glu), vec(ssm_out_g), i)
        xb, x16, route, counts = _outproj(attn_n, y_n, w_out_b, xf, vec(ln1_g), vec(ln1_b),
                                          w_route, vec(b_route), i)
        te, tr, e_start = _moe_tiles(counts[0, :N_EXPERTS].astype(jnp.int32), t, tm_e)
        meta = route[:, :RL_R2 + 1].astype(jnp.int32)
        d1 = e_start[meta[:, RL_E1]] + meta[:, RL_R1]
        d2 = e_start[meta[:, RL_E2]] + meta[:, RL_R2]
        xs = _dispatch(x16, d1, d2, xs)
        ys = _experts(xs, te, tr, w_gate_e, w_up_e, w_down_e, i, tm_e)
        xf, xb = _combine_ple_ln(xb, x16, route, d1, d2, ys, p2, w_ple_b, w_pg_b,
                                 vec(b_ple_gate), vec(ln2_g), vec(ln2_b), i)
    return xf.reshape(bsz, seq_len, d)
```

```python
import functools
import math

import jax
import jax.numpy as jnp
from jax import lax
from jax.experimental import pallas as pl
from jax.experimental.pallas import tpu as pltpu

F32 = jnp.float32
BF16 = jnp.bfloat16

D_MODEL = 2048
DEPTH = 4
GRID_W = 64
N_HEADS = 8
N_KV_HEADS = 2
GQA = N_HEADS // N_KV_HEADS
HEAD_DIM = 128
ATTN_WIDTH = N_HEADS * HEAD_DIM
KV_WIDTH = N_KV_HEADS * HEAD_DIM
QK_WIDTH = ATTN_WIDTH + KV_WIDTH
ROPE_THETA = 10000.0
ROPE_AXIS_PAIRS = HEAD_DIM // 4
SSM_WIDTH = D_MODEL // 2
SSM_CH = 16
SSM_GROUPS = SSM_WIDTH // SSM_CH
SSM_STATE = 64
D_MIX = ATTN_WIDTH + SSM_WIDTH
D_IN_PROJ = ATTN_WIDTH + 2 * KV_WIDTH + SSM_WIDTH
N_EXPERT_GROUPS = 4
EXPERTS_PER_GROUP = 4
N_EXPERTS = N_EXPERT_GROUPS * EXPERTS_PER_GROUP
D_EXPERT = 512
PLE_DIM = 256
ALPHA = (2 * DEPTH) ** 0.25
EPS = 1e-6

LANES = 128
SUBLANES = 8
VMEM_LIMIT = 56 * 1024 * 1024

SSM_CHUNK = 16
SSM_CW = SSM_CHUNK * SSM_CH
SSM_GB = 8


def _cparams(sem):
    return pltpu.CompilerParams(dimension_semantics=sem, vmem_limit_bytes=VMEM_LIMIT)


def _s5_discretize(a_re, a_im, log_dt, b_re, b_im):
    dt = jnp.exp(log_dt)[:, None]
    lam_re, lam_im = dt * a_re, dt * a_im
    mag = jnp.exp(lam_re)
    abar_re, abar_im = mag * jnp.cos(lam_im), mag * jnp.sin(lam_im)
    den = a_re * a_re + a_im * a_im
    nr, ni = abar_re - 1.0, abar_im
    coef_re = (nr * a_re + ni * a_im) / den
    coef_im = (ni * a_re - nr * a_im) / den
    bb_re = coef_re[..., None] * b_re - coef_im[..., None] * b_im
    bb_im = coef_re[..., None] * b_im + coef_im[..., None] * b_re
    return lam_re, lam_im, bb_re, bb_im


def _cpow(lam_re, lam_im, k):
    mag = jnp.exp(lam_re * k)
    return mag * jnp.cos(lam_im * k), mag * jnp.sin(lam_im * k)


def _s5_operands(a_re, a_im, log_dt, b_re, b_im, c_re, c_im, d_skip):
    hp = lax.Precision.HIGHEST
    L, H, G = SSM_CHUNK, SSM_CH, SSM_GROUPS
    lam_f = _s5_discretize(a_re[0], a_im[0], log_dt[0], b_re[0], b_im[0])
    lam_b = _s5_discretize(a_re[1], a_im[1], log_dt[1], b_re[1], b_im[1])
    tau = jnp.arange(L, dtype=F32)

    def powers(lam, expo):
        return _cpow(lam[0][:, None, :], lam[1][:, None, :], expo[None, :, None])

    def conv_kernel(lam, cr, ci):
        pr, pi = powers(lam, tau)
        bbr, bbi = lam[2], lam[3]
        wr = pr[..., None] * bbr[:, None] - pi[..., None] * bbi[:, None]
        wi = pr[..., None] * bbi[:, None] + pi[..., None] * bbr[:, None]
        return (jnp.einsum('gop,glpi->gloi', cr, wr, precision=hp)
                - jnp.einsum('gop,glpi->gloi', ci, wi, precision=hp))

    k_f = conv_kernel(lam_f, c_re[0], c_im[0])
    k_b = conv_kernel(lam_b, c_re[1], c_im[1])
    k_f = k_f.at[:, 0].add(d_skip[:, :, None] * jnp.eye(H, dtype=F32))
    lag = jnp.arange(L)[None, :] - jnp.arange(L)[:, None]
    l_idx = jnp.arange(L)[None, None, :]
    sel = jnp.concatenate([(lag[..., None] == l_idx), (-lag[..., None] == l_idx)], axis=-1).astype(F32)
    wm = jnp.einsum('stl,gloi->gsito', sel, jnp.concatenate([k_f, k_b], axis=1), precision=hp)
    wm = wm.reshape(G, L * H, L * H)

    def end_map(lam, expo):
        pr, pi = powers(lam, expo)
        bbr, bbi = lam[2].transpose(0, 2, 1)[:, None], lam[3].transpose(0, 2, 1)[:, None]
        pr, pi = pr[:, :, None, :], pi[:, :, None, :]
        return pr * bbr - pi * bbi, pr * bbi + pi * bbr

    ef_re, ef_im = end_map(lam_f, (L - 1) - tau)
    eb_re, eb_im = end_map(lam_b, tau)
    we = jnp.concatenate([ef_re, eb_re, ef_im, eb_im], axis=-1).reshape(G, L * H, 4 * SSM_STATE)

    def out_map(lam, cr, ci, expo):
        pr, pi = powers(lam, expo)
        pr, pi = pr.transpose(0, 2, 1)[..., None], pi.transpose(0, 2, 1)[..., None]
        cr, ci = cr.transpose(0, 2, 1)[:, :, None, :], ci.transpose(0, 2, 1)[:, :, None, :]
        return cr * pr - ci * pi, -(cr * pi + ci * pr)

    of_re, of_im = out_map(lam_f, c_re[0], c_im[0], tau + 1.0)
    ob_re, ob_im = out_map(lam_b, c_re[1], c_im[1], L - tau)
    w2 = jnp.concatenate([of_re, ob_re, of_im, ob_im], axis=1).reshape(G, 4 * SSM_STATE, L * H)

    afr, afi = _cpow(lam_f[0], lam_f[1], float(L))
    abr, abi = _cpow(lam_b[0], lam_b[1], float(L))
    ar = jnp.concatenate([afr, abr], axis=-1)
    ai = jnp.concatenate([afi, abi], axis=-1)
    return wm.astype(BF16), we.astype(BF16), w2.astype(BF16), ar, ai


_S5_ROWS = 264


def _gelu_tanh(y):
    return 0.5 * y * (1.0 + jnp.tanh(math.sqrt(2.0 / math.pi) * (y + 0.044715 * (y * y * y))))


def _s5_perm():
    n = SSM_GB * SSM_CW
    r = lax.broadcasted_iota(jnp.int32, (n, n), 0)
    c = lax.broadcasted_iota(jnp.int32, (n, n), 1)
    r_step, r_grp, r_ch = r // LANES, (r % LANES) // SSM_CH, r % SSM_CH
    c_grp, c_step, c_ch = c // SSM_CW, (c % SSM_CW) // SSM_CH, c % SSM_CH
    return ((r_step == c_step) & (r_grp == c_grp) & (r_ch == c_ch)).astype(BF16)


def _s5_kernel(u_ref, perm_ref, wm_ref, we_ref, w2_ref, ar_ref, ai_ref, y_ref,
               er_s, ei_s, fr_s, fi_s, br_s, bi_s, yi_s, yg_s, *, n_chunks):
    nc = n_chunks
    zero_tile = jnp.zeros((SUBLANES, LANES), F32)
    ug = jnp.dot(u_ref[...], perm_ref[...], preferred_element_type=F32).astype(BF16)
    for g in range(SSM_GB):
        base = g * _S5_ROWS
        u_g = ug[:, g * SSM_CW:(g + 1) * SSM_CW]
        yi_s[g] = jnp.dot(u_g, wm_ref[g], preferred_element_type=F32)
        ends = jnp.dot(u_g, we_ref[g], preferred_element_type=F32)
        er_s[pl.ds(base, nc), :] = ends[:, :LANES]
        ei_s[pl.ds(base, nc), :] = ends[:, LANES:]
        fr_s[pl.ds(base, SUBLANES), :] = zero_tile
        fi_s[pl.ds(base, SUBLANES), :] = zero_tile
        br_s[pl.ds(base + nc, SUBLANES), :] = zero_tile
        bi_s[pl.ds(base + nc, SUBLANES), :] = zero_tile

    ar = ar_ref[...]
    ai = ai_ref[...]
    fwd_lane = lax.broadcasted_iota(jnp.int32, (SUBLANES, LANES), 1) < SSM_STATE

    def step(i, carry):
        xr, xi = carry
        rows_f = pl.ds(i, SUBLANES, stride=_S5_ROWS)
        rows_b = pl.ds(nc - 1 - i, SUBLANES, stride=_S5_ROWS)
        er = jnp.where(fwd_lane, er_s[rows_f, :], er_s[rows_b, :])
        ei = jnp.where(fwd_lane, ei_s[rows_f, :], ei_s[rows_b, :])
        nr = ar * xr - ai * xi + er
        ni = ar * xi + ai * xr + ei
        to_f = pl.ds(i + 1, SUBLANES, stride=_S5_ROWS)
        to_b = pl.ds(nc + 6 - i, SUBLANES, stride=_S5_ROWS)
        fr_s[to_f, :] = nr
        fi_s[to_f, :] = ni
        br_s[to_b, :] = nr
        bi_s[to_b, :] = ni
        return nr, ni

    lax.fori_loop(0, nc, step, (zero_tile, zero_tile))

    fwd_col = lax.broadcasted_iota(jnp.int32, (nc, LANES), 1) < SSM_STATE
    for g in range(SSM_GB):
        base = g * _S5_ROWS
        cr = jnp.where(fwd_col, fr_s[pl.ds(base, nc), :], br_s[pl.ds(base + SUBLANES, nc), :])
        ci = jnp.where(fwd_col, fi_s[pl.ds(base, nc), :], bi_s[pl.ds(base + SUBLANES, nc), :])
        y = (yi_s[g]
             + jnp.dot(cr.astype(BF16), w2_ref[g, :LANES, :], preferred_element_type=F32)
             + jnp.dot(ci.astype(BF16), w2_ref[g, LANES:, :], preferred_element_type=F32))
        yg_s[:, g * SSM_CW:(g + 1) * SSM_CW] = _gelu_tanh(y).astype(BF16)

    z = lax.dot_general(yg_s[...], perm_ref[...], (((1,), (1,)), ((), ())), preferred_element_type=F32)
    for j in range(SSM_CHUNK):
        y_ref[pl.ds(j, nc, stride=SSM_CHUNK), :] = z[:, j * LANES:(j + 1) * LANES]


def _s5_mixer(u3, perm, wm, we, w2, ar, ai, bsz):
    nblk, rows, width = u3.shape
    nc = rows // bsz
    gb = SSM_GB
    cw = SSM_CW
    kern = functools.partial(_s5_kernel, n_chunks=nc)
    return pl.pallas_call(
        kern,
        grid=(bsz, nblk),
        in_specs=[
            pl.BlockSpec((None, nc, width), lambda b, j: (j, b, 0)),
            pl.BlockSpec((width, width), lambda b, j: (0, 0)),
            pl.BlockSpec((gb, cw, cw), lambda b, j: (j, 0, 0)),
            pl.BlockSpec((gb, cw, cw), lambda b, j: (j, 0, 0)),
            pl.BlockSpec((gb, cw, cw), lambda b, j: (j, 0, 0)),
            pl.BlockSpec((gb, LANES), lambda b, j: (j, 0)),
            pl.BlockSpec((gb, LANES), lambda b, j: (j, 0)),
        ],
        out_specs=pl.BlockSpec((nc * SSM_CHUNK, LANES), lambda b, j: (b, j)),
        out_shape=jax.ShapeDtypeStruct((bsz * nc * SSM_CHUNK, nblk * LANES), F32),
        scratch_shapes=[
            *[pltpu.VMEM((gb * _S5_ROWS, LANES), F32) for _ in range(6)],
            pltpu.VMEM((gb, nc, cw), F32),
            pltpu.VMEM((nc, width), BF16),
        ],
        compiler_params=_cparams(("parallel", "parallel")),
        name="s5_mixer",
    )(u3, perm, wm, we, w2, ar, ai)


def _layer_norm(h, g, b):
    mu = jnp.mean(h, axis=-1, keepdims=True)
    c = h - mu
    var = jnp.mean(c * c, axis=-1, keepdims=True)
    return c * lax.rsqrt(var + EPS) * g + b


def _ln0_kernel(x_ref, g_ref, b_ref, xf_ref, xb_ref):
    y = _layer_norm(x_ref[...], g_ref[...], b_ref[...])
    xf_ref[...] = y
    xb_ref[...] = y.astype(BF16)


def _row_spec(tm, width):
    return pl.BlockSpec((tm, width), lambda m: (m, 0))


def _resident_spec(block_shape, index_map):
    return pl.BlockSpec(block_shape, index_map, pipeline_mode=pl.Buffered(1))


def _layer_vec_spec(layer, width):
    return pl.BlockSpec((None, 1, width), lambda m: (layer, 0, 0))


def _ln0(x, g, b, tm=256):
    t, d = x.shape
    tm = min(tm, t)
    return pl.pallas_call(
        _ln0_kernel,
        grid=(t // tm,),
        in_specs=[_row_spec(tm, d),
                  pl.BlockSpec((1, d), lambda m: (0, 0)),
                  pl.BlockSpec((1, d), lambda m: (0, 0))],
        out_specs=[_row_spec(tm, d), _row_spec(tm, d)],
        out_shape=[jax.ShapeDtypeStruct((t, d), F32), jax.ShapeDtypeStruct((t, d), BF16)],
        compiler_params=_cparams(("parallel",)),
        name="ln0",
    )(x, g.reshape(1, d), b.reshape(1, d))


def _inproj_kernel(x_ref, w_ref, qg_ref, kg_ref, cos_ref, sin_ref, q_ref, k_ref, v_ref, u_ref, u_s):
    x = x_ref[...]
    cos2 = cos_ref[...]
    sin2 = sin_ref[...]
    scale = HEAD_DIM ** -0.5

    def head(z, gain, mult):
        zn = z * lax.rsqrt(jnp.mean(z * z, axis=-1, keepdims=True) + EPS) * gain
        rot = pltpu.roll(zn, HEAD_DIM // 2, axis=1)
        out = zn * cos2 + rot * sin2
        return out * mult if mult != 1.0 else out

    pair = 2 * HEAD_DIM
    for c in range(D_IN_PROJ // pair):
        acc = jnp.dot(x, w_ref[:, c * pair:(c + 1) * pair], preferred_element_type=F32)
        col = c * pair
        if col < ATTN_WIDTH:
            for h in range(2):
                z = head(acc[:, h * HEAD_DIM:(h + 1) * HEAD_DIM], qg_ref[...], scale)
                q_ref[:, col + h * HEAD_DIM: col + (h + 1) * HEAD_DIM] = z.astype(BF16)
        elif col < QK_WIDTH:
            for h in range(2):
                z = head(acc[:, h * HEAD_DIM:(h + 1) * HEAD_DIM], kg_ref[...], 1.0)
                k_ref[:, h * HEAD_DIM:(h + 1) * HEAD_DIM] = z.astype(BF16)
        elif col < QK_WIDTH + KV_WIDTH:
            v_ref[...] = acc.astype(BF16)
        else:
            tm = acc.shape[0]
            for half in range(2):
                blk = (col - (QK_WIDTH + KV_WIDTH)) // LANES + half
                u_s[...] = acc[:, half * LANES:(half + 1) * LANES]
                for j in range(SSM_CHUNK):
                    step_rows = u_s[pl.ds(j, tm // SSM_CHUNK, stride=SSM_CHUNK), :]
                    u_ref[blk, :, j * LANES:(j + 1) * LANES] = step_rows.astype(BF16)


def _inproj(xb, w_in, q_gain, k_gain, cos2, sin2, layer, seq_len, tm=512):
    t, d = xb.shape
    tm = min(tm, seq_len)
    sblocks = seq_len // tm
    nblk = SSM_GROUPS // SSM_GB
    return pl.pallas_call(
        _inproj_kernel,
        grid=(t // tm,),
        in_specs=[_row_spec(tm, d),
                  _resident_spec((None, d, D_IN_PROJ), lambda m: (layer, 0, 0)),
                  _layer_vec_spec(layer, HEAD_DIM),
                  _layer_vec_spec(layer, HEAD_DIM),
                  pl.BlockSpec((tm, HEAD_DIM), lambda m: (m % sblocks, 0)),
                  pl.BlockSpec((tm, HEAD_DIM), lambda m: (m % sblocks, 0))],
        out_specs=[_row_spec(tm, ATTN_WIDTH), _row_spec(tm, KV_WIDTH), _row_spec(tm, KV_WIDTH),
                   pl.BlockSpec((nblk, tm // SSM_CHUNK, SSM_GB * SSM_CW), lambda m: (0, m, 0))],
        out_shape=[jax.ShapeDtypeStruct((t, ATTN_WIDTH), BF16),
                   jax.ShapeDtypeStruct((t, KV_WIDTH), BF16),
                   jax.ShapeDtypeStruct((t, KV_WIDTH), BF16),
                   jax.ShapeDtypeStruct((nblk, t // SSM_CHUNK, SSM_GB * SSM_CW), BF16)],
        scratch_shapes=[pltpu.VMEM((tm, LANES), F32)],
        compiler_params=_cparams(("parallel",)),
        name="inproj",
    )(xb, w_in, q_gain, k_gain, cos2, sin2)


def _attn_kernel(q_ref, k_ref, v_ref, g_ref, o_ref):
    outs = []
    ssq = None
    for h in range(N_HEADS):
        kv = h // GQA
        q = q_ref[:, h * HEAD_DIM:(h + 1) * HEAD_DIM]
        k = k_ref[:, kv * HEAD_DIM:(kv + 1) * HEAD_DIM]
        v = v_ref[:, kv * HEAD_DIM:(kv + 1) * HEAD_DIM]
        s = lax.dot_general(q, k, (((1,), (1,)), ((), ())), preferred_element_type=F32)
        m = jnp.max(s, axis=-1, keepdims=True)
        p = jnp.exp(s - m)
        l = jnp.sum(p, axis=-1, keepdims=True)
        o = jnp.dot(p.astype(BF16), v, preferred_element_type=F32) / l
        outs.append(o)
        sq = jnp.sum(o * o, axis=-1, keepdims=True)
        ssq = sq if ssq is None else ssq + sq
    r = lax.rsqrt(ssq * (1.0 / ATTN_WIDTH) + EPS)
    for h in range(N_HEADS):
        sl = slice(h * HEAD_DIM, (h + 1) * HEAD_DIM)
        o_ref[:, sl] = (outs[h] * r * g_ref[:, sl]).astype(BF16)


def _attention(q, k, v, gain, layer, bsz, seq_len, tq=256):
    t = q.shape[0]
    tq = min(tq, seq_len)
    qblocks = seq_len // tq
    return pl.pallas_call(
        _attn_kernel,
        grid=(bsz, qblocks),
        in_specs=[pl.BlockSpec((tq, ATTN_WIDTH), lambda b, i: (b * qblocks + i, 0)),
                  pl.BlockSpec((seq_len, KV_WIDTH), lambda b, i: (b, 0)),
                  pl.BlockSpec((seq_len, KV_WIDTH), lambda b, i: (b, 0)),
                  pl.BlockSpec((None, 1, ATTN_WIDTH), lambda b, i: (layer, 0, 0))],
        out_specs=pl.BlockSpec((tq, ATTN_WIDTH), lambda b, i: (b * qblocks + i, 0)),
        out_shape=jax.ShapeDtypeStruct((t, ATTN_WIDTH), BF16),
        compiler_params=_cparams(("parallel", "parallel")),
        name="attention",
    )(q, k, v, gain)


def _glu_kernel(y_ref, w_ref, b_ref, g_ref, o_ref):
    y = y_ref[...]
    gate = jnp.dot(y.astype(BF16), w_ref[...], preferred_element_type=F32) + b_ref[...]
    z = y * jax.nn.sigmoid(gate)
    zn = z * lax.rsqrt(jnp.mean(z * z, axis=-1, keepdims=True) + EPS) * g_ref[...]
    o_ref[...] = zn.astype(BF16)


def _glu(y, w_glu, b_glu, gain, layer, tm=512):
    t, w = y.shape
    tm = min(tm, t)
    return pl.pallas_call(
        _glu_kernel,
        grid=(t // tm,),
        in_specs=[_row_spec(tm, w),
                  pl.BlockSpec((None, w, w), lambda m: (layer, 0, 0)),
                  _layer_vec_spec(layer, w),
                  _layer_vec_spec(layer, w)],
        out_specs=_row_spec(tm, w),
        out_shape=jax.ShapeDtypeStruct((t, w), BF16),
        compiler_params=_cparams(("parallel",)),
        name="glu",
    )(y, w_glu, b_glu, gain)


ROUTE_LANE_GROUP = N_EXPERTS


def _route(logits):
    lane = lax.broadcasted_iota(jnp.int32, logits.shape, 1)
    neg = jnp.float32(-1e30)
    big = jnp.int32(LANES)

    def masked_softmax(mask):
        z = jnp.where(mask, logits, neg)
        zmax = jnp.max(z, axis=-1, keepdims=True)
        e = jnp.where(mask, jnp.exp(z - zmax), 0.0)
        return e / jnp.sum(e, axis=-1, keepdims=True)

    def first_argmax(vals, mask):
        top = jnp.max(jnp.where(mask, vals, -1.0), axis=-1, keepdims=True)
        idx = jnp.min(jnp.where(mask & (vals == top), lane, big), axis=-1, keepdims=True)
        return top, idx

    gmask = (lane >= ROUTE_LANE_GROUP) & (lane < ROUTE_LANE_GROUP + N_EXPERT_GROUPS)
    g_top, g_lane = first_argmax(masked_softmax(gmask), gmask)
    e_lo = (g_lane - ROUTE_LANE_GROUP) * EXPERTS_PER_GROUP
    emask = (lane >= e_lo) & (lane < e_lo + EXPERTS_PER_GROUP)
    e_prob = masked_softmax(emask)
    v1, i1 = first_argmax(e_prob, emask)
    v2, i2 = first_argmax(e_prob, emask & (lane != i1))
    denom = v1 + v2
    return i1, i2, g_top * (v1 / denom), g_top * (v2 / denom)


TOK_ROWS = D_MODEL // LANES
MOE_TM = 256
DMA_UNROLL = 8
GATHER_PITCH = 24
RL_E1, RL_E2, RL_G1, RL_G2, RL_R1, RL_R2 = 0, 1, 2, 3, 4, 5


def _to_token_major(ref, x):
    tm = x.shape[0]
    for s in range(TOK_ROWS):
        ref[pl.ds(s, tm, stride=TOK_ROWS), :] = x[:, s * LANES:(s + 1) * LANES]


def _from_token_major(ref, tm, s):
    return ref[pl.ds(s, tm, stride=TOK_ROWS), :]


def _outproj_kernel(a_ref, y_ref, wa_ref, wy_ref, x_ref, g_ref, b_ref, wr_ref, br_ref,
                    xb_ref, x16_ref, r_ref, meta_ref, cnt_ref, run_s):
    tm = a_ref.shape[0]

    @pl.when(pl.program_id(0) == 0)
    def _():
        run_s[...] = jnp.zeros_like(run_s)

    m = (jnp.dot(a_ref[...], wa_ref[...], preferred_element_type=F32)
         + jnp.dot(y_ref[...], wy_ref[...], preferred_element_type=F32))
    xn = _layer_norm(ALPHA * x_ref[...] + m, g_ref[...], b_ref[...])
    xn_hi = xn.astype(BF16)
    xb_ref[...] = xn_hi
    _to_token_major(x16_ref, xn)
    xn_lo = (xn - xn_hi.astype(F32)).astype(BF16)
    logits = (jnp.dot(xn_hi, wr_ref[0], preferred_element_type=F32)
              + jnp.dot(xn_hi, wr_ref[1], preferred_element_type=F32)
              + jnp.dot(xn_lo, wr_ref[0], preferred_element_type=F32)) + br_ref[...]
    i1, i2, g1, g2 = _route(logits)

    lane = lax.broadcasted_iota(jnp.int32, (tm, LANES), 1)
    hit1, hit2 = lane == i1, lane == i2
    onehot = (hit1 | hit2).astype(BF16)
    earlier = (lax.broadcasted_iota(jnp.int32, (tm, tm), 0)
               > lax.broadcasted_iota(jnp.int32, (tm, tm), 1)).astype(BF16)
    rank = jnp.dot(earlier, onehot, preferred_element_type=F32) + run_s[0:1, :]
    r1 = jnp.sum(jnp.where(hit1, rank, 0.0), axis=-1, keepdims=True)
    r2 = jnp.sum(jnp.where(hit2, rank, 0.0), axis=-1, keepdims=True)
    total = run_s[0:1, :] + jnp.sum(onehot.astype(F32), axis=0, keepdims=True)
    run_s[...] = jnp.broadcast_to(total, run_s.shape)
    cnt_ref[...] = jnp.broadcast_to(total, cnt_ref.shape)
    out = jnp.zeros((tm, LANES), F32)
    for col, val in ((RL_E1, i1.astype(F32)), (RL_E2, i2.astype(F32)), (RL_G1, g1), (RL_G2, g2),
                     (RL_R1, r1), (RL_R2, r2)):
        out = jnp.where(lane == col, val, out)
    r_ref[...] = out
    meta_ref[...] = out.T[:SUBLANES, :].astype(jnp.int32)


def _outproj(attn_n, y_n, w_out, xf, ln_g, ln_b, w_route, b_route, layer, tm=512):
    t, d = xf.shape
    tm = min(tm, t)
    half = D_MIX // 2
    return pl.pallas_call(
        _outproj_kernel,
        grid=(t // tm,),
        in_specs=[_row_spec(tm, half), _row_spec(tm, half),
                  _resident_spec((None, half, d), lambda m: (layer, 0, 0)),
                  _resident_spec((None, half, d), lambda m: (layer, 1, 0)),
                  _row_spec(tm, d),
                  _layer_vec_spec(layer, d), _layer_vec_spec(layer, d),
                  _resident_spec((None, 2, d, LANES), lambda m: (layer, 0, 0, 0)),
                  _layer_vec_spec(layer, LANES)],
        out_specs=[_row_spec(tm, d), _row_spec(tm * TOK_ROWS, LANES), _row_spec(tm, LANES),
                   pl.BlockSpec((SUBLANES, tm), lambda m: (0, m)),
                   pl.BlockSpec((SUBLANES, LANES), lambda m: (0, 0))],
        out_shape=[jax.ShapeDtypeStruct((t, d), BF16),
                   jax.ShapeDtypeStruct((t * TOK_ROWS, LANES), F32),
                   jax.ShapeDtypeStruct((t, LANES), F32),
                   jax.ShapeDtypeStruct((SUBLANES, t), jnp.int32),
                   jax.ShapeDtypeStruct((SUBLANES, LANES), F32)],
        scratch_shapes=[pltpu.VMEM((SUBLANES, LANES), F32)],
        compiler_params=_cparams(("arbitrary",)),
        name="outproj_ln_route",
    )(attn_n, y_n, w_out, w_out, xf, ln_g, ln_b, w_route, b_route)


def _token_copy(src, src_tok, dst, dst_tok, sem, dst_pitch=TOK_ROWS):
    return pltpu.make_async_copy(
        src.at[pl.ds(pl.multiple_of(src_tok * TOK_ROWS, TOK_ROWS), TOK_ROWS), :],
        dst.at[pl.ds(pl.multiple_of(dst_tok * dst_pitch, SUBLANES), TOK_ROWS), :],
        sem)


def _sorted_row(es_ref, e_ref, r_ref, t):
    return es_ref[e_ref[t]] + r_ref[t]


def _dispatch_kernel(es_ref, e1_ref, e2_ref, r1_ref, r2_ref, x16_ref, xs_in_ref, xs_ref, sem, *, tm):
    del xs_in_ref

    def start(t, c):
        _token_copy(x16_ref, t, xs_ref, _sorted_row(es_ref, e1_ref, r1_ref, t), sem.at[0]).start()
        _token_copy(x16_ref, t, xs_ref, _sorted_row(es_ref, e2_ref, r2_ref, t), sem.at[1]).start()
        return c

    lax.fori_loop(0, tm, start, 0, unroll=DMA_UNROLL)
    whole = pl.ds(0, tm * TOK_ROWS)
    pltpu.make_async_copy(x16_ref, xs_ref.at[whole, :], sem.at[0]).wait()
    pltpu.make_async_copy(x16_ref, xs_ref.at[whole, :], sem.at[1]).wait()


def _smem_tile_spec(tm):
    return pl.BlockSpec((tm,), lambda m: (m,), memory_space=pltpu.SMEM)


def _token_meta_specs(tm):
    return [pl.BlockSpec(memory_space=pltpu.SMEM)] + [_smem_tile_spec(tm)] * 4


def _dispatch(x16, e_start, meta, xs_buf, tm=256):
    t = meta.shape[1]
    tm = min(tm, t)
    return pl.pallas_call(
        functools.partial(_dispatch_kernel, tm=tm),
        grid=(t // tm,),
        in_specs=_token_meta_specs(tm) + [_row_spec(tm * TOK_ROWS, LANES),
                                          pl.BlockSpec(memory_space=pl.ANY)],
        out_specs=pl.BlockSpec(memory_space=pl.ANY),
        out_shape=jax.ShapeDtypeStruct(xs_buf.shape, F32),
        input_output_aliases={6: 0},
        scratch_shapes=[pltpu.SemaphoreType.DMA((2,))],
        compiler_params=_cparams(("arbitrary",)),
        name="moe_dispatch",
    )(e_start, meta[RL_E1], meta[RL_E2], meta[RL_R1], meta[RL_R2], x16, xs_buf)


def _expert_kernel(te_ref, tr_ref, xs_ref, wg_ref, wu_ref, wd_ref, ys_ref, lhs_s, wg_s, wu_s, wd_s, *, tm):
    j = pl.program_id(0)
    rows = tr_ref[j]

    @pl.when(rows == 0)
    def _():
        ys_ref[...] = jnp.zeros_like(ys_ref)

    @pl.when((rows > 0) & ((j == 0) | (te_ref[j] != te_ref[jnp.maximum(j - 1, 0)])))
    def _():
        wg_s[...] = wg_ref[...].astype(BF16)
        wu_s[...] = wu_ref[...].astype(BF16)
        wd_s[...] = wd_ref[...].astype(BF16)

    @pl.when(rows > 0)
    def _():
        valid = lax.broadcasted_iota(jnp.int32, (tm, LANES), 0) < rows
        for s in range(TOK_ROWS):
            blk = jnp.where(valid, _from_token_major(xs_ref, tm, s), 0.0)
            lhs_s[:, s * LANES:(s + 1) * LANES] = blk.astype(BF16)
        x = lhs_s[...]
        hg = jnp.dot(x, wg_s[...], preferred_element_type=F32)
        hu = jnp.dot(x, wu_s[...], preferred_element_type=F32)
        hid = ((hg * jax.nn.sigmoid(hg)) * hu).astype(BF16)
        for c in range(D_MODEL // D_EXPERT):
            out = jnp.dot(hid, wd_s[:, c * D_EXPERT:(c + 1) * D_EXPERT], preferred_element_type=F32)
            for h in range(D_EXPERT // LANES):
                s = c * (D_EXPERT // LANES) + h
                ys_ref[pl.ds(s, tm, stride=TOK_ROWS), :] = out[:, h * LANES:(h + 1) * LANES]


def _moe_n_tiles(t, tm):
    return (2 * t) // tm + N_EXPERTS


def _moe_tiles(counts, t, tm):
    n_tiles = _moe_n_tiles(t, tm)
    per_e = (counts + tm - 1) // tm
    ends = jnp.cumsum(per_e)
    starts = ends - per_e
    j = jnp.arange(n_tiles, dtype=jnp.int32)
    e = jnp.minimum(jnp.sum((j[:, None] >= ends[None, :]).astype(jnp.int32), axis=1), N_EXPERTS - 1)
    rows = jnp.clip(counts[e] - (j - starts[e]) * tm, 0, tm)
    rows = jnp.where(j < ends[-1], rows, 0)
    return e.astype(jnp.int32), rows.astype(jnp.int32), (starts * tm).astype(jnp.int32)


def _experts(xs, te, tr, w_gate, w_up, w_down, layer, tm):
    d = D_MODEL
    blk = pl.BlockSpec((tm * TOK_ROWS, LANES), lambda j, te, tr: (j, 0))
    grid_spec = pltpu.PrefetchScalarGridSpec(
        num_scalar_prefetch=2,
        grid=(te.shape[0],),
        in_specs=[blk,
                  pl.BlockSpec((None, None, d, D_EXPERT), lambda j, te, tr: (layer, te[j], 0, 0)),
                  pl.BlockSpec((None, None, d, D_EXPERT), lambda j, te, tr: (layer, te[j], 0, 0)),
                  pl.BlockSpec((None, None, D_EXPERT, d), lambda j, te, tr: (layer, te[j], 0, 0))],
        out_specs=blk,
        scratch_shapes=[pltpu.VMEM((tm, d), BF16),
                        pltpu.VMEM((d, D_EXPERT), BF16),
                        pltpu.VMEM((d, D_EXPERT), BF16),
                        pltpu.VMEM((D_EXPERT, d), BF16)],
    )
    return pl.pallas_call(
        functools.partial(_expert_kernel, tm=tm),
        grid_spec=grid_spec,
        out_shape=jax.ShapeDtypeStruct(xs.shape, F32),
        compiler_params=_cparams(("arbitrary",)),
        name="moe_experts",
    )(te, tr, xs, w_gate, w_up, w_down)


def _combine_kernel(es_ref, e1_ref, e2_ref, r1_ref, r2_ref, xb_ref, x16_ref, r_ref, p_ref, wp_ref, wg_ref,
                    bg_ref, g_ref, b_ref, ys_ref, of_ref, ob_ref, y1_s, y2_s, h_s, sem, *, tm):
    def start(t, c):
        _token_copy(ys_ref, _sorted_row(es_ref, e1_ref, r1_ref, t), y1_s, t, sem.at[0], GATHER_PITCH).start()
        _token_copy(ys_ref, _sorted_row(es_ref, e2_ref, r2_ref, t), y2_s, t, sem.at[1], GATHER_PITCH).start()
        return c

    lax.fori_loop(0, tm, start, 0, unroll=DMA_UNROLL)
    gate = jnp.dot(xb_ref[...], wg_ref[...], preferred_element_type=F32) + bg_ref[...]
    ple = jnp.dot(p_ref[...].astype(BF16), wp_ref[...], preferred_element_type=F32) * jax.nn.sigmoid(gate)
    whole = pl.ds(0, tm * TOK_ROWS)
    pltpu.make_async_copy(ys_ref.at[whole, :], y1_s.at[whole, :], sem.at[0]).wait()
    pltpu.make_async_copy(ys_ref.at[whole, :], y2_s.at[whole, :], sem.at[1]).wait()
    g1 = r_ref[:, RL_G1:RL_G1 + 1]
    g2 = r_ref[:, RL_G2:RL_G2 + 1]
    for s in range(TOK_ROWS):
        sl = slice(s * LANES, (s + 1) * LANES)
        f = (g1 * y1_s[pl.ds(s, tm, stride=GATHER_PITCH), :]
             + g2 * y2_s[pl.ds(s, tm, stride=GATHER_PITCH), :])
        h_s[:, sl] = ALPHA * _from_token_major(x16_ref, tm, s) + f + ple[:, sl]
    xn = _layer_norm(h_s[...], g_ref[...], b_ref[...])
    of_ref[...] = xn
    ob_ref[...] = xn.astype(BF16)


def _combine_ple_ln(xb, x16, route, e_start, meta, ys, p, w_ple, w_pg, b_pg, ln_g, ln_b, layer, tm=512):
    t, d = xb.shape
    tm = min(tm, t)
    return pl.pallas_call(
        functools.partial(_combine_kernel, tm=tm),
        grid=(t // tm,),
        in_specs=_token_meta_specs(tm) + [
            _row_spec(tm, d), _row_spec(tm * TOK_ROWS, LANES), _row_spec(tm, LANES),
            pl.BlockSpec((None, tm, PLE_DIM), lambda m: (layer, m, 0)),
            _resident_spec((None, PLE_DIM, d), lambda m: (layer, 0, 0)),
            _resident_spec((None, d, d), lambda m: (layer, 0, 0)),
            _layer_vec_spec(layer, d), _layer_vec_spec(layer, d), _layer_vec_spec(layer, d),
            pl.BlockSpec(memory_space=pl.ANY)],
        out_specs=[_row_spec(tm, d), _row_spec(tm, d)],
        out_shape=[jax.ShapeDtypeStruct((t, d), F32), jax.ShapeDtypeStruct((t, d), BF16)],
        scratch_shapes=[pltpu.VMEM((tm * GATHER_PITCH, LANES), F32),
                        pltpu.VMEM((tm * GATHER_PITCH, LANES), F32),
                        pltpu.VMEM((tm, d), F32),
                        pltpu.SemaphoreType.DMA((2,))],
        compiler_params=_cparams(("arbitrary",)),
        name="moe_combine_ple_ln",
    )(e_start, meta[RL_E1], meta[RL_E2], meta[RL_R1], meta[RL_R2],
      xb, x16, route, p, w_ple, w_pg, b_pg, ln_g, ln_b, ys)


def _rope_tables(seq_len):
    rows = seq_len // GRID_W
    row = jnp.repeat(jnp.arange(rows, dtype=F32), GRID_W)
    col = jnp.tile(jnp.arange(GRID_W, dtype=F32), rows)
    inv_freq = ROPE_THETA ** (-jnp.arange(ROPE_AXIS_PAIRS, dtype=F32) / ROPE_AXIS_PAIRS)
    ang = jnp.concatenate([row[:, None] * inv_freq, col[:, None] * inv_freq], axis=-1)
    cos, sin = jnp.cos(ang), jnp.sin(ang)
    return jnp.concatenate([cos, cos], axis=-1), jnp.concatenate([-sin, sin], axis=-1)


def kernel(x, p, ln0_g, ln0_b, w_in, q_norm_g, k_norm_g, ssm_a_re, ssm_a_im, ssm_log_dt, ssm_b_re, ssm_b_im, ssm_c_re, ssm_c_im, ssm_d, w_glu, b_glu, attn_out_g, ssm_out_g, w_out, ln1_g, ln1_b, router_group_w, router_group_b, router_expert_w, router_expert_b, w_gate_e, w_up_e, w_down_e, w_ple, w_ple_gate, b_ple_gate, ln2_g, ln2_b):
    bsz, seq_len, d = x.shape
    t = bsz * seq_len
    n_layers = w_in.shape[0]
    vec = lambda a: a.reshape(n_layers, 1, a.shape[-1])

    cos2, sin2 = _rope_tables(seq_len)
    w_in_b, w_glu_b, w_out_b = w_in.astype(BF16), w_glu.astype(BF16), w_out.astype(BF16)
    w_ple_b, w_pg_b = w_ple.astype(BF16), w_ple_gate.astype(BF16)
    pad = LANES - N_EXPERTS - N_EXPERT_GROUPS
    w_route = jnp.pad(jnp.concatenate([router_expert_w, router_group_w], axis=-1), ((0, 0), (0, 0), (0, pad)))
    w_route_hi = w_route.astype(BF16)
    w_route_lo = (w_route - w_route_hi.astype(F32)).astype(BF16)
    w_route = jnp.stack([w_route_hi, w_route_lo], axis=1)
    b_route = jnp.pad(jnp.concatenate([router_expert_b, router_group_b], axis=-1), ((0, 0), (0, pad)))
    s5_ops = jax.vmap(_s5_operands)(ssm_a_re, ssm_a_im, ssm_log_dt, ssm_b_re, ssm_b_im,
                                    ssm_c_re, ssm_c_im, ssm_d)
    p2 = p.reshape(n_layers, t, PLE_DIM)
    perm = _s5_perm()

    tm_e = min(MOE_TM, t)
    xs = jnp.zeros((_moe_n_tiles(t, tm_e) * tm_e * TOK_ROWS, LANES), F32)
    xf, xb = _ln0(x.reshape(t, d), ln0_g, ln0_b)
    for i in range(n_layers):
        q, k, v, u = _inproj(xb, w_in_b, vec(q_norm_g), vec(k_norm_g), cos2, sin2, i, seq_len)
        attn_n = _attention(q, k, v, vec(attn_out_g), i, bsz, seq_len)
        y = _s5_mixer(u, perm, *(op[i] for op in s5_ops), bsz)
        y_n = _glu(y, w_glu_b, vec(b_glu), vec(ssm_out_g), i)
        xb, x16, route, meta, counts = _outproj(attn_n, y_n, w_out_b, xf, vec(ln1_g), vec(ln1_b),
                                                w_route, vec(b_route), i)
        te, tr, e_start = _moe_tiles(counts[0, :N_EXPERTS].astype(jnp.int32), t, tm_e)
        xs = _dispatch(x16, e_start, meta, xs)
        ys = _experts(xs, te, tr, w_gate_e, w_up_e, w_down_e, i, tm_e)
        xf, xb = _combine_ple_ln(xb, x16, route, e_start, meta, ys, p2, w_ple_b, w_pg_b,
                                 vec(b_ple_gate), vec(ln2_g), vec(ln2_b), i)
    return xf.reshape(bsz, seq_len, d)
```

```python
import functools
import math

import jax
import jax.numpy as jnp
from jax import lax
from jax.experimental import pallas as pl
from jax.experimental.pallas import tpu as pltpu

F32 = jnp.float32
BF16 = jnp.bfloat16

D_MODEL = 2048
DEPTH = 4
GRID_W = 64
N_HEADS = 8
N_KV_HEADS = 2
GQA = N_HEADS // N_KV_HEADS
HEAD_DIM = 128
ATTN_WIDTH = N_HEADS * HEAD_DIM
KV_WIDTH = N_KV_HEADS * HEAD_DIM
QK_WIDTH = ATTN_WIDTH + KV_WIDTH
ROPE_THETA = 10000.0
ROPE_AXIS_PAIRS = HEAD_DIM // 4
SSM_WIDTH = D_MODEL // 2
SSM_CH = 16
SSM_GROUPS = SSM_WIDTH // SSM_CH
SSM_STATE = 64
D_MIX = ATTN_WIDTH + SSM_WIDTH
D_IN_PROJ = ATTN_WIDTH + 2 * KV_WIDTH + SSM_WIDTH
N_EXPERT_GROUPS = 4
EXPERTS_PER_GROUP = 4
N_EXPERTS = N_EXPERT_GROUPS * EXPERTS_PER_GROUP
D_EXPERT = 512
PLE_DIM = 256
ALPHA = (2 * DEPTH) ** 0.25
EPS = 1e-6

LANES = 128
SUBLANES = 8
VMEM_LIMIT = 56 * 1024 * 1024

SSM_CHUNK = 16
SSM_CW = SSM_CHUNK * SSM_CH
SSM_GB = 8


def _cparams(sem):
    return pltpu.CompilerParams(dimension_semantics=sem, vmem_limit_bytes=VMEM_LIMIT)


def _s5_discretize(a_re, a_im, log_dt, b_re, b_im):
    dt = jnp.exp(log_dt)[:, None]
    lam_re, lam_im = dt * a_re, dt * a_im
    mag = jnp.exp(lam_re)
    abar_re, abar_im = mag * jnp.cos(lam_im), mag * jnp.sin(lam_im)
    den = a_re * a_re + a_im * a_im
    nr, ni = abar_re - 1.0, abar_im
    coef_re = (nr * a_re + ni * a_im) / den
    coef_im = (ni * a_re - nr * a_im) / den
    bb_re = coef_re[..., None] * b_re - coef_im[..., None] * b_im
    bb_im = coef_re[..., None] * b_im + coef_im[..., None] * b_re
    return lam_re, lam_im, bb_re, bb_im


def _cpow(lam_re, lam_im, k):
    mag = jnp.exp(lam_re * k)
    return mag * jnp.cos(lam_im * k), mag * jnp.sin(lam_im * k)


def _s5_operands(a_re, a_im, log_dt, b_re, b_im, c_re, c_im, d_skip):
    hp = lax.Precision.HIGHEST
    L, H, G = SSM_CHUNK, SSM_CH, SSM_GROUPS
    lam_f = _s5_discretize(a_re[0], a_im[0], log_dt[0], b_re[0], b_im[0])
    lam_b = _s5_discretize(a_re[1], a_im[1], log_dt[1], b_re[1], b_im[1])
    tau = jnp.arange(L, dtype=F32)

    def powers(lam, expo):
        return _cpow(lam[0][:, None, :], lam[1][:, None, :], expo[None, :, None])

    def conv_kernel(lam, cr, ci):
        pr, pi = powers(lam, tau)
        bbr, bbi = lam[2], lam[3]
        wr = pr[..., None] * bbr[:, None] - pi[..., None] * bbi[:, None]
        wi = pr[..., None] * bbi[:, None] + pi[..., None] * bbr[:, None]
        return (jnp.einsum('gop,glpi->gloi', cr, wr, precision=hp)
                - jnp.einsum('gop,glpi->gloi', ci, wi, precision=hp))

    k_f = conv_kernel(lam_f, c_re[0], c_im[0])
    k_b = conv_kernel(lam_b, c_re[1], c_im[1])
    lag0 = (tau == 0.0).astype(F32)[None, :, None, None]
    k_f = k_f + lag0 * (d_skip[:, None, :, None] * jnp.eye(H, dtype=F32))
    lag = jnp.arange(L)[None, :] - jnp.arange(L)[:, None]
    l_idx = jnp.arange(L)[None, None, :]
    sel = jnp.concatenate([(lag[..., None] == l_idx), (-lag[..., None] == l_idx)], axis=-1).astype(F32)
    wm = jnp.einsum('stl,gloi->gsito', sel, jnp.concatenate([k_f, k_b], axis=1), precision=hp)
    wm = wm.reshape(G, L * H, L * H)

    def end_map(lam, expo):
        pr, pi = powers(lam, expo)
        bbr, bbi = lam[2].transpose(0, 2, 1)[:, None], lam[3].transpose(0, 2, 1)[:, None]
        pr, pi = pr[:, :, None, :], pi[:, :, None, :]
        return pr * bbr - pi * bbi, pr * bbi + pi * bbr

    ef_re, ef_im = end_map(lam_f, (L - 1) - tau)
    eb_re, eb_im = end_map(lam_b, tau)
    we = jnp.concatenate([ef_re, eb_re, ef_im, eb_im], axis=-1).reshape(G, L * H, 4 * SSM_STATE)

    def out_map(lam, cr, ci, expo):
        pr, pi = powers(lam, expo)
        pr, pi = pr.transpose(0, 2, 1)[..., None], pi.transpose(0, 2, 1)[..., None]
        cr, ci = cr.transpose(0, 2, 1)[:, :, None, :], ci.transpose(0, 2, 1)[:, :, None, :]
        return cr * pr - ci * pi, -(cr * pi + ci * pr)

    of_re, of_im = out_map(lam_f, c_re[0], c_im[0], tau + 1.0)
    ob_re, ob_im = out_map(lam_b, c_re[1], c_im[1], L - tau)
    w2 = jnp.concatenate([of_re, ob_re, of_im, ob_im], axis=1).reshape(G, 4 * SSM_STATE, L * H)

    afr, afi = _cpow(lam_f[0], lam_f[1], float(L))
    abr, abi = _cpow(lam_b[0], lam_b[1], float(L))
    ar = jnp.concatenate([afr, abr], axis=-1)
    ai = jnp.concatenate([afi, abi], axis=-1)
    return wm.astype(BF16), we.astype(BF16), w2.astype(BF16), ar, ai


_S5_ROWS = 264


def _gelu_tanh(y):
    return 0.5 * y * (1.0 + jnp.tanh(math.sqrt(2.0 / math.pi) * (y + 0.044715 * (y * y * y))))


def _s5_perm():
    n = SSM_GB * SSM_CW
    r = lax.broadcasted_iota(jnp.int32, (n, n), 0)
    c = lax.broadcasted_iota(jnp.int32, (n, n), 1)
    r_step, r_grp, r_ch = r // LANES, (r % LANES) // SSM_CH, r % SSM_CH
    c_grp, c_step, c_ch = c // SSM_CW, (c % SSM_CW) // SSM_CH, c % SSM_CH
    return ((r_step == c_step) & (r_grp == c_grp) & (r_ch == c_ch)).astype(BF16)


def _s5_kernel(u_ref, perm_ref, wm_ref, we_ref, w2_ref, ar_ref, ai_ref, y_ref,
               er_s, ei_s, fr_s, fi_s, br_s, bi_s, yi_s, yg_s, *, n_chunks):
    nc = n_chunks
    zero_tile = jnp.zeros((SUBLANES, LANES), F32)
    ug = jnp.dot(u_ref[...], perm_ref[...], preferred_element_type=F32).astype(BF16)
    for g in range(SSM_GB):
        base = g * _S5_ROWS
        u_g = ug[:, g * SSM_CW:(g + 1) * SSM_CW]
        yi_s[g] = jnp.dot(u_g, wm_ref[g], preferred_element_type=F32)
        ends = jnp.dot(u_g, we_ref[g], preferred_element_type=F32)
        er_s[pl.ds(base, nc), :] = ends[:, :LANES]
        ei_s[pl.ds(base, nc), :] = ends[:, LANES:]
        fr_s[pl.ds(base, SUBLANES), :] = zero_tile
        fi_s[pl.ds(base, SUBLANES), :] = zero_tile
        br_s[pl.ds(base + nc, SUBLANES), :] = zero_tile
        bi_s[pl.ds(base + nc, SUBLANES), :] = zero_tile

    ar = ar_ref[...]
    ai = ai_ref[...]
    fwd_lane = lax.broadcasted_iota(jnp.int32, (SUBLANES, LANES), 1) < SSM_STATE

    def step(i, carry):
        xr, xi = carry
        rows_f = pl.ds(i, SUBLANES, stride=_S5_ROWS)
        rows_b = pl.ds(nc - 1 - i, SUBLANES, stride=_S5_ROWS)
        er = jnp.where(fwd_lane, er_s[rows_f, :], er_s[rows_b, :])
        ei = jnp.where(fwd_lane, ei_s[rows_f, :], ei_s[rows_b, :])
        nr = ar * xr - ai * xi + er
        ni = ar * xi + ai * xr + ei
        to_f = pl.ds(i + 1, SUBLANES, stride=_S5_ROWS)
        to_b = pl.ds(nc + 6 - i, SUBLANES, stride=_S5_ROWS)
        fr_s[to_f, :] = nr
        fi_s[to_f, :] = ni
        br_s[to_b, :] = nr
        bi_s[to_b, :] = ni
        return nr, ni

    lax.fori_loop(0, nc, step, (zero_tile, zero_tile))

    fwd_col = lax.broadcasted_iota(jnp.int32, (nc, LANES), 1) < SSM_STATE
    for g in range(SSM_GB):
        base = g * _S5_ROWS
        cr = jnp.where(fwd_col, fr_s[pl.ds(base, nc), :], br_s[pl.ds(base + SUBLANES, nc), :])
        ci = jnp.where(fwd_col, fi_s[pl.ds(base, nc), :], bi_s[pl.ds(base + SUBLANES, nc), :])
        y = (yi_s[g]
             + jnp.dot(cr.astype(BF16), w2_ref[g, :LANES, :], preferred_element_type=F32)
             + jnp.dot(ci.astype(BF16), w2_ref[g, LANES:, :], preferred_element_type=F32))
        yg_s[:, g * SSM_CW:(g + 1) * SSM_CW] = _gelu_tanh(y).astype(BF16)

    z = lax.dot_general(yg_s[...], perm_ref[...], (((1,), (1,)), ((), ())), preferred_element_type=F32)
    for j in range(SSM_CHUNK):
        y_ref[pl.ds(j, nc, stride=SSM_CHUNK), :] = z[:, j * LANES:(j + 1) * LANES]


def _s5_mixer(u3, perm, wm, we, w2, ar, ai, bsz):
    nblk, rows, width = u3.shape
    nc = rows // bsz
    gb = SSM_GB
    cw = SSM_CW
    kern = functools.partial(_s5_kernel, n_chunks=nc)
    return pl.pallas_call(
        kern,
        grid=(bsz, nblk),
        in_specs=[
            pl.BlockSpec((None, nc, width), lambda b, j: (j, b, 0)),
            pl.BlockSpec((width, width), lambda b, j: (0, 0)),
            pl.BlockSpec((gb, cw, cw), lambda b, j: (j, 0, 0)),
            pl.BlockSpec((gb, cw, cw), lambda b, j: (j, 0, 0)),
            pl.BlockSpec((gb, cw, cw), lambda b, j: (j, 0, 0)),
            pl.BlockSpec((gb, LANES), lambda b, j: (j, 0)),
            pl.BlockSpec((gb, LANES), lambda b, j: (j, 0)),
        ],
        out_specs=pl.BlockSpec((nc * SSM_CHUNK, LANES), lambda b, j: (b, j)),
        out_shape=jax.ShapeDtypeStruct((bsz * nc * SSM_CHUNK, nblk * LANES), F32),
        scratch_shapes=[
            *[pltpu.VMEM((gb * _S5_ROWS, LANES), F32) for _ in range(6)],
            pltpu.VMEM((gb, nc, cw), F32),
            pltpu.VMEM((nc, width), BF16),
        ],
        compiler_params=_cparams(("parallel", "parallel")),
        name="s5_mixer",
    )(u3, perm, wm, we, w2, ar, ai)


def _layer_norm(h, g, b):
    mu = jnp.mean(h, axis=-1, keepdims=True)
    c = h - mu
    var = jnp.mean(c * c, axis=-1, keepdims=True)
    return c * lax.rsqrt(var + EPS) * g + b


def _ln0_kernel(x_ref, g_ref, b_ref, xf_ref, xb_ref):
    y = _layer_norm(x_ref[...], g_ref[...], b_ref[...])
    xf_ref[...] = y
    xb_ref[...] = y.astype(BF16)


def _row_spec(tm, width):
    return pl.BlockSpec((tm, width), lambda m: (m, 0))


def _resident_spec(block_shape, index_map):
    return pl.BlockSpec(block_shape, index_map, pipeline_mode=pl.Buffered(1))


def _layer_vec_spec(layer, width):
    return pl.BlockSpec((None, 1, width), lambda m: (layer, 0, 0))


def _ln0(x, g, b, tm=256):
    t, d = x.shape
    tm = min(tm, t)
    return pl.pallas_call(
        _ln0_kernel,
        grid=(t // tm,),
        in_specs=[_row_spec(tm, d),
                  pl.BlockSpec((1, d), lambda m: (0, 0)),
                  pl.BlockSpec((1, d), lambda m: (0, 0))],
        out_specs=[_row_spec(tm, d), _row_spec(tm, d)],
        out_shape=[jax.ShapeDtypeStruct((t, d), F32), jax.ShapeDtypeStruct((t, d), BF16)],
        compiler_params=_cparams(("parallel",)),
        name="ln0",
    )(x, g.reshape(1, d), b.reshape(1, d))


def _inproj_kernel(x_ref, w_ref, qg_ref, kg_ref, cos_ref, sin_ref, q_ref, k_ref, v_ref, u_ref, u_s):
    x = x_ref[...]
    cos2 = cos_ref[...]
    sin2 = sin_ref[...]
    scale = HEAD_DIM ** -0.5

    def head(z, gain, mult):
        zn = z * lax.rsqrt(jnp.mean(z * z, axis=-1, keepdims=True) + EPS) * gain
        rot = pltpu.roll(zn, HEAD_DIM // 2, axis=1)
        out = zn * cos2 + rot * sin2
        return out * mult if mult != 1.0 else out

    pair = 2 * HEAD_DIM
    for c in range(D_IN_PROJ // pair):
        acc = jnp.dot(x, w_ref[:, c * pair:(c + 1) * pair], preferred_element_type=F32)
        col = c * pair
        if col < ATTN_WIDTH:
            for h in range(2):
                z = head(acc[:, h * HEAD_DIM:(h + 1) * HEAD_DIM], qg_ref[...], scale)
                q_ref[:, col + h * HEAD_DIM: col + (h + 1) * HEAD_DIM] = z.astype(BF16)
        elif col < QK_WIDTH:
            for h in range(2):
                z = head(acc[:, h * HEAD_DIM:(h + 1) * HEAD_DIM], kg_ref[...], 1.0)
                k_ref[:, h * HEAD_DIM:(h + 1) * HEAD_DIM] = z.astype(BF16)
        elif col < QK_WIDTH + KV_WIDTH:
            v_ref[...] = acc.astype(BF16)
        else:
            tm = acc.shape[0]
            for half in range(2):
                blk = (col - (QK_WIDTH + KV_WIDTH)) // LANES + half
                u_s[...] = acc[:, half * LANES:(half + 1) * LANES]
                for j in range(SSM_CHUNK):
                    step_rows = u_s[pl.ds(j, tm // SSM_CHUNK, stride=SSM_CHUNK), :]
                    u_ref[blk, :, j * LANES:(j + 1) * LANES] = step_rows.astype(BF16)


def _inproj(xb, w_in, q_gain, k_gain, cos2, sin2, layer, seq_len, tm=512):
    t, d = xb.shape
    tm = min(tm, seq_len)
    sblocks = seq_len // tm
    nblk = SSM_GROUPS // SSM_GB
    return pl.pallas_call(
        _inproj_kernel,
        grid=(t // tm,),
        in_specs=[_row_spec(tm, d),
                  _resident_spec((None, d, D_IN_PROJ), lambda m: (layer, 0, 0)),
                  _layer_vec_spec(layer, HEAD_DIM),
                  _layer_vec_spec(layer, HEAD_DIM),
                  pl.BlockSpec((tm, HEAD_DIM), lambda m: (m % sblocks, 0)),
                  pl.BlockSpec((tm, HEAD_DIM), lambda m: (m % sblocks, 0))],
        out_specs=[_row_spec(tm, ATTN_WIDTH), _row_spec(tm, KV_WIDTH), _row_spec(tm, KV_WIDTH),
                   pl.BlockSpec((nblk, tm // SSM_CHUNK, SSM_GB * SSM_CW), lambda m: (0, m, 0))],
        out_shape=[jax.ShapeDtypeStruct((t, ATTN_WIDTH), BF16),
                   jax.ShapeDtypeStruct((t, KV_WIDTH), BF16),
                   jax.ShapeDtypeStruct((t, KV_WIDTH), BF16),
                   jax.ShapeDtypeStruct((nblk, t // SSM_CHUNK, SSM_GB * SSM_CW), BF16)],
        scratch_shapes=[pltpu.VMEM((tm, LANES), F32)],
        compiler_params=_cparams(("parallel",)),
        name="inproj",
    )(xb, w_in, q_gain, k_gain, cos2, sin2)


def _attn_kernel(q_ref, k_ref, v_ref, g_ref, o_ref):
    outs = []
    ssq = None
    for h in range(N_HEADS):
        kv = h // GQA
        q = q_ref[:, h * HEAD_DIM:(h + 1) * HEAD_DIM]
        k = k_ref[:, kv * HEAD_DIM:(kv + 1) * HEAD_DIM]
        v = v_ref[:, kv * HEAD_DIM:(kv + 1) * HEAD_DIM]
        s = lax.dot_general(q, k, (((1,), (1,)), ((), ())), preferred_element_type=F32)
        m = jnp.max(s, axis=-1, keepdims=True)
        p = jnp.exp(s - m)
        l = jnp.sum(p, axis=-1, keepdims=True)
        o = jnp.dot(p.astype(BF16), v, preferred_element_type=F32) / l
        outs.append(o)
        sq = jnp.sum(o * o, axis=-1, keepdims=True)
        ssq = sq if ssq is None else ssq + sq
    r = lax.rsqrt(ssq * (1.0 / ATTN_WIDTH) + EPS)
    for h in range(N_HEADS):
        sl = slice(h * HEAD_DIM, (h + 1) * HEAD_DIM)
        o_ref[:, sl] = (outs[h] * r * g_ref[:, sl]).astype(BF16)


def _attention(q, k, v, gain, layer, bsz, seq_len, tq=512):
    t = q.shape[0]
    tq = min(tq, seq_len)
    qblocks = seq_len // tq
    return pl.pallas_call(
        _attn_kernel,
        grid=(bsz, qblocks),
        in_specs=[pl.BlockSpec((tq, ATTN_WIDTH), lambda b, i: (b * qblocks + i, 0)),
                  pl.BlockSpec((seq_len, KV_WIDTH), lambda b, i: (b, 0)),
                  pl.BlockSpec((seq_len, KV_WIDTH), lambda b, i: (b, 0)),
                  pl.BlockSpec((None, 1, ATTN_WIDTH), lambda b, i: (layer, 0, 0))],
        out_specs=pl.BlockSpec((tq, ATTN_WIDTH), lambda b, i: (b * qblocks + i, 0)),
        out_shape=jax.ShapeDtypeStruct((t, ATTN_WIDTH), BF16),
        compiler_params=_cparams(("parallel", "parallel")),
        name="attention",
    )(q, k, v, gain)


def _glu_kernel(y_ref, w_ref, b_ref, g_ref, o_ref):
    y = y_ref[...]
    gate = jnp.dot(y.astype(BF16), w_ref[...], preferred_element_type=F32) + b_ref[...]
    z = y * jax.nn.sigmoid(gate)
    zn = z * lax.rsqrt(jnp.mean(z * z, axis=-1, keepdims=True) + EPS) * g_ref[...]
    o_ref[...] = zn.astype(BF16)


def _glu(y, w_glu, b_glu, gain, layer, tm=512):
    t, w = y.shape
    tm = min(tm, t)
    return pl.pallas_call(
        _glu_kernel,
        grid=(t // tm,),
        in_specs=[_row_spec(tm, w),
                  pl.BlockSpec((None, w, w), lambda m: (layer, 0, 0)),
                  _layer_vec_spec(layer, w),
                  _layer_vec_spec(layer, w)],
        out_specs=_row_spec(tm, w),
        out_shape=jax.ShapeDtypeStruct((t, w), BF16),
        compiler_params=_cparams(("parallel",)),
        name="glu",
    )(y, w_glu, b_glu, gain)


ROUTE_LANE_GROUP = N_EXPERTS


def _route(logits):
    lane = lax.broadcasted_iota(jnp.int32, logits.shape, 1)
    neg = jnp.float32(-1e30)
    big = jnp.int32(LANES)

    def masked_softmax(mask):
        z = jnp.where(mask, logits, neg)
        zmax = jnp.max(z, axis=-1, keepdims=True)
        e = jnp.where(mask, jnp.exp(z - zmax), 0.0)
        return e / jnp.sum(e, axis=-1, keepdims=True)

    def first_argmax(vals, mask):
        top = jnp.max(jnp.where(mask, vals, -1.0), axis=-1, keepdims=True)
        idx = jnp.min(jnp.where(mask & (vals == top), lane, big), axis=-1, keepdims=True)
        return top, idx

    gmask = (lane >= ROUTE_LANE_GROUP) & (lane < ROUTE_LANE_GROUP + N_EXPERT_GROUPS)
    g_top, g_lane = first_argmax(masked_softmax(gmask), gmask)
    e_lo = (g_lane - ROUTE_LANE_GROUP) * EXPERTS_PER_GROUP
    emask = (lane >= e_lo) & (lane < e_lo + EXPERTS_PER_GROUP)
    e_prob = masked_softmax(emask)
    v1, i1 = first_argmax(e_prob, emask)
    v2, i2 = first_argmax(e_prob, emask & (lane != i1))
    denom = v1 + v2
    return i1, i2, g_top * (v1 / denom), g_top * (v2 / denom)


TOK_ROWS = D_MODEL // LANES
MOE_TM = 256
DMA_UNROLL = 8
GATHER_PITCH = 24
RL_E1, RL_E2, RL_G1, RL_G2, RL_R1, RL_R2 = 0, 1, 2, 3, 4, 5


def _to_token_major(ref, x):
    tm = x.shape[0]
    for s in range(TOK_ROWS):
        ref[pl.ds(s, tm, stride=TOK_ROWS), :] = x[:, s * LANES:(s + 1) * LANES]


def _from_token_major(ref, tm, s):
    return ref[pl.ds(s, tm, stride=TOK_ROWS), :]


def _outproj_kernel(a_ref, y_ref, wa_ref, wy_ref, x_ref, g_ref, b_ref, wr_ref, br_ref,
                    xb_ref, x16_ref, r_ref, meta_ref, cnt_ref, run_s):
    tm = a_ref.shape[0]

    @pl.when(pl.program_id(0) == 0)
    def _():
        run_s[...] = jnp.zeros_like(run_s)

    m = (jnp.dot(a_ref[...], wa_ref[...], preferred_element_type=F32)
         + jnp.dot(y_ref[...], wy_ref[...], preferred_element_type=F32))
    xn = _layer_norm(ALPHA * x_ref[...] + m, g_ref[...], b_ref[...])
    xn_hi = xn.astype(BF16)
    xb_ref[...] = xn_hi
    _to_token_major(x16_ref, xn)
    xn_lo = (xn - xn_hi.astype(F32)).astype(BF16)
    logits = (jnp.dot(xn_hi, wr_ref[0], preferred_element_type=F32)
              + jnp.dot(xn_hi, wr_ref[1], preferred_element_type=F32)
              + jnp.dot(xn_lo, wr_ref[0], preferred_element_type=F32)) + br_ref[...]
    i1, i2, g1, g2 = _route(logits)

    lane = lax.broadcasted_iota(jnp.int32, (tm, LANES), 1)
    hit1, hit2 = lane == i1, lane == i2
    onehot = (hit1 | hit2).astype(BF16)
    earlier = (lax.broadcasted_iota(jnp.int32, (tm, tm), 0)
               > lax.broadcasted_iota(jnp.int32, (tm, tm), 1)).astype(BF16)
    rank = jnp.dot(earlier, onehot, preferred_element_type=F32) + run_s[0:1, :]
    r1 = jnp.sum(jnp.where(hit1, rank, 0.0), axis=-1, keepdims=True)
    r2 = jnp.sum(jnp.where(hit2, rank, 0.0), axis=-1, keepdims=True)
    total = run_s[0:1, :] + jnp.sum(onehot.astype(F32), axis=0, keepdims=True)
    run_s[...] = jnp.broadcast_to(total, run_s.shape)
    cnt_ref[...] = jnp.broadcast_to(total, cnt_ref.shape)
    out = jnp.zeros((tm, LANES), F32)
    for col, val in ((RL_E1, i1.astype(F32)), (RL_E2, i2.astype(F32)), (RL_G1, g1), (RL_G2, g2),
                     (RL_R1, r1), (RL_R2, r2)):
        out = jnp.where(lane == col, val, out)
    r_ref[...] = out
    meta_ref[...] = out.T[:SUBLANES, :].astype(jnp.int32)


def _outproj(attn_n, y_n, w_out, xf, ln_g, ln_b, w_route, b_route, layer, tm=512):
    t, d = xf.shape
    tm = min(tm, t)
    half = D_MIX // 2
    return pl.pallas_call(
        _outproj_kernel,
        grid=(t // tm,),
        in_specs=[_row_spec(tm, half), _row_spec(tm, half),
                  _resident_spec((None, half, d), lambda m: (layer, 0, 0)),
                  _resident_spec((None, half, d), lambda m: (layer, 1, 0)),
                  _row_spec(tm, d),
                  _layer_vec_spec(layer, d), _layer_vec_spec(layer, d),
                  _resident_spec((None, 2, d, LANES), lambda m: (layer, 0, 0, 0)),
                  _layer_vec_spec(layer, LANES)],
        out_specs=[_row_spec(tm, d), _row_spec(tm * TOK_ROWS, LANES), _row_spec(tm, LANES),
                   pl.BlockSpec((SUBLANES, tm), lambda m: (0, m)),
                   pl.BlockSpec((SUBLANES, LANES), lambda m: (0, 0))],
        out_shape=[jax.ShapeDtypeStruct((t, d), BF16),
                   jax.ShapeDtypeStruct((t * TOK_ROWS, LANES), F32),
                   jax.ShapeDtypeStruct((t, LANES), F32),
                   jax.ShapeDtypeStruct((SUBLANES, t), jnp.int32),
                   jax.ShapeDtypeStruct((SUBLANES, LANES), F32)],
        scratch_shapes=[pltpu.VMEM((SUBLANES, LANES), F32)],
        compiler_params=_cparams(("arbitrary",)),
        name="outproj_ln_route",
    )(attn_n, y_n, w_out, w_out, xf, ln_g, ln_b, w_route, b_route)


def _token_copy(src, src_tok, dst, dst_tok, sem, dst_pitch=TOK_ROWS):
    return pltpu.make_async_copy(
        src.at[pl.ds(pl.multiple_of(src_tok * TOK_ROWS, TOK_ROWS), TOK_ROWS), :],
        dst.at[pl.ds(pl.multiple_of(dst_tok * dst_pitch, SUBLANES), TOK_ROWS), :],
        sem)


def _sorted_row(es_ref, e_ref, r_ref, t):
    return es_ref[e_ref[t]] + r_ref[t]


def _dispatch_kernel(es_ref, e1_ref, e2_ref, r1_ref, r2_ref, x16_ref, xb_ref, wg_ref, bg_ref, xs_in_ref,
                     xs_ref, gate_ref, sem, *, tm):
    del xs_in_ref
    for t in range(tm):
        _token_copy(x16_ref, t, xs_ref, _sorted_row(es_ref, e1_ref, r1_ref, t), sem.at[0]).start()
        _token_copy(x16_ref, t, xs_ref, _sorted_row(es_ref, e2_ref, r2_ref, t), sem.at[1]).start()
    gate = jnp.dot(xb_ref[...], wg_ref[...], preferred_element_type=F32) + bg_ref[...]
    gate_ref[...] = jax.nn.sigmoid(gate).astype(BF16)
    whole = pl.ds(0, tm * TOK_ROWS)
    pltpu.make_async_copy(x16_ref, xs_ref.at[whole, :], sem.at[0]).wait()
    pltpu.make_async_copy(x16_ref, xs_ref.at[whole, :], sem.at[1]).wait()


def _smem_tile_spec(tm):
    return pl.BlockSpec((tm,), lambda m: (m,), memory_space=pltpu.SMEM)


def _token_meta_specs(tm):
    return [pl.BlockSpec(memory_space=pltpu.SMEM)] + [_smem_tile_spec(tm)] * 4


def _dispatch(x16, e_start, meta, xs_buf, xb, w_pg, b_pg, layer, tm=512):
    t, d = xb.shape
    tm = min(tm, t)
    return pl.pallas_call(
        functools.partial(_dispatch_kernel, tm=tm),
        grid=(t // tm,),
        in_specs=_token_meta_specs(tm) + [_row_spec(tm * TOK_ROWS, LANES),
                                          _row_spec(tm, d),
                                          _resident_spec((None, d, d), lambda m: (layer, 0, 0)),
                                          _layer_vec_spec(layer, d),
                                          pl.BlockSpec(memory_space=pl.ANY)],
        out_specs=[pl.BlockSpec(memory_space=pl.ANY), _row_spec(tm, d)],
        out_shape=[jax.ShapeDtypeStruct(xs_buf.shape, F32), jax.ShapeDtypeStruct((t, d), BF16)],
        input_output_aliases={9: 0},
        scratch_shapes=[pltpu.SemaphoreType.DMA((2,))],
        compiler_params=_cparams(("arbitrary",)),
        name="moe_dispatch_gate",
    )(e_start, meta[RL_E1], meta[RL_E2], meta[RL_R1], meta[RL_R2], x16, xb, w_pg, b_pg, xs_buf)


def _expert_kernel(te_ref, tr_ref, xs_ref, wg_ref, wu_ref, wd_ref, ys_ref, lhs_s, wg_s, wu_s, wd_s, *, tm):
    j = pl.program_id(0)
    rows = tr_ref[j]

    @pl.when(rows == 0)
    def _():
        ys_ref[...] = jnp.zeros_like(ys_ref)

    @pl.when((rows > 0) & ((j == 0) | (te_ref[j] != te_ref[jnp.maximum(j - 1, 0)])))
    def _():
        wg_s[...] = wg_ref[...].astype(BF16)
        wu_s[...] = wu_ref[...].astype(BF16)
        wd_s[...] = wd_ref[...].astype(BF16)

    @pl.when(rows > 0)
    def _():
        valid = lax.broadcasted_iota(jnp.int32, (tm, LANES), 0) < rows
        for s in range(TOK_ROWS):
            blk = jnp.where(valid, _from_token_major(xs_ref, tm, s), 0.0)
            lhs_s[:, s * LANES:(s + 1) * LANES] = blk.astype(BF16)
        x = lhs_s[...]
        hg = jnp.dot(x, wg_s[...], preferred_element_type=F32)
        hu = jnp.dot(x, wu_s[...], preferred_element_type=F32)
        hid = ((hg * jax.nn.sigmoid(hg)) * hu).astype(BF16)
        for c in range(D_MODEL // D_EXPERT):
            out = jnp.dot(hid, wd_s[:, c * D_EXPERT:(c + 1) * D_EXPERT], preferred_element_type=F32)
            for h in range(D_EXPERT // LANES):
                s = c * (D_EXPERT // LANES) + h
                ys_ref[pl.ds(s, tm, stride=TOK_ROWS), :] = out[:, h * LANES:(h + 1) * LANES]


def _moe_n_tiles(t, tm):
    return (2 * t) // tm + N_EXPERTS


def _moe_tiles(counts, t, tm):
    n_tiles = _moe_n_tiles(t, tm)
    per_e = (counts + tm - 1) // tm
    ends = jnp.cumsum(per_e)
    starts = ends - per_e
    j = jnp.arange(n_tiles, dtype=jnp.int32)
    e = jnp.minimum(jnp.sum((j[:, None] >= ends[None, :]).astype(jnp.int32), axis=1), N_EXPERTS - 1)
    rows = jnp.clip(counts[e] - (j - starts[e]) * tm, 0, tm)
    rows = jnp.where(j < ends[-1], rows, 0)
    return e.astype(jnp.int32), rows.astype(jnp.int32), (starts * tm).astype(jnp.int32)


def _experts(xs, te, tr, w_gate, w_up, w_down, layer, tm):
    d = D_MODEL
    blk = pl.BlockSpec((tm * TOK_ROWS, LANES), lambda j, te, tr: (j, 0))
    grid_spec = pltpu.PrefetchScalarGridSpec(
        num_scalar_prefetch=2,
        grid=(te.shape[0],),
        in_specs=[blk,
                  pl.BlockSpec((None, None, d, D_EXPERT), lambda j, te, tr: (layer, te[j], 0, 0)),
                  pl.BlockSpec((None, None, d, D_EXPERT), lambda j, te, tr: (layer, te[j], 0, 0)),
                  pl.BlockSpec((None, None, D_EXPERT, d), lambda j, te, tr: (layer, te[j], 0, 0))],
        out_specs=blk,
        scratch_shapes=[pltpu.VMEM((tm, d), BF16),
                        pltpu.VMEM((d, D_EXPERT), BF16),
                        pltpu.VMEM((d, D_EXPERT), BF16),
                        pltpu.VMEM((D_EXPERT, d), BF16)],
    )
    return pl.pallas_call(
        functools.partial(_expert_kernel, tm=tm),
        grid_spec=grid_spec,
        out_shape=jax.ShapeDtypeStruct(xs.shape, F32),
        compiler_params=_cparams(("arbitrary",)),
        name="moe_experts",
    )(te, tr, xs, w_gate, w_up, w_down)


def _combine_kernel(es_ref, e1_ref, e2_ref, r1_ref, r2_ref, e1n_ref, e2n_ref, r1n_ref, r2n_ref,
                    gate_ref, x16_ref, r_ref, p_ref, wp_ref, g_ref, b_ref, ys_ref, of_ref, ob_ref,
                    y1a_s, y2a_s, y1b_s, y2b_s, h_s, sem, *, sub):
    i = pl.program_id(0)
    slots = ((y1a_s, y2a_s), (y1b_s, y2b_s))

    def issue(slot, e1, e2, r1, r2, off):
        y1_s, y2_s = slots[slot]
        for t in range(sub):
            _token_copy(ys_ref, _sorted_row(es_ref, e1, r1, off + t), y1_s, t, sem.at[slot, 0],
                        GATHER_PITCH).start()
            _token_copy(ys_ref, _sorted_row(es_ref, e2, r2, off + t), y2_s, t, sem.at[slot, 1],
                        GATHER_PITCH).start()

    def wait(slot):
        whole = pl.ds(0, sub * TOK_ROWS)
        for k in range(2):
            pltpu.make_async_copy(ys_ref.at[whole, :], slots[slot][k].at[whole, :], sem.at[slot, k]).wait()

    def combine(slot, off):
        y1_s, y2_s = slots[slot]
        rows = pl.ds(off, sub)
        ple = (jnp.dot(p_ref[rows, :].astype(BF16), wp_ref[...], preferred_element_type=F32)
               * gate_ref[rows, :].astype(F32))
        g1 = r_ref[rows, RL_G1:RL_G1 + 1]
        g2 = r_ref[rows, RL_G2:RL_G2 + 1]
        for s in range(TOK_ROWS):
            sl = slice(s * LANES, (s + 1) * LANES)
            f = (g1 * y1_s[pl.ds(s, sub, stride=GATHER_PITCH), :]
                 + g2 * y2_s[pl.ds(s, sub, stride=GATHER_PITCH), :])
            resid = x16_ref[pl.ds(off * TOK_ROWS + s, sub, stride=TOK_ROWS), :]
            h_s[:, sl] = ALPHA * resid + f + ple[:, sl]
        xn = _layer_norm(h_s[...], g_ref[...], b_ref[...])
        of_ref[rows, :] = xn
        ob_ref[rows, :] = xn.astype(BF16)

    @pl.when(i == 0)
    def _():
        issue(0, e1_ref, e2_ref, r1_ref, r2_ref, 0)

    wait(0)
    issue(1, e1_ref, e2_ref, r1_ref, r2_ref, sub)
    combine(0, 0)
    wait(1)
    issue(0, e1n_ref, e2n_ref, r1n_ref, r2n_ref, 0)
    combine(1, sub)

    @pl.when(i == pl.num_programs(0) - 1)
    def _():
        wait(0)


def _combine_ple_ln(gate, x16, route, e_start, meta, ys, p, w_ple, ln_g, ln_b, layer, sub=256):
    t, d = gate.shape
    sub = min(sub, t // 2)
    tm = 2 * sub
    steps = t // tm
    fields = (meta[RL_E1], meta[RL_E2], meta[RL_R1], meta[RL_R2])
    next_spec = pl.BlockSpec((sub,), lambda m: (jnp.minimum(2 * m + 2, 2 * steps - 2),),
                             memory_space=pltpu.SMEM)
    return pl.pallas_call(
        functools.partial(_combine_kernel, sub=sub),
        grid=(steps,),
        in_specs=_token_meta_specs(tm) + [next_spec] * 4 + [
            _row_spec(tm, d), _row_spec(tm * TOK_ROWS, LANES), _row_spec(tm, LANES),
            pl.BlockSpec((None, tm, PLE_DIM), lambda m: (layer, m, 0)),
            _resident_spec((None, PLE_DIM, d), lambda m: (layer, 0, 0)),
            _layer_vec_spec(layer, d), _layer_vec_spec(layer, d),
            pl.BlockSpec(memory_space=pl.ANY)],
        out_specs=[_row_spec(tm, d), _row_spec(tm, d)],
        out_shape=[jax.ShapeDtypeStruct((t, d), F32), jax.ShapeDtypeStruct((t, d), BF16)],
        scratch_shapes=[*[pltpu.VMEM((sub * GATHER_PITCH, LANES), F32) for _ in range(4)],
                        pltpu.VMEM((sub, d), F32),
                        pltpu.SemaphoreType.DMA((2, 2))],
        compiler_params=_cparams(("arbitrary",)),
        name="moe_combine_ple_ln",
    )(e_start, *fields, *fields, gate, x16, route, p, w_ple, ln_g, ln_b, ys)


def _rope_tables(seq_len):
    rows = seq_len // GRID_W
    row = jnp.repeat(jnp.arange(rows, dtype=F32), GRID_W)
    col = jnp.tile(jnp.arange(GRID_W, dtype=F32), rows)
    inv_freq = ROPE_THETA ** (-jnp.arange(ROPE_AXIS_PAIRS, dtype=F32) / ROPE_AXIS_PAIRS)
    ang = jnp.concatenate([row[:, None] * inv_freq, col[:, None] * inv_freq], axis=-1)
    cos, sin = jnp.cos(ang), jnp.sin(ang)
    return jnp.concatenate([cos, cos], axis=-1), jnp.concatenate([-sin, sin], axis=-1)


def kernel(x, p, ln0_g, ln0_b, w_in, q_norm_g, k_norm_g, ssm_a_re, ssm_a_im, ssm_log_dt, ssm_b_re, ssm_b_im, ssm_c_re, ssm_c_im, ssm_d, w_glu, b_glu, attn_out_g, ssm_out_g, w_out, ln1_g, ln1_b, router_group_w, router_group_b, router_expert_w, router_expert_b, w_gate_e, w_up_e, w_down_e, w_ple, w_ple_gate, b_ple_gate, ln2_g, ln2_b):
    bsz, seq_len, d = x.shape
    t = bsz * seq_len
    n_layers = w_in.shape[0]
    vec = lambda a: a.reshape(n_layers, 1, a.shape[-1])

    cos2, sin2 = _rope_tables(seq_len)
    w_in_b, w_glu_b, w_out_b = w_in.astype(BF16), w_glu.astype(BF16), w_out.astype(BF16)
    w_ple_b, w_pg_b = w_ple.astype(BF16), w_ple_gate.astype(BF16)
    pad = LANES - N_EXPERTS - N_EXPERT_GROUPS
    w_route = jnp.pad(jnp.concatenate([router_expert_w, router_group_w], axis=-1), ((0, 0), (0, 0), (0, pad)))
    w_route_hi = w_route.astype(BF16)
    w_route_lo = (w_route - w_route_hi.astype(F32)).astype(BF16)
    w_route = jnp.stack([w_route_hi, w_route_lo], axis=1)
    b_route = jnp.pad(jnp.concatenate([router_expert_b, router_group_b], axis=-1), ((0, 0), (0, pad)))
    s5_ops = jax.vmap(_s5_operands)(ssm_a_re, ssm_a_im, ssm_log_dt, ssm_b_re, ssm_b_im,
                                    ssm_c_re, ssm_c_im, ssm_d)
    p2 = p.reshape(n_layers, t, PLE_DIM)
    perm = _s5_perm()

    tm_e = min(MOE_TM, t)
    xs = jnp.zeros((_moe_n_tiles(t, tm_e) * tm_e * TOK_ROWS, LANES), F32)
    xf, xb = _ln0(x.reshape(t, d), ln0_g, ln0_b)
    for i in range(n_layers):
        q, k, v, u = _inproj(xb, w_in_b, vec(q_norm_g), vec(k_norm_g), cos2, sin2, i, seq_len)
        attn_n = _attention(q, k, v, vec(attn_out_g), i, bsz, seq_len)
        y = _s5_mixer(u, perm, *(op[i] for op in s5_ops), bsz)
        y_n = _glu(y, w_glu_b, vec(b_glu), vec(ssm_out_g), i)
        xb, x16, route, meta, counts = _outproj(attn_n, y_n, w_out_b, xf, vec(ln1_g), vec(ln1_b),
                                                w_route, vec(b_route), i)
        te, tr, e_start = _moe_tiles(counts[0, :N_EXPERTS].astype(jnp.int32), t, tm_e)
        xs, gate = _dispatch(x16, e_start, meta, xs, xb, w_pg_b, vec(b_ple_gate), i)
        ys = _experts(xs, te, tr, w_gate_e, w_up_e, w_down_e, i, tm_e)
        xf, xb = _combine_ple_ln(gate, x16, route, e_start, meta, ys, p2, w_ple_b,
                                 vec(ln2_g), vec(ln2_b), i)
    return xf.reshape(bsz, seq_len, d)
```

```python
import functools
import math

import jax
import jax.numpy as jnp
from jax import lax
from jax.experimental import pallas as pl
from jax.experimental.pallas import tpu as pltpu

F32 = jnp.float32
BF16 = jnp.bfloat16

D_MODEL = 2048
DEPTH = 4
GRID_W = 64
N_HEADS = 8
N_KV_HEADS = 2
GQA = N_HEADS // N_KV_HEADS
HEAD_DIM = 128
ATTN_WIDTH = N_HEADS * HEAD_DIM
KV_WIDTH = N_KV_HEADS * HEAD_DIM
QK_WIDTH = ATTN_WIDTH + KV_WIDTH
ROPE_THETA = 10000.0
ROPE_AXIS_PAIRS = HEAD_DIM // 4
SSM_WIDTH = D_MODEL // 2
SSM_CH = 16
SSM_GROUPS = SSM_WIDTH // SSM_CH
SSM_STATE = 64
D_MIX = ATTN_WIDTH + SSM_WIDTH
D_IN_PROJ = ATTN_WIDTH + 2 * KV_WIDTH + SSM_WIDTH
N_EXPERT_GROUPS = 4
EXPERTS_PER_GROUP = 4
N_EXPERTS = N_EXPERT_GROUPS * EXPERTS_PER_GROUP
D_EXPERT = 512
PLE_DIM = 256
ALPHA = (2 * DEPTH) ** 0.25
EPS = 1e-6

LANES = 128
SUBLANES = 8
VMEM_LIMIT = 56 * 1024 * 1024

SSM_CHUNK = 16
SSM_CW = SSM_CHUNK * SSM_CH
SSM_GB = 8


def _cparams(sem):
    return pltpu.CompilerParams(dimension_semantics=sem, vmem_limit_bytes=VMEM_LIMIT)


def _s5_discretize(a_re, a_im, log_dt, b_re, b_im):
    dt = jnp.exp(log_dt)[:, None]
    lam_re, lam_im = dt * a_re, dt * a_im
    mag = jnp.exp(lam_re)
    abar_re, abar_im = mag * jnp.cos(lam_im), mag * jnp.sin(lam_im)
    den = a_re * a_re + a_im * a_im
    nr, ni = abar_re - 1.0, abar_im
    coef_re = (nr * a_re + ni * a_im) / den
    coef_im = (ni * a_re - nr * a_im) / den
    bb_re = coef_re[..., None] * b_re - coef_im[..., None] * b_im
    bb_im = coef_re[..., None] * b_im + coef_im[..., None] * b_re
    return lam_re, lam_im, bb_re, bb_im


def _cpow(lam_re, lam_im, k):
    mag = jnp.exp(lam_re * k)
    return mag * jnp.cos(lam_im * k), mag * jnp.sin(lam_im * k)


def _s5_operands(a_re, a_im, log_dt, b_re, b_im, c_re, c_im, d_skip):
    hp = lax.Precision.HIGHEST
    L, H, G = SSM_CHUNK, SSM_CH, SSM_GROUPS
    lam_f = _s5_discretize(a_re[0], a_im[0], log_dt[0], b_re[0], b_im[0])
    lam_b = _s5_discretize(a_re[1], a_im[1], log_dt[1], b_re[1], b_im[1])
    tau = jnp.arange(L, dtype=F32)

    def powers(lam, expo):
        return _cpow(lam[0][:, None, :], lam[1][:, None, :], expo[None, :, None])

    def conv_kernel(lam, cr, ci):
        pr, pi = powers(lam, tau)
        bbr, bbi = lam[2], lam[3]
        wr = pr[..., None] * bbr[:, None] - pi[..., None] * bbi[:, None]
        wi = pr[..., None] * bbi[:, None] + pi[..., None] * bbr[:, None]
        return (jnp.einsum('gop,glpi->gloi', cr, wr, precision=hp)
                - jnp.einsum('gop,glpi->gloi', ci, wi, precision=hp))

    k_f = conv_kernel(lam_f, c_re[0], c_im[0])
    k_b = conv_kernel(lam_b, c_re[1], c_im[1])
    lag0 = (tau == 0.0).astype(F32)[None, :, None, None]
    k_f = k_f + lag0 * (d_skip[:, None, :, None] * jnp.eye(H, dtype=F32))
    lag = jnp.arange(L)[None, :] - jnp.arange(L)[:, None]
    l_idx = jnp.arange(L)[None, None, :]
    sel = jnp.concatenate([(lag[..., None] == l_idx), (-lag[..., None] == l_idx)], axis=-1).astype(F32)
    wm = jnp.einsum('stl,gloi->gsito', sel, jnp.concatenate([k_f, k_b], axis=1), precision=hp)
    wm = wm.reshape(G, L * H, L * H)

    def end_map(lam, expo):
        pr, pi = powers(lam, expo)
        bbr, bbi = lam[2].transpose(0, 2, 1)[:, None], lam[3].transpose(0, 2, 1)[:, None]
        pr, pi = pr[:, :, None, :], pi[:, :, None, :]
        return pr * bbr - pi * bbi, pr * bbi + pi * bbr

    ef_re, ef_im = end_map(lam_f, (L - 1) - tau)
    eb_re, eb_im = end_map(lam_b, tau)
    we = jnp.concatenate([ef_re, eb_re, ef_im, eb_im], axis=-1).reshape(G, L * H, 4 * SSM_STATE)

    def out_map(lam, cr, ci, expo):
        pr, pi = powers(lam, expo)
        pr, pi = pr.transpose(0, 2, 1)[..., None], pi.transpose(0, 2, 1)[..., None]
        cr, ci = cr.transpose(0, 2, 1)[:, :, None, :], ci.transpose(0, 2, 1)[:, :, None, :]
        return cr * pr - ci * pi, -(cr * pi + ci * pr)

    of_re, of_im = out_map(lam_f, c_re[0], c_im[0], tau + 1.0)
    ob_re, ob_im = out_map(lam_b, c_re[1], c_im[1], L - tau)
    w2 = jnp.concatenate([of_re, ob_re, of_im, ob_im], axis=1).reshape(G, 4 * SSM_STATE, L * H)

    afr, afi = _cpow(lam_f[0], lam_f[1], float(L))
    abr, abi = _cpow(lam_b[0], lam_b[1], float(L))
    ar = jnp.concatenate([afr, abr], axis=-1)
    ai = jnp.concatenate([afi, abi], axis=-1)
    return wm.astype(BF16), we.astype(BF16), w2.astype(BF16), ar, ai


_S5_ROWS = 264


def _gelu_tanh(y):
    return 0.5 * y * (1.0 + jnp.tanh(math.sqrt(2.0 / math.pi) * (y + 0.044715 * (y * y * y))))


def _s5_perm():
    n = SSM_GB * SSM_CW
    r = lax.broadcasted_iota(jnp.int32, (n, n), 0)
    c = lax.broadcasted_iota(jnp.int32, (n, n), 1)
    r_step, r_grp, r_ch = r // LANES, (r % LANES) // SSM_CH, r % SSM_CH
    c_grp, c_step, c_ch = c // SSM_CW, (c % SSM_CW) // SSM_CH, c % SSM_CH
    return ((r_step == c_step) & (r_grp == c_grp) & (r_ch == c_ch)).astype(BF16)


def _s5_kernel(u_ref, perm_ref, wm_ref, we_ref, w2_ref, ar_ref, ai_ref, y_ref,
               er_s, ei_s, fr_s, fi_s, br_s, bi_s, yi_s, yg_s, *, n_chunks):
    nc = n_chunks
    zero_tile = jnp.zeros((SUBLANES, LANES), F32)
    ug = jnp.dot(u_ref[...], perm_ref[...], preferred_element_type=F32).astype(BF16)
    for g in range(SSM_GB):
        base = g * _S5_ROWS
        u_g = ug[:, g * SSM_CW:(g + 1) * SSM_CW]
        yi_s[g] = jnp.dot(u_g, wm_ref[g], preferred_element_type=F32)
        ends = jnp.dot(u_g, we_ref[g], preferred_element_type=F32)
        er_s[pl.ds(base, nc), :] = ends[:, :LANES]
        ei_s[pl.ds(base, nc), :] = ends[:, LANES:]
        fr_s[pl.ds(base, SUBLANES), :] = zero_tile
        fi_s[pl.ds(base, SUBLANES), :] = zero_tile
        br_s[pl.ds(base + nc, SUBLANES), :] = zero_tile
        bi_s[pl.ds(base + nc, SUBLANES), :] = zero_tile

    ar = ar_ref[...]
    ai = ai_ref[...]
    fwd_lane = lax.broadcasted_iota(jnp.int32, (SUBLANES, LANES), 1) < SSM_STATE

    def step(i, carry):
        xr, xi = carry
        rows_f = pl.ds(i, SUBLANES, stride=_S5_ROWS)
        rows_b = pl.ds(nc - 1 - i, SUBLANES, stride=_S5_ROWS)
        er = jnp.where(fwd_lane, er_s[rows_f, :], er_s[rows_b, :])
        ei = jnp.where(fwd_lane, ei_s[rows_f, :], ei_s[rows_b, :])
        nr = ar * xr - ai * xi + er
        ni = ar * xi + ai * xr + ei
        to_f = pl.ds(i + 1, SUBLANES, stride=_S5_ROWS)
        to_b = pl.ds(nc + 6 - i, SUBLANES, stride=_S5_ROWS)
        fr_s[to_f, :] = nr
        fi_s[to_f, :] = ni
        br_s[to_b, :] = nr
        bi_s[to_b, :] = ni
        return nr, ni

    lax.fori_loop(0, nc, step, (zero_tile, zero_tile))

    fwd_col = lax.broadcasted_iota(jnp.int32, (nc, LANES), 1) < SSM_STATE
    for g in range(SSM_GB):
        base = g * _S5_ROWS
        cr = jnp.where(fwd_col, fr_s[pl.ds(base, nc), :], br_s[pl.ds(base + SUBLANES, nc), :])
        ci = jnp.where(fwd_col, fi_s[pl.ds(base, nc), :], bi_s[pl.ds(base + SUBLANES, nc), :])
        y = (yi_s[g]
             + jnp.dot(cr.astype(BF16), w2_ref[g, :LANES, :], preferred_element_type=F32)
             + jnp.dot(ci.astype(BF16), w2_ref[g, LANES:, :], preferred_element_type=F32))
        yg_s[:, g * SSM_CW:(g + 1) * SSM_CW] = _gelu_tanh(y).astype(BF16)

    z = lax.dot_general(yg_s[...], perm_ref[...], (((1,), (1,)), ((), ())), preferred_element_type=F32)
    for j in range(SSM_CHUNK):
        y_ref[pl.ds(j, nc, stride=SSM_CHUNK), :] = z[:, j * LANES:(j + 1) * LANES]


def _s5_mixer(u3, perm, wm, we, w2, ar, ai, layer, bsz):
    nblk, rows, width = u3.shape
    nc = rows // bsz
    gb = SSM_GB
    cw = SSM_CW
    kern = functools.partial(_s5_kernel, n_chunks=nc)
    return pl.pallas_call(
        kern,
        grid=(bsz, nblk),
        in_specs=[
            pl.BlockSpec((None, nc, width), lambda b, j: (j, b, 0)),
            pl.BlockSpec((width, width), lambda b, j: (0, 0)),
            pl.BlockSpec((None, gb, cw, cw), lambda b, j: (layer, j, 0, 0)),
            pl.BlockSpec((None, gb, cw, cw), lambda b, j: (layer, j, 0, 0)),
            pl.BlockSpec((None, gb, cw, cw), lambda b, j: (layer, j, 0, 0)),
            pl.BlockSpec((None, gb, LANES), lambda b, j: (layer, j, 0)),
            pl.BlockSpec((None, gb, LANES), lambda b, j: (layer, j, 0)),
        ],
        out_specs=pl.BlockSpec((nc * SSM_CHUNK, LANES), lambda b, j: (b, j)),
        out_shape=jax.ShapeDtypeStruct((bsz * nc * SSM_CHUNK, nblk * LANES), F32),
        scratch_shapes=[
            *[pltpu.VMEM((gb * _S5_ROWS, LANES), F32) for _ in range(6)],
            pltpu.VMEM((gb, nc, cw), F32),
            pltpu.VMEM((nc, width), BF16),
        ],
        compiler_params=_cparams(("parallel", "parallel")),
        name="s5_mixer",
    )(u3, perm, wm, we, w2, ar, ai)


def _layer_norm(h, g, b):
    mu = jnp.mean(h, axis=-1, keepdims=True)
    c = h - mu
    var = jnp.mean(c * c, axis=-1, keepdims=True)
    return c * lax.rsqrt(var + EPS) * g + b


def _ln0_kernel(x_ref, g_ref, b_ref, xf_ref, xb_ref):
    y = _layer_norm(x_ref[...], g_ref[...], b_ref[...])
    xf_ref[...] = y
    xb_ref[...] = y.astype(BF16)


def _row_spec(tm, width):
    return pl.BlockSpec((tm, width), lambda m: (m, 0))


def _resident_spec(block_shape, index_map):
    return pl.BlockSpec(block_shape, index_map, pipeline_mode=pl.Buffered(1))


def _layer_vec_spec(layer, width):
    return pl.BlockSpec((None, 1, width), lambda m: (layer, 0, 0))


def _ln0(x, g, b, tm=256):
    t, d = x.shape
    tm = min(tm, t)
    return pl.pallas_call(
        _ln0_kernel,
        grid=(t // tm,),
        in_specs=[_row_spec(tm, d),
                  pl.BlockSpec((1, d), lambda m: (0, 0)),
                  pl.BlockSpec((1, d), lambda m: (0, 0))],
        out_specs=[_row_spec(tm, d), _row_spec(tm, d)],
        out_shape=[jax.ShapeDtypeStruct((t, d), F32), jax.ShapeDtypeStruct((t, d), BF16)],
        compiler_params=_cparams(("parallel",)),
        name="ln0",
    )(x, g.reshape(1, d), b.reshape(1, d))


def _inproj_kernel(x_ref, w_ref, qg_ref, kg_ref, cos_ref, sin_ref, q_ref, k_ref, v_ref, u_ref, u_s):
    x = x_ref[...]
    cos2 = cos_ref[...]
    sin2 = sin_ref[...]
    scale = HEAD_DIM ** -0.5

    def head(z, gain, mult):
        zn = z * lax.rsqrt(jnp.mean(z * z, axis=-1, keepdims=True) + EPS) * gain
        rot = pltpu.roll(zn, HEAD_DIM // 2, axis=1)
        out = zn * cos2 + rot * sin2
        return out * mult if mult != 1.0 else out

    pair = 2 * HEAD_DIM
    for c in range(D_IN_PROJ // pair):
        acc = jnp.dot(x, w_ref[:, c * pair:(c + 1) * pair], preferred_element_type=F32)
        col = c * pair
        if col < ATTN_WIDTH:
            for h in range(2):
                z = head(acc[:, h * HEAD_DIM:(h + 1) * HEAD_DIM], qg_ref[...], scale)
                q_ref[:, col + h * HEAD_DIM: col + (h + 1) * HEAD_DIM] = z.astype(BF16)
        elif col < QK_WIDTH:
            for h in range(2):
                z = head(acc[:, h * HEAD_DIM:(h + 1) * HEAD_DIM], kg_ref[...], 1.0)
                k_ref[:, h * HEAD_DIM:(h + 1) * HEAD_DIM] = z.astype(BF16)
        elif col < QK_WIDTH + KV_WIDTH:
            v_ref[...] = acc.astype(BF16)
        else:
            tm = acc.shape[0]
            for half in range(2):
                blk = (col - (QK_WIDTH + KV_WIDTH)) // LANES + half
                u_s[...] = acc[:, half * LANES:(half + 1) * LANES]
                for j in range(SSM_CHUNK):
                    step_rows = u_s[pl.ds(j, tm // SSM_CHUNK, stride=SSM_CHUNK), :]
                    u_ref[blk, :, j * LANES:(j + 1) * LANES] = step_rows.astype(BF16)


def _inproj(xb, w_in, q_gain, k_gain, cos2, sin2, layer, seq_len, tm=512):
    t, d = xb.shape
    tm = min(tm, seq_len)
    sblocks = seq_len // tm
    nblk = SSM_GROUPS // SSM_GB
    return pl.pallas_call(
        _inproj_kernel,
        grid=(t // tm,),
        in_specs=[_row_spec(tm, d),
                  _resident_spec((None, d, D_IN_PROJ), lambda m: (layer, 0, 0)),
                  _layer_vec_spec(layer, HEAD_DIM),
                  _layer_vec_spec(layer, HEAD_DIM),
                  pl.BlockSpec((tm, HEAD_DIM), lambda m: (m % sblocks, 0)),
                  pl.BlockSpec((tm, HEAD_DIM), lambda m: (m % sblocks, 0))],
        out_specs=[_row_spec(tm, ATTN_WIDTH), _row_spec(tm, KV_WIDTH), _row_spec(tm, KV_WIDTH),
                   pl.BlockSpec((nblk, tm // SSM_CHUNK, SSM_GB * SSM_CW), lambda m: (0, m, 0))],
        out_shape=[jax.ShapeDtypeStruct((t, ATTN_WIDTH), BF16),
                   jax.ShapeDtypeStruct((t, KV_WIDTH), BF16),
                   jax.ShapeDtypeStruct((t, KV_WIDTH), BF16),
                   jax.ShapeDtypeStruct((nblk, t // SSM_CHUNK, SSM_GB * SSM_CW), BF16)],
        scratch_shapes=[pltpu.VMEM((tm, LANES), F32)],
        compiler_params=_cparams(("parallel",)),
        name="inproj",
    )(xb, w_in, q_gain, k_gain, cos2, sin2)


def _attn_kernel(q_ref, k_ref, v_ref, g_ref, o_ref):
    outs = []
    ssq = None
    for h in range(N_HEADS):
        kv = h // GQA
        q = q_ref[:, h * HEAD_DIM:(h + 1) * HEAD_DIM]
        k = k_ref[:, kv * HEAD_DIM:(kv + 1) * HEAD_DIM]
        v = v_ref[:, kv * HEAD_DIM:(kv + 1) * HEAD_DIM]
        s = lax.dot_general(q, k, (((1,), (1,)), ((), ())), preferred_element_type=F32)
        m = jnp.max(s, axis=-1, keepdims=True)
        p = jnp.exp(s - m)
        l = jnp.sum(p, axis=-1, keepdims=True)
        o = jnp.dot(p.astype(BF16), v, preferred_element_type=F32) / l
        outs.append(o)
        sq = jnp.sum(o * o, axis=-1, keepdims=True)
        ssq = sq if ssq is None else ssq + sq
    r = lax.rsqrt(ssq * (1.0 / ATTN_WIDTH) + EPS)
    for h in range(N_HEADS):
        sl = slice(h * HEAD_DIM, (h + 1) * HEAD_DIM)
        o_ref[:, sl] = (outs[h] * r * g_ref[:, sl]).astype(BF16)


def _attention(q, k, v, gain, layer, bsz, seq_len, tq=512):
    t = q.shape[0]
    tq = min(tq, seq_len)
    qblocks = seq_len // tq
    return pl.pallas_call(
        _attn_kernel,
        grid=(bsz, qblocks),
        in_specs=[pl.BlockSpec((tq, ATTN_WIDTH), lambda b, i: (b * qblocks + i, 0)),
                  pl.BlockSpec((seq_len, KV_WIDTH), lambda b, i: (b, 0)),
                  pl.BlockSpec((seq_len, KV_WIDTH), lambda b, i: (b, 0)),
                  pl.BlockSpec((None, 1, ATTN_WIDTH), lambda b, i: (layer, 0, 0))],
        out_specs=pl.BlockSpec((tq, ATTN_WIDTH), lambda b, i: (b * qblocks + i, 0)),
        out_shape=jax.ShapeDtypeStruct((t, ATTN_WIDTH), BF16),
        compiler_params=_cparams(("parallel", "parallel")),
        name="attention",
    )(q, k, v, gain)


def _glu_kernel(y_ref, w_ref, b_ref, g_ref, o_ref):
    y = y_ref[...]
    gate = jnp.dot(y.astype(BF16), w_ref[...], preferred_element_type=F32) + b_ref[...]
    z = y * jax.nn.sigmoid(gate)
    zn = z * lax.rsqrt(jnp.mean(z * z, axis=-1, keepdims=True) + EPS) * g_ref[...]
    o_ref[...] = zn.astype(BF16)


def _glu(y, w_glu, b_glu, gain, layer, tm=512):
    t, w = y.shape
    tm = min(tm, t)
    return pl.pallas_call(
        _glu_kernel,
        grid=(t // tm,),
        in_specs=[_row_spec(tm, w),
                  pl.BlockSpec((None, w, w), lambda m: (layer, 0, 0)),
                  _layer_vec_spec(layer, w),
                  _layer_vec_spec(layer, w)],
        out_specs=_row_spec(tm, w),
        out_shape=jax.ShapeDtypeStruct((t, w), BF16),
        compiler_params=_cparams(("parallel",)),
        name="glu",
    )(y, w_glu, b_glu, gain)


ROUTE_LANE_GROUP = N_EXPERTS


def _route(logits):
    lane = lax.broadcasted_iota(jnp.int32, logits.shape, 1)
    neg = jnp.float32(-1e30)
    big = jnp.int32(LANES)

    def masked_softmax(mask):
        z = jnp.where(mask, logits, neg)
        zmax = jnp.max(z, axis=-1, keepdims=True)
        e = jnp.where(mask, jnp.exp(z - zmax), 0.0)
        return e / jnp.sum(e, axis=-1, keepdims=True)

    def first_argmax(vals, mask):
        top = jnp.max(jnp.where(mask, vals, -1.0), axis=-1, keepdims=True)
        idx = jnp.min(jnp.where(mask & (vals == top), lane, big), axis=-1, keepdims=True)
        return top, idx

    gmask = (lane >= ROUTE_LANE_GROUP) & (lane < ROUTE_LANE_GROUP + N_EXPERT_GROUPS)
    g_top, g_lane = first_argmax(masked_softmax(gmask), gmask)
    e_lo = (g_lane - ROUTE_LANE_GROUP) * EXPERTS_PER_GROUP
    emask = (lane >= e_lo) & (lane < e_lo + EXPERTS_PER_GROUP)
    e_prob = masked_softmax(emask)
    v1, i1 = first_argmax(e_prob, emask)
    v2, i2 = first_argmax(e_prob, emask & (lane != i1))
    denom = v1 + v2
    return i1, i2, g_top * (v1 / denom), g_top * (v2 / denom)


TOK_ROWS = D_MODEL // LANES
MOE_TM = 256
DMA_UNROLL = 8
GATHER_PITCH = 24
RL_E1, RL_E2, RL_G1, RL_G2, RL_R1, RL_R2 = 0, 1, 2, 3, 4, 5


def _to_token_major(ref, x):
    tm = x.shape[0]
    for s in range(TOK_ROWS):
        ref[pl.ds(s, tm, stride=TOK_ROWS), :] = x[:, s * LANES:(s + 1) * LANES]


def _from_token_major(ref, tm, s):
    return ref[pl.ds(s, tm, stride=TOK_ROWS), :]


def _outproj_kernel(a_ref, y_ref, wa_ref, wy_ref, x_ref, g_ref, b_ref, wr_ref, br_ref,
                    xb_ref, x16_ref, r_ref, meta_ref, cnt_ref, run_s):
    tm = a_ref.shape[0]

    @pl.when(pl.program_id(0) == 0)
    def _():
        run_s[...] = jnp.zeros_like(run_s)

    m = (jnp.dot(a_ref[...], wa_ref[...], preferred_element_type=F32)
         + jnp.dot(y_ref[...], wy_ref[...], preferred_element_type=F32))
    xn = _layer_norm(ALPHA * x_ref[...] + m, g_ref[...], b_ref[...])
    xn_hi = xn.astype(BF16)
    xb_ref[...] = xn_hi
    _to_token_major(x16_ref, xn)
    xn_lo = (xn - xn_hi.astype(F32)).astype(BF16)
    logits = (jnp.dot(xn_hi, wr_ref[0], preferred_element_type=F32)
              + jnp.dot(xn_hi, wr_ref[1], preferred_element_type=F32)
              + jnp.dot(xn_lo, wr_ref[0], preferred_element_type=F32)) + br_ref[...]
    i1, i2, g1, g2 = _route(logits)

    lane = lax.broadcasted_iota(jnp.int32, (tm, LANES), 1)
    hit1, hit2 = lane == i1, lane == i2
    onehot = (hit1 | hit2).astype(BF16)
    earlier = (lax.broadcasted_iota(jnp.int32, (tm, tm), 0)
               > lax.broadcasted_iota(jnp.int32, (tm, tm), 1)).astype(BF16)
    rank = jnp.dot(earlier, onehot, preferred_element_type=F32) + run_s[0:1, :]
    r1 = jnp.sum(jnp.where(hit1, rank, 0.0), axis=-1, keepdims=True)
    r2 = jnp.sum(jnp.where(hit2, rank, 0.0), axis=-1, keepdims=True)
    total = run_s[0:1, :] + jnp.sum(onehot.astype(F32), axis=0, keepdims=True)
    run_s[...] = jnp.broadcast_to(total, run_s.shape)
    cnt_ref[...] = jnp.broadcast_to(total, cnt_ref.shape)
    out = jnp.zeros((tm, LANES), F32)
    for col, val in ((RL_E1, i1.astype(F32)), (RL_E2, i2.astype(F32)), (RL_G1, g1), (RL_G2, g2),
                     (RL_R1, r1), (RL_R2, r2)):
        out = jnp.where(lane == col, val, out)
    r_ref[...] = out
    meta_ref[...] = out.T[:SUBLANES, :].astype(jnp.int32)


def _outproj(attn_n, y_n, w_out, xf, ln_g, ln_b, w_route, b_route, layer, tm=512):
    t, d = xf.shape
    tm = min(tm, t)
    half = D_MIX // 2
    return pl.pallas_call(
        _outproj_kernel,
        grid=(t // tm,),
        in_specs=[_row_spec(tm, half), _row_spec(tm, half),
                  _resident_spec((None, half, d), lambda m: (layer, 0, 0)),
                  _resident_spec((None, half, d), lambda m: (layer, 1, 0)),
                  _row_spec(tm, d),
                  _layer_vec_spec(layer, d), _layer_vec_spec(layer, d),
                  _resident_spec((None, 2, d, LANES), lambda m: (layer, 0, 0, 0)),
                  _layer_vec_spec(layer, LANES)],
        out_specs=[_row_spec(tm, d), _row_spec(tm * TOK_ROWS, LANES), _row_spec(tm, LANES),
                   pl.BlockSpec((SUBLANES, tm), lambda m: (0, m)),
                   pl.BlockSpec((SUBLANES, LANES), lambda m: (0, 0))],
        out_shape=[jax.ShapeDtypeStruct((t, d), BF16),
                   jax.ShapeDtypeStruct((t * TOK_ROWS, LANES), F32),
                   jax.ShapeDtypeStruct((t, LANES), F32),
                   jax.ShapeDtypeStruct((SUBLANES, t), jnp.int32),
                   jax.ShapeDtypeStruct((SUBLANES, LANES), F32)],
        scratch_shapes=[pltpu.VMEM((SUBLANES, LANES), F32)],
        compiler_params=_cparams(("arbitrary",)),
        name="outproj_ln_route",
    )(attn_n, y_n, w_out, w_out, xf, ln_g, ln_b, w_route, b_route)


def _token_copy(src, src_tok, dst, dst_tok, sem, dst_pitch=TOK_ROWS):
    return pltpu.make_async_copy(
        src.at[pl.ds(pl.multiple_of(src_tok * TOK_ROWS, TOK_ROWS), TOK_ROWS), :],
        dst.at[pl.ds(pl.multiple_of(dst_tok * dst_pitch, SUBLANES), TOK_ROWS), :],
        sem)


def _sorted_row(es_ref, e_ref, r_ref, t):
    return es_ref[e_ref[t]] + r_ref[t]


def _dispatch_kernel(es_ref, e1_ref, e2_ref, r1_ref, r2_ref, x16_ref, xb_ref, wg_ref, bg_ref, xs_in_ref,
                     xs_ref, gate_ref, sem, *, tm):
    del xs_in_ref
    for t in range(tm):
        _token_copy(x16_ref, t, xs_ref, _sorted_row(es_ref, e1_ref, r1_ref, t), sem.at[0]).start()
        _token_copy(x16_ref, t, xs_ref, _sorted_row(es_ref, e2_ref, r2_ref, t), sem.at[1]).start()
    gate = jnp.dot(xb_ref[...], wg_ref[...], preferred_element_type=F32) + bg_ref[...]
    gate_ref[...] = jax.nn.sigmoid(gate).astype(BF16)
    whole = pl.ds(0, tm * TOK_ROWS)
    pltpu.make_async_copy(x16_ref, xs_ref.at[whole, :], sem.at[0]).wait()
    pltpu.make_async_copy(x16_ref, xs_ref.at[whole, :], sem.at[1]).wait()


def _smem_tile_spec(tm):
    return pl.BlockSpec((tm,), lambda m: (m,), memory_space=pltpu.SMEM)


def _token_meta_specs(tm):
    return [pl.BlockSpec(memory_space=pltpu.SMEM)] + [_smem_tile_spec(tm)] * 4


def _dispatch(x16, e_start, meta, xs_buf, xb, w_pg, b_pg, layer, tm=512):
    t, d = xb.shape
    tm = min(tm, t)
    return pl.pallas_call(
        functools.partial(_dispatch_kernel, tm=tm),
        grid=(t // tm,),
        in_specs=_token_meta_specs(tm) + [_row_spec(tm * TOK_ROWS, LANES),
                                          _row_spec(tm, d),
                                          _resident_spec((None, d, d), lambda m: (layer, 0, 0)),
                                          _layer_vec_spec(layer, d),
                                          pl.BlockSpec(memory_space=pl.ANY)],
        out_specs=[pl.BlockSpec(memory_space=pl.ANY), _row_spec(tm, d)],
        out_shape=[jax.ShapeDtypeStruct(xs_buf.shape, F32), jax.ShapeDtypeStruct((t, d), BF16)],
        input_output_aliases={9: 0},
        scratch_shapes=[pltpu.SemaphoreType.DMA((2,))],
        compiler_params=_cparams(("arbitrary",)),
        name="moe_dispatch_gate",
    )(e_start, meta[RL_E1], meta[RL_E2], meta[RL_R1], meta[RL_R2], x16, xb, w_pg, b_pg, xs_buf)


def _expert_kernel(te_ref, tr_ref, xs_ref, wg_ref, wu_ref, wd_ref, ys_ref, lhs_s, wg_s, wu_s, wd_s, *, tm):
    j = pl.program_id(0)
    rows = tr_ref[j]

    @pl.when(rows == 0)
    def _():
        ys_ref[...] = jnp.zeros_like(ys_ref)

    @pl.when((rows > 0) & ((j == 0) | (te_ref[j] != te_ref[jnp.maximum(j - 1, 0)])))
    def _():
        wg_s[...] = wg_ref[...].astype(BF16)
        wu_s[...] = wu_ref[...].astype(BF16)
        wd_s[...] = wd_ref[...].astype(BF16)

    @pl.when(rows > 0)
    def _():
        valid = lax.broadcasted_iota(jnp.int32, (tm, LANES), 0) < rows
        for s in range(TOK_ROWS):
            blk = jnp.where(valid, _from_token_major(xs_ref, tm, s), 0.0)
            lhs_s[:, s * LANES:(s + 1) * LANES] = blk.astype(BF16)
        x = lhs_s[...]
        hg = jnp.dot(x, wg_s[...], preferred_element_type=F32)
        hu = jnp.dot(x, wu_s[...], preferred_element_type=F32)
        hid = ((hg * jax.nn.sigmoid(hg)) * hu).astype(BF16)
        for c in range(D_MODEL // D_EXPERT):
            out = jnp.dot(hid, wd_s[:, c * D_EXPERT:(c + 1) * D_EXPERT], preferred_element_type=F32)
            for h in range(D_EXPERT // LANES):
                s = c * (D_EXPERT // LANES) + h
                ys_ref[pl.ds(s, tm, stride=TOK_ROWS), :] = out[:, h * LANES:(h + 1) * LANES]


def _moe_n_tiles(t, tm):
    return (2 * t) // tm + N_EXPERTS


def _moe_tiles(counts, t, tm):
    n_tiles = _moe_n_tiles(t, tm)
    per_e = (counts + tm - 1) // tm
    ends = jnp.cumsum(per_e)
    starts = ends - per_e
    j = jnp.arange(n_tiles, dtype=jnp.int32)[:, None]
    e = jnp.minimum(jnp.sum((j >= ends[None, :]).astype(jnp.int32), axis=1), N_EXPERTS - 1)
    mine = (j >= starts[None, :]) & (j < ends[None, :])
    rows = jnp.sum(jnp.where(mine, jnp.clip(counts[None, :] - (j - starts[None, :]) * tm, 0, tm), 0), axis=1)
    return e.astype(jnp.int32), rows.astype(jnp.int32), (starts * tm).astype(jnp.int32)


def _experts(xs, te, tr, w_gate, w_up, w_down, layer, tm):
    d = D_MODEL
    blk = pl.BlockSpec((tm * TOK_ROWS, LANES), lambda j, te, tr: (j, 0))
    grid_spec = pltpu.PrefetchScalarGridSpec(
        num_scalar_prefetch=2,
        grid=(te.shape[0],),
        in_specs=[blk,
                  pl.BlockSpec((None, None, d, D_EXPERT), lambda j, te, tr: (layer, te[j], 0, 0)),
                  pl.BlockSpec((None, None, d, D_EXPERT), lambda j, te, tr: (layer, te[j], 0, 0)),
                  pl.BlockSpec((None, None, D_EXPERT, d), lambda j, te, tr: (layer, te[j], 0, 0))],
        out_specs=blk,
        scratch_shapes=[pltpu.VMEM((tm, d), BF16),
                        pltpu.VMEM((d, D_EXPERT), BF16),
                        pltpu.VMEM((d, D_EXPERT), BF16),
                        pltpu.VMEM((D_EXPERT, d), BF16)],
    )
    return pl.pallas_call(
        functools.partial(_expert_kernel, tm=tm),
        grid_spec=grid_spec,
        out_shape=jax.ShapeDtypeStruct(xs.shape, F32),
        compiler_params=_cparams(("arbitrary",)),
        name="moe_experts",
    )(te, tr, xs, w_gate, w_up, w_down)


def _combine_kernel(es_ref, e1_ref, e2_ref, r1_ref, r2_ref, e1n_ref, e2n_ref, r1n_ref, r2n_ref,
                    gate_ref, x16_ref, r_ref, p_ref, wp_ref, g_ref, b_ref, ys_ref, of_ref, ob_ref,
                    *scratch, sub, n_sub, lead):
    slots = [(scratch[2 * k], scratch[2 * k + 1]) for k in range(n_sub)]
    h_s, sem = scratch[2 * n_sub], scratch[2 * n_sub + 1]
    i = pl.program_id(0)
    this_step = (e1_ref, e2_ref, r1_ref, r2_ref)
    next_step = (e1n_ref, e2n_ref, r1n_ref, r2n_ref)

    def issue(slot, fields, off):
        e1, e2, r1, r2 = fields
        y1_s, y2_s = slots[slot]
        for t in range(sub):
            _token_copy(ys_ref, _sorted_row(es_ref, e1, r1, off + t), y1_s, t, sem.at[slot, 0],
                        GATHER_PITCH).start()
            _token_copy(ys_ref, _sorted_row(es_ref, e2, r2, off + t), y2_s, t, sem.at[slot, 1],
                        GATHER_PITCH).start()

    def wait(slot):
        whole = pl.ds(0, sub * TOK_ROWS)
        for k in range(2):
            pltpu.make_async_copy(ys_ref.at[whole, :], slots[slot][k].at[whole, :], sem.at[slot, k]).wait()

    def combine(slot, off):
        y1_s, y2_s = slots[slot]
        rows = pl.ds(off, sub)
        ple = (jnp.dot(p_ref[rows, :].astype(BF16), wp_ref[...], preferred_element_type=F32)
               * gate_ref[rows, :].astype(F32))
        g1 = r_ref[rows, RL_G1:RL_G1 + 1]
        g2 = r_ref[rows, RL_G2:RL_G2 + 1]
        for s in range(TOK_ROWS):
            sl = slice(s * LANES, (s + 1) * LANES)
            f = (g1 * y1_s[pl.ds(s, sub, stride=GATHER_PITCH), :]
                 + g2 * y2_s[pl.ds(s, sub, stride=GATHER_PITCH), :])
            resid = x16_ref[pl.ds(off * TOK_ROWS + s, sub, stride=TOK_ROWS), :]
            h_s[:, sl] = ALPHA * resid + f + ple[:, sl]
        xn = _layer_norm(h_s[...], g_ref[...], b_ref[...])
        of_ref[rows, :] = xn
        ob_ref[rows, :] = xn.astype(BF16)

    @pl.when(i == 0)
    def _():
        for k in range(lead):
            issue(k, this_step, k * sub)

    for k in range(n_sub):
        wait(k)
        ahead = k + lead
        if ahead < n_sub:
            issue(ahead, this_step, ahead * sub)
        else:
            issue(ahead - n_sub, next_step, (ahead - n_sub) * sub)
        combine(k, k * sub)

    @pl.when(i == pl.num_programs(0) - 1)
    def _():
        for k in range(lead):
            wait(k)


COMBINE_SUB = 128
COMBINE_SLOTS = 4
COMBINE_LEAD = 2


def _combine_ple_ln(gate, x16, route, e_start, meta, ys, p, w_ple, ln_g, ln_b, layer):
    t, d = gate.shape
    n_sub, lead = COMBINE_SLOTS, COMBINE_LEAD
    sub = min(COMBINE_SUB, t // n_sub)
    tm = n_sub * sub
    steps = t // tm
    ahead = lead * sub
    fields = (meta[RL_E1], meta[RL_E2], meta[RL_R1], meta[RL_R2])
    next_spec = pl.BlockSpec((ahead,), lambda m: (jnp.minimum(m + 1, steps - 1) * (tm // ahead),),
                             memory_space=pltpu.SMEM)
    return pl.pallas_call(
        functools.partial(_combine_kernel, sub=sub, n_sub=n_sub, lead=lead),
        grid=(steps,),
        in_specs=_token_meta_specs(tm) + [next_spec] * 4 + [
            _row_spec(tm, d), _row_spec(tm * TOK_ROWS, LANES), _row_spec(tm, LANES),
            pl.BlockSpec((None, tm, PLE_DIM), lambda m: (layer, m, 0)),
            _resident_spec((None, PLE_DIM, d), lambda m: (layer, 0, 0)),
            _layer_vec_spec(layer, d), _layer_vec_spec(layer, d),
            pl.BlockSpec(memory_space=pl.ANY)],
        out_specs=[_row_spec(tm, d), _row_spec(tm, d)],
        out_shape=[jax.ShapeDtypeStruct((t, d), F32), jax.ShapeDtypeStruct((t, d), BF16)],
        scratch_shapes=[*[pltpu.VMEM((sub * GATHER_PITCH, LANES), F32) for _ in range(2 * n_sub)],
                        pltpu.VMEM((sub, d), F32),
                        pltpu.SemaphoreType.DMA((n_sub, 2))],
        compiler_params=_cparams(("arbitrary",)),
        name="moe_combine_ple_ln",
    )(e_start, *fields, *fields, gate, x16, route, p, w_ple, ln_g, ln_b, ys)


def _rope_tables(seq_len):
    rows = seq_len // GRID_W
    row = jnp.repeat(jnp.arange(rows, dtype=F32), GRID_W)
    col = jnp.tile(jnp.arange(GRID_W, dtype=F32), rows)
    inv_freq = ROPE_THETA ** (-jnp.arange(ROPE_AXIS_PAIRS, dtype=F32) / ROPE_AXIS_PAIRS)
    ang = jnp.concatenate([row[:, None] * inv_freq, col[:, None] * inv_freq], axis=-1)
    cos, sin = jnp.cos(ang), jnp.sin(ang)
    return jnp.concatenate([cos, cos], axis=-1), jnp.concatenate([-sin, sin], axis=-1)


def kernel(x, p, ln0_g, ln0_b, w_in, q_norm_g, k_norm_g, ssm_a_re, ssm_a_im, ssm_log_dt, ssm_b_re, ssm_b_im, ssm_c_re, ssm_c_im, ssm_d, w_glu, b_glu, attn_out_g, ssm_out_g, w_out, ln1_g, ln1_b, router_group_w, router_group_b, router_expert_w, router_expert_b, w_gate_e, w_up_e, w_down_e, w_ple, w_ple_gate, b_ple_gate, ln2_g, ln2_b):
    bsz, seq_len, d = x.shape
    t = bsz * seq_len
    n_layers = w_in.shape[0]
    vec = lambda a: a.reshape(n_layers, 1, a.shape[-1])

    cos2, sin2 = _rope_tables(seq_len)
    w_in_b, w_glu_b, w_out_b = w_in.astype(BF16), w_glu.astype(BF16), w_out.astype(BF16)
    w_ple_b, w_pg_b = w_ple.astype(BF16), w_ple_gate.astype(BF16)
    pad = LANES - N_EXPERTS - N_EXPERT_GROUPS
    w_route = jnp.pad(jnp.concatenate([router_expert_w, router_group_w], axis=-1), ((0, 0), (0, 0), (0, pad)))
    w_route_hi = w_route.astype(BF16)
    w_route_lo = (w_route - w_route_hi.astype(F32)).astype(BF16)
    w_route = jnp.stack([w_route_hi, w_route_lo], axis=1)
    b_route = jnp.pad(jnp.concatenate([router_expert_b, router_group_b], axis=-1), ((0, 0), (0, pad)))
    s5_ops = jax.vmap(_s5_operands)(ssm_a_re, ssm_a_im, ssm_log_dt, ssm_b_re, ssm_b_im,
                                    ssm_c_re, ssm_c_im, ssm_d)
    p2 = p.reshape(n_layers, t, PLE_DIM)
    perm = _s5_perm()

    tm_e = min(MOE_TM, t)
    xs = jnp.zeros((_moe_n_tiles(t, tm_e) * tm_e * TOK_ROWS, LANES), F32)
    xf, xb = _ln0(x.reshape(t, d), ln0_g, ln0_b)
    for i in range(n_layers):
        q, k, v, u = _inproj(xb, w_in_b, vec(q_norm_g), vec(k_norm_g), cos2, sin2, i, seq_len)
        attn_n = _attention(q, k, v, vec(attn_out_g), i, bsz, seq_len)
        y = _s5_mixer(u, perm, *s5_ops, i, bsz)
        y_n = _glu(y, w_glu_b, vec(b_glu), vec(ssm_out_g), i)
        xb, x16, route, meta, counts = _outproj(attn_n, y_n, w_out_b, xf, vec(ln1_g), vec(ln1_b),
                                                w_route, vec(b_route), i)
        te, tr, e_start = _moe_tiles(counts[0, :N_EXPERTS].astype(jnp.int32), t, tm_e)
        xs, gate = _dispatch(x16, e_start, meta, xs, xb, w_pg_b, vec(b_ple_gate), i)
        ys = _experts(xs, te, tr, w_gate_e, w_up_e, w_down_e, i, tm_e)
        xf, xb = _combine_ple_ln(gate, x16, route, e_start, meta, ys, p2, w_ple_b,
                                 vec(ln2_g), vec(ln2_b), i)
    return xf.reshape(bsz, seq_len, d)
```

```python
import functools
import math

import jax
import jax.numpy as jnp
from jax import lax
from jax.experimental import pallas as pl
from jax.experimental.pallas import tpu as pltpu

F32 = jnp.float32
BF16 = jnp.bfloat16

D_MODEL = 2048
DEPTH = 4
GRID_W = 64
N_HEADS = 8
N_KV_HEADS = 2
GQA = N_HEADS // N_KV_HEADS
HEAD_DIM = 128
ATTN_WIDTH = N_HEADS * HEAD_DIM
KV_WIDTH = N_KV_HEADS * HEAD_DIM
QK_WIDTH = ATTN_WIDTH + KV_WIDTH
ROPE_THETA = 10000.0
ROPE_AXIS_PAIRS = HEAD_DIM // 4
SSM_WIDTH = D_MODEL // 2
SSM_CH = 16
SSM_GROUPS = SSM_WIDTH // SSM_CH
SSM_STATE = 64
D_MIX = ATTN_WIDTH + SSM_WIDTH
D_IN_PROJ = ATTN_WIDTH + 2 * KV_WIDTH + SSM_WIDTH
N_EXPERT_GROUPS = 4
EXPERTS_PER_GROUP = 4
N_EXPERTS = N_EXPERT_GROUPS * EXPERTS_PER_GROUP
D_EXPERT = 512
PLE_DIM = 256
ALPHA = (2 * DEPTH) ** 0.25
EPS = 1e-6

LANES = 128
SUBLANES = 8
VMEM_LIMIT = 56 * 1024 * 1024

SSM_CHUNK = 16
SSM_CW = SSM_CHUNK * SSM_CH
SSM_GB = 8


def _cparams(sem):
    return pltpu.CompilerParams(dimension_semantics=sem, vmem_limit_bytes=VMEM_LIMIT)


def _s5_discretize(a_re, a_im, log_dt, b_re, b_im):
    dt = jnp.exp(log_dt)[:, None]
    lam_re, lam_im = dt * a_re, dt * a_im
    mag = jnp.exp(lam_re)
    abar_re, abar_im = mag * jnp.cos(lam_im), mag * jnp.sin(lam_im)
    den = a_re * a_re + a_im * a_im
    nr, ni = abar_re - 1.0, abar_im
    coef_re = (nr * a_re + ni * a_im) / den
    coef_im = (ni * a_re - nr * a_im) / den
    bb_re = coef_re[..., None] * b_re - coef_im[..., None] * b_im
    bb_im = coef_re[..., None] * b_im + coef_im[..., None] * b_re
    return lam_re, lam_im, bb_re, bb_im


def _cpow(lam_re, lam_im, k):
    mag = jnp.exp(lam_re * k)
    return mag * jnp.cos(lam_im * k), mag * jnp.sin(lam_im * k)


def _s5_operands(a_re, a_im, log_dt, b_re, b_im, c_re, c_im, d_skip):
    hp = lax.Precision.HIGHEST
    L, H, G = SSM_CHUNK, SSM_CH, SSM_GROUPS
    lam_f = _s5_discretize(a_re[0], a_im[0], log_dt[0], b_re[0], b_im[0])
    lam_b = _s5_discretize(a_re[1], a_im[1], log_dt[1], b_re[1], b_im[1])
    tau = jnp.arange(L, dtype=F32)

    def powers(lam, expo):
        return _cpow(lam[0][:, None, :], lam[1][:, None, :], expo[None, :, None])

    def conv_kernel(lam, cr, ci):
        pr, pi = powers(lam, tau)
        bbr, bbi = lam[2], lam[3]
        wr = pr[..., None] * bbr[:, None] - pi[..., None] * bbi[:, None]
        wi = pr[..., None] * bbi[:, None] + pi[..., None] * bbr[:, None]
        return (jnp.einsum('gop,glpi->gloi', cr, wr, precision=hp)
                - jnp.einsum('gop,glpi->gloi', ci, wi, precision=hp))

    k_f = conv_kernel(lam_f, c_re[0], c_im[0])
    k_b = conv_kernel(lam_b, c_re[1], c_im[1])
    lag0 = (tau == 0.0).astype(F32)[None, :, None, None]
    k_f = k_f + lag0 * (d_skip[:, None, :, None] * jnp.eye(H, dtype=F32))
    lag = jnp.arange(L)[None, :] - jnp.arange(L)[:, None]
    l_idx = jnp.arange(L)[None, None, :]
    sel = jnp.concatenate([(lag[..., None] == l_idx), (-lag[..., None] == l_idx)], axis=-1).astype(F32)
    wm = jnp.einsum('stl,gloi->gsito', sel, jnp.concatenate([k_f, k_b], axis=1), precision=hp)
    wm = wm.reshape(G, L * H, L * H)

    def end_map(lam, expo):
        pr, pi = powers(lam, expo)
        bbr, bbi = lam[2].transpose(0, 2, 1)[:, None], lam[3].transpose(0, 2, 1)[:, None]
        pr, pi = pr[:, :, None, :], pi[:, :, None, :]
        return pr * bbr - pi * bbi, pr * bbi + pi * bbr

    ef_re, ef_im = end_map(lam_f, (L - 1) - tau)
    eb_re, eb_im = end_map(lam_b, tau)
    we = jnp.concatenate([ef_re, eb_re, ef_im, eb_im], axis=-1).reshape(G, L * H, 4 * SSM_STATE)

    def out_map(lam, cr, ci, expo):
        pr, pi = powers(lam, expo)
        pr, pi = pr.transpose(0, 2, 1)[..., None], pi.transpose(0, 2, 1)[..., None]
        cr, ci = cr.transpose(0, 2, 1)[:, :, None, :], ci.transpose(0, 2, 1)[:, :, None, :]
        return cr * pr - ci * pi, -(cr * pi + ci * pr)

    of_re, of_im = out_map(lam_f, c_re[0], c_im[0], tau + 1.0)
    ob_re, ob_im = out_map(lam_b, c_re[1], c_im[1], L - tau)
    w2 = jnp.concatenate([of_re, ob_re, of_im, ob_im], axis=1).reshape(G, 4 * SSM_STATE, L * H)

    afr, afi = _cpow(lam_f[0], lam_f[1], float(L))
    abr, abi = _cpow(lam_b[0], lam_b[1], float(L))
    ar = jnp.concatenate([afr, abr], axis=-1)
    ai = jnp.concatenate([afi, abi], axis=-1)
    return wm.astype(BF16), we.astype(BF16), w2.astype(BF16), ar, ai


_S5_ROWS = 264


def _gelu_tanh(y):
    return 0.5 * y * (1.0 + jnp.tanh(math.sqrt(2.0 / math.pi) * (y + 0.044715 * (y * y * y))))


def _s5_perm():
    n = SSM_GB * SSM_CW
    r = lax.broadcasted_iota(jnp.int32, (n, n), 0)
    c = lax.broadcasted_iota(jnp.int32, (n, n), 1)
    r_step, r_grp, r_ch = r // LANES, (r % LANES) // SSM_CH, r % SSM_CH
    c_grp, c_step, c_ch = c // SSM_CW, (c % SSM_CW) // SSM_CH, c % SSM_CH
    return ((r_step == c_step) & (r_grp == c_grp) & (r_ch == c_ch)).astype(BF16)


def _s5_kernel(u_ref, perm_ref, wm_ref, we_ref, w2_ref, ar_ref, ai_ref, y_ref,
               er_s, ei_s, fr_s, fi_s, br_s, bi_s, yi_s, yg_s, *, n_chunks):
    nc = n_chunks
    zero_tile = jnp.zeros((SUBLANES, LANES), F32)
    ug = jnp.dot(u_ref[...], perm_ref[...], preferred_element_type=F32).astype(BF16)
    for g in range(SSM_GB):
        base = g * _S5_ROWS
        u_g = ug[:, g * SSM_CW:(g + 1) * SSM_CW]
        yi_s[g] = jnp.dot(u_g, wm_ref[g], preferred_element_type=F32)
        ends = jnp.dot(u_g, we_ref[g], preferred_element_type=F32)
        er_s[pl.ds(base, nc), :] = ends[:, :LANES]
        ei_s[pl.ds(base, nc), :] = ends[:, LANES:]
        fr_s[pl.ds(base, SUBLANES), :] = zero_tile
        fi_s[pl.ds(base, SUBLANES), :] = zero_tile
        br_s[pl.ds(base + nc, SUBLANES), :] = zero_tile
        bi_s[pl.ds(base + nc, SUBLANES), :] = zero_tile

    ar = ar_ref[...]
    ai = ai_ref[...]
    fwd_lane = lax.broadcasted_iota(jnp.int32, (SUBLANES, LANES), 1) < SSM_STATE

    def step(i, carry):
        xr, xi = carry
        rows_f = pl.ds(i, SUBLANES, stride=_S5_ROWS)
        rows_b = pl.ds(nc - 1 - i, SUBLANES, stride=_S5_ROWS)
        er = jnp.where(fwd_lane, er_s[rows_f, :], er_s[rows_b, :])
        ei = jnp.where(fwd_lane, ei_s[rows_f, :], ei_s[rows_b, :])
        nr = ar * xr - ai * xi + er
        ni = ar * xi + ai * xr + ei
        to_f = pl.ds(i + 1, SUBLANES, stride=_S5_ROWS)
        to_b = pl.ds(nc + 6 - i, SUBLANES, stride=_S5_ROWS)
        fr_s[to_f, :] = nr
        fi_s[to_f, :] = ni
        br_s[to_b, :] = nr
        bi_s[to_b, :] = ni
        return nr, ni

    lax.fori_loop(0, nc, step, (zero_tile, zero_tile), unroll=4)

    fwd_col = lax.broadcasted_iota(jnp.int32, (nc, LANES), 1) < SSM_STATE
    for g in range(SSM_GB):
        base = g * _S5_ROWS
        cr = jnp.where(fwd_col, fr_s[pl.ds(base, nc), :], br_s[pl.ds(base + SUBLANES, nc), :])
        ci = jnp.where(fwd_col, fi_s[pl.ds(base, nc), :], bi_s[pl.ds(base + SUBLANES, nc), :])
        y = (yi_s[g]
             + jnp.dot(cr.astype(BF16), w2_ref[g, :LANES, :], preferred_element_type=F32)
             + jnp.dot(ci.astype(BF16), w2_ref[g, LANES:, :], preferred_element_type=F32))
        yg_s[:, g * SSM_CW:(g + 1) * SSM_CW] = _gelu_tanh(y).astype(BF16)

    z = lax.dot_general(yg_s[...], perm_ref[...], (((1,), (1,)), ((), ())), preferred_element_type=F32)
    for j in range(SSM_CHUNK):
        y_ref[pl.ds(j, nc, stride=SSM_CHUNK), :] = z[:, j * LANES:(j + 1) * LANES]


def _s5_mixer(u3, perm, wm, we, w2, ar, ai, layer, bsz):
    nblk, rows, width = u3.shape
    nc = rows // bsz
    gb = SSM_GB
    cw = SSM_CW
    kern = functools.partial(_s5_kernel, n_chunks=nc)
    return pl.pallas_call(
        kern,
        grid=(bsz, nblk),
        in_specs=[
            pl.BlockSpec((None, nc, width), lambda b, j: (j, b, 0)),
            pl.BlockSpec((width, width), lambda b, j: (0, 0)),
            pl.BlockSpec((None, gb, cw, cw), lambda b, j: (layer, j, 0, 0)),
            pl.BlockSpec((None, gb, cw, cw), lambda b, j: (layer, j, 0, 0)),
            pl.BlockSpec((None, gb, cw, cw), lambda b, j: (layer, j, 0, 0)),
            pl.BlockSpec((None, gb, LANES), lambda b, j: (layer, j, 0)),
            pl.BlockSpec((None, gb, LANES), lambda b, j: (layer, j, 0)),
        ],
        out_specs=pl.BlockSpec((nc * SSM_CHUNK, LANES), lambda b, j: (b, j)),
        out_shape=jax.ShapeDtypeStruct((bsz * nc * SSM_CHUNK, nblk * LANES), F32),
        scratch_shapes=[
            *[pltpu.VMEM((gb * _S5_ROWS, LANES), F32) for _ in range(6)],
            pltpu.VMEM((gb, nc, cw), F32),
            pltpu.VMEM((nc, width), BF16),
        ],
        compiler_params=_cparams(("parallel", "parallel")),
        name="s5_mixer",
    )(u3, perm, wm, we, w2, ar, ai)


def _layer_norm(h, g, b):
    mu = jnp.mean(h, axis=-1, keepdims=True)
    c = h - mu
    var = jnp.mean(c * c, axis=-1, keepdims=True)
    return c * lax.rsqrt(var + EPS) * g + b


def _ln0_kernel(x_ref, g_ref, b_ref, xf_ref, xb_ref):
    y = _layer_norm(x_ref[...], g_ref[...], b_ref[...])
    xf_ref[...] = y
    xb_ref[...] = y.astype(BF16)


def _row_spec(tm, width):
    return pl.BlockSpec((tm, width), lambda m: (m, 0))


def _resident_spec(block_shape, index_map):
    return pl.BlockSpec(block_shape, index_map, pipeline_mode=pl.Buffered(1))


def _layer_vec_spec(layer, width):
    return pl.BlockSpec((None, 1, width), lambda m: (layer, 0, 0))


def _ln0(x, g, b, tm=256):
    t, d = x.shape
    tm = min(tm, t)
    return pl.pallas_call(
        _ln0_kernel,
        grid=(t // tm,),
        in_specs=[_row_spec(tm, d),
                  pl.BlockSpec((1, d), lambda m: (0, 0)),
                  pl.BlockSpec((1, d), lambda m: (0, 0))],
        out_specs=[_row_spec(tm, d), _row_spec(tm, d)],
        out_shape=[jax.ShapeDtypeStruct((t, d), F32), jax.ShapeDtypeStruct((t, d), BF16)],
        compiler_params=_cparams(("parallel",)),
        name="ln0",
    )(x, g.reshape(1, d), b.reshape(1, d))


def _inproj_kernel(x_ref, w_ref, qg_ref, kg_ref, cos_ref, sin_ref, q_ref, k_ref, v_ref, u_ref, u_s):
    x = x_ref[...]
    cos2 = cos_ref[...]
    sin2 = sin_ref[...]
    scale = HEAD_DIM ** -0.5

    def head(z, gain, mult):
        zn = z * lax.rsqrt(jnp.mean(z * z, axis=-1, keepdims=True) + EPS) * gain
        rot = pltpu.roll(zn, HEAD_DIM // 2, axis=1)
        out = zn * cos2 + rot * sin2
        return out * mult if mult != 1.0 else out

    pair = 2 * HEAD_DIM
    for c in range(D_IN_PROJ // pair):
        acc = jnp.dot(x, w_ref[:, c * pair:(c + 1) * pair], preferred_element_type=F32)
        col = c * pair
        if col < ATTN_WIDTH:
            for h in range(2):
                z = head(acc[:, h * HEAD_DIM:(h + 1) * HEAD_DIM], qg_ref[...], scale)
                q_ref[:, col + h * HEAD_DIM: col + (h + 1) * HEAD_DIM] = z.astype(BF16)
        elif col < QK_WIDTH:
            for h in range(2):
                z = head(acc[:, h * HEAD_DIM:(h + 1) * HEAD_DIM], kg_ref[...], 1.0)
                k_ref[:, h * HEAD_DIM:(h + 1) * HEAD_DIM] = z.astype(BF16)
        elif col < QK_WIDTH + KV_WIDTH:
            v_ref[...] = acc.astype(BF16)
        else:
            tm = acc.shape[0]
            for half in range(2):
                blk = (col - (QK_WIDTH + KV_WIDTH)) // LANES + half
                u_s[...] = acc[:, half * LANES:(half + 1) * LANES]
                for j in range(SSM_CHUNK):
                    step_rows = u_s[pl.ds(j, tm // SSM_CHUNK, stride=SSM_CHUNK), :]
                    u_ref[blk, :, j * LANES:(j + 1) * LANES] = step_rows.astype(BF16)


def _inproj(xb, w_in, q_gain, k_gain, cos2, sin2, layer, seq_len, tm=512):
    t, d = xb.shape
    tm = min(tm, seq_len)
    sblocks = seq_len // tm
    nblk = SSM_GROUPS // SSM_GB
    return pl.pallas_call(
        _inproj_kernel,
        grid=(t // tm,),
        in_specs=[_row_spec(tm, d),
                  _resident_spec((None, d, D_IN_PROJ), lambda m: (layer, 0, 0)),
                  _layer_vec_spec(layer, HEAD_DIM),
                  _layer_vec_spec(layer, HEAD_DIM),
                  pl.BlockSpec((tm, HEAD_DIM), lambda m: (m % sblocks, 0)),
                  pl.BlockSpec((tm, HEAD_DIM), lambda m: (m % sblocks, 0))],
        out_specs=[_row_spec(tm, ATTN_WIDTH), _row_spec(tm, KV_WIDTH), _row_spec(tm, KV_WIDTH),
                   pl.BlockSpec((nblk, tm // SSM_CHUNK, SSM_GB * SSM_CW), lambda m: (0, m, 0))],
        out_shape=[jax.ShapeDtypeStruct((t, ATTN_WIDTH), BF16),
                   jax.ShapeDtypeStruct((t, KV_WIDTH), BF16),
                   jax.ShapeDtypeStruct((t, KV_WIDTH), BF16),
                   jax.ShapeDtypeStruct((nblk, t // SSM_CHUNK, SSM_GB * SSM_CW), BF16)],
        scratch_shapes=[pltpu.VMEM((tm, LANES), F32)],
        compiler_params=_cparams(("parallel",)),
        name="inproj",
    )(xb, w_in, q_gain, k_gain, cos2, sin2)


def _attn_kernel(q_ref, k_ref, v_ref, g_ref, o_ref):
    outs = []
    ssq = None
    for h in range(N_HEADS):
        kv = h // GQA
        q = q_ref[:, h * HEAD_DIM:(h + 1) * HEAD_DIM]
        k = k_ref[:, kv * HEAD_DIM:(kv + 1) * HEAD_DIM]
        v = v_ref[:, kv * HEAD_DIM:(kv + 1) * HEAD_DIM]
        s = lax.dot_general(q, k, (((1,), (1,)), ((), ())), preferred_element_type=F32)
        m = jnp.max(s, axis=-1, keepdims=True)
        p = jnp.exp(s - m)
        l = jnp.sum(p, axis=-1, keepdims=True)
        o = jnp.dot(p.astype(BF16), v, preferred_element_type=F32) / l
        outs.append(o)
        sq = jnp.sum(o * o, axis=-1, keepdims=True)
        ssq = sq if ssq is None else ssq + sq
    r = lax.rsqrt(ssq * (1.0 / ATTN_WIDTH) + EPS)
    for h in range(N_HEADS):
        sl = slice(h * HEAD_DIM, (h + 1) * HEAD_DIM)
        o_ref[:, sl] = (outs[h] * r * g_ref[:, sl]).astype(BF16)


def _attention(q, k, v, gain, layer, bsz, seq_len, tq=512):
    t = q.shape[0]
    tq = min(tq, seq_len)
    qblocks = seq_len // tq
    return pl.pallas_call(
        _attn_kernel,
        grid=(bsz, qblocks),
        in_specs=[pl.BlockSpec((tq, ATTN_WIDTH), lambda b, i: (b * qblocks + i, 0)),
                  pl.BlockSpec((seq_len, KV_WIDTH), lambda b, i: (b, 0)),
                  pl.BlockSpec((seq_len, KV_WIDTH), lambda b, i: (b, 0)),
                  pl.BlockSpec((None, 1, ATTN_WIDTH), lambda b, i: (layer, 0, 0))],
        out_specs=pl.BlockSpec((tq, ATTN_WIDTH), lambda b, i: (b * qblocks + i, 0)),
        out_shape=jax.ShapeDtypeStruct((t, ATTN_WIDTH), BF16),
        compiler_params=_cparams(("parallel", "parallel")),
        name="attention",
    )(q, k, v, gain)


def _glu_kernel(y_ref, w_ref, b_ref, g_ref, o_ref):
    y = y_ref[...]
    gate = jnp.dot(y.astype(BF16), w_ref[...], preferred_element_type=F32) + b_ref[...]
    z = y * jax.nn.sigmoid(gate)
    zn = z * lax.rsqrt(jnp.mean(z * z, axis=-1, keepdims=True) + EPS) * g_ref[...]
    o_ref[...] = zn.astype(BF16)


def _glu(y, w_glu, b_glu, gain, layer, tm=512):
    t, w = y.shape
    tm = min(tm, t)
    return pl.pallas_call(
        _glu_kernel,
        grid=(t // tm,),
        in_specs=[_row_spec(tm, w),
                  pl.BlockSpec((None, w, w), lambda m: (layer, 0, 0)),
                  _layer_vec_spec(layer, w),
                  _layer_vec_spec(layer, w)],
        out_specs=_row_spec(tm, w),
        out_shape=jax.ShapeDtypeStruct((t, w), BF16),
        compiler_params=_cparams(("parallel",)),
        name="glu",
    )(y, w_glu, b_glu, gain)


ROUTE_LANE_GROUP = N_EXPERTS


def _route(logits):
    lane = lax.broadcasted_iota(jnp.int32, logits.shape, 1)
    neg = jnp.float32(-1e30)
    big = jnp.int32(LANES)

    def masked_softmax(mask):
        z = jnp.where(mask, logits, neg)
        zmax = jnp.max(z, axis=-1, keepdims=True)
        e = jnp.where(mask, jnp.exp(z - zmax), 0.0)
        return e / jnp.sum(e, axis=-1, keepdims=True)

    def first_argmax(vals, mask):
        top = jnp.max(jnp.where(mask, vals, -1.0), axis=-1, keepdims=True)
        idx = jnp.min(jnp.where(mask & (vals == top), lane, big), axis=-1, keepdims=True)
        return top, idx

    gmask = (lane >= ROUTE_LANE_GROUP) & (lane < ROUTE_LANE_GROUP + N_EXPERT_GROUPS)
    g_top, g_lane = first_argmax(masked_softmax(gmask), gmask)
    e_lo = (g_lane - ROUTE_LANE_GROUP) * EXPERTS_PER_GROUP
    emask = (lane >= e_lo) & (lane < e_lo + EXPERTS_PER_GROUP)
    e_prob = masked_softmax(emask)
    v1, i1 = first_argmax(e_prob, emask)
    v2, i2 = first_argmax(e_prob, emask & (lane != i1))
    denom = v1 + v2
    return i1, i2, g_top * (v1 / denom), g_top * (v2 / denom)


TOK_ROWS = D_MODEL // LANES
MOE_TM = 256
GATHER_PITCH = 24
RL_E1, RL_E2, RL_G1, RL_G2, RL_R1, RL_R2 = 0, 1, 2, 3, 4, 5


def _to_token_major(ref, x):
    tm = x.shape[0]
    for s in range(TOK_ROWS):
        ref[pl.ds(s, tm, stride=TOK_ROWS), :] = x[:, s * LANES:(s + 1) * LANES]


def _from_token_major(ref, tm, s):
    return ref[pl.ds(s, tm, stride=TOK_ROWS), :]


def _outproj_kernel(a_ref, y_ref, wa_ref, wy_ref, x_ref, g_ref, b_ref, wr_ref, br_ref,
                    xb_ref, x16_ref, r_ref, meta_ref, cnt_ref, run_s):
    tm = a_ref.shape[0]

    @pl.when(pl.program_id(0) == 0)
    def _():
        run_s[...] = jnp.zeros_like(run_s)

    m = (jnp.dot(a_ref[...], wa_ref[...], preferred_element_type=F32)
         + jnp.dot(y_ref[...], wy_ref[...], preferred_element_type=F32))
    xn = _layer_norm(ALPHA * x_ref[...] + m, g_ref[...], b_ref[...])
    xn_hi = xn.astype(BF16)
    xb_ref[...] = xn_hi
    _to_token_major(x16_ref, xn)
    xn_lo = (xn - xn_hi.astype(F32)).astype(BF16)
    logits = (jnp.dot(xn_hi, wr_ref[0], preferred_element_type=F32)
              + jnp.dot(xn_hi, wr_ref[1], preferred_element_type=F32)
              + jnp.dot(xn_lo, wr_ref[0], preferred_element_type=F32)) + br_ref[...]
    i1, i2, g1, g2 = _route(logits)

    lane = lax.broadcasted_iota(jnp.int32, (tm, LANES), 1)
    hit1, hit2 = lane == i1, lane == i2
    onehot = (hit1 | hit2).astype(BF16)
    earlier = (lax.broadcasted_iota(jnp.int32, (tm, tm), 0)
               > lax.broadcasted_iota(jnp.int32, (tm, tm), 1)).astype(BF16)
    rank = jnp.dot(earlier, onehot, preferred_element_type=F32) + run_s[0:1, :]
    r1 = jnp.sum(jnp.where(hit1, rank, 0.0), axis=-1, keepdims=True)
    r2 = jnp.sum(jnp.where(hit2, rank, 0.0), axis=-1, keepdims=True)
    total = run_s[0:1, :] + jnp.sum(onehot.astype(F32), axis=0, keepdims=True)
    run_s[...] = jnp.broadcast_to(total, run_s.shape)
    cnt_ref[...] = jnp.broadcast_to(total, cnt_ref.shape)
    out = jnp.zeros((tm, LANES), F32)
    for col, val in ((RL_E1, i1.astype(F32)), (RL_E2, i2.astype(F32)), (RL_G1, g1), (RL_G2, g2),
                     (RL_R1, r1), (RL_R2, r2)):
        out = jnp.where(lane == col, val, out)
    r_ref[...] = out
    meta_ref[...] = out.T[:SUBLANES, :].astype(jnp.int32)


def _outproj(attn_n, y_n, w_out, xf, ln_g, ln_b, w_route, b_route, layer, tm=512):
    t, d = xf.shape
    tm = min(tm, t)
    half = D_MIX // 2
    return pl.pallas_call(
        _outproj_kernel,
        grid=(t // tm,),
        in_specs=[_row_spec(tm, half), _row_spec(tm, half),
                  _resident_spec((None, half, d), lambda m: (layer, 0, 0)),
                  _resident_spec((None, half, d), lambda m: (layer, 1, 0)),
                  _row_spec(tm, d),
                  _layer_vec_spec(layer, d), _layer_vec_spec(layer, d),
                  _resident_spec((None, 2, d, LANES), lambda m: (layer, 0, 0, 0)),
                  _layer_vec_spec(layer, LANES)],
        out_specs=[_row_spec(tm, d), _row_spec(tm * TOK_ROWS, LANES), _row_spec(tm, LANES),
                   pl.BlockSpec((SUBLANES, tm), lambda m: (0, m)),
                   pl.BlockSpec((SUBLANES, LANES), lambda m: (0, 0))],
        out_shape=[jax.ShapeDtypeStruct((t, d), BF16),
                   jax.ShapeDtypeStruct((t * TOK_ROWS, LANES), F32),
                   jax.ShapeDtypeStruct((t, LANES), F32),
                   jax.ShapeDtypeStruct((SUBLANES, t), jnp.int32),
                   jax.ShapeDtypeStruct((SUBLANES, LANES), F32)],
        scratch_shapes=[pltpu.VMEM((SUBLANES, LANES), F32)],
        compiler_params=_cparams(("arbitrary",)),
        name="outproj_ln_route",
    )(attn_n, y_n, w_out, w_out, xf, ln_g, ln_b, w_route, b_route)


def _token_copy(src, src_tok, dst, dst_tok, sem, dst_pitch=TOK_ROWS):
    return pltpu.make_async_copy(
        src.at[pl.ds(pl.multiple_of(src_tok * TOK_ROWS, TOK_ROWS), TOK_ROWS), :],
        dst.at[pl.ds(pl.multiple_of(dst_tok * dst_pitch, SUBLANES), TOK_ROWS), :],
        sem)


def _sorted_row(es_ref, e_ref, r_ref, t):
    return es_ref[e_ref[t]] + r_ref[t]


def _dispatch_kernel(es_ref, e1_ref, e2_ref, r1_ref, r2_ref, x16_ref, xb_ref, wg_ref, bg_ref, xs_in_ref,
                     xs_ref, gate_ref, sem, *, tm):
    del xs_in_ref
    for t in range(tm):
        _token_copy(x16_ref, t, xs_ref, _sorted_row(es_ref, e1_ref, r1_ref, t), sem.at[0]).start()
        _token_copy(x16_ref, t, xs_ref, _sorted_row(es_ref, e2_ref, r2_ref, t), sem.at[1]).start()
    gate = jnp.dot(xb_ref[...], wg_ref[...], preferred_element_type=F32) + bg_ref[...]
    gate_ref[...] = jax.nn.sigmoid(gate).astype(BF16)
    whole = pl.ds(0, tm * TOK_ROWS)
    pltpu.make_async_copy(x16_ref, xs_ref.at[whole, :], sem.at[0]).wait()
    pltpu.make_async_copy(x16_ref, xs_ref.at[whole, :], sem.at[1]).wait()


def _smem_tile_spec(tm):
    return pl.BlockSpec((tm,), lambda m: (m,), memory_space=pltpu.SMEM)


def _token_meta_specs(tm):
    return [pl.BlockSpec(memory_space=pltpu.SMEM)] + [_smem_tile_spec(tm)] * 4


def _dispatch(x16, e_start, meta, xs_buf, xb, w_pg, b_pg, layer, tm=512):
    t, d = xb.shape
    tm = min(tm, t)
    return pl.pallas_call(
        functools.partial(_dispatch_kernel, tm=tm),
        grid=(t // tm,),
        in_specs=_token_meta_specs(tm) + [_row_spec(tm * TOK_ROWS, LANES),
                                          _row_spec(tm, d),
                                          _resident_spec((None, d, d), lambda m: (layer, 0, 0)),
                                          _layer_vec_spec(layer, d),
                                          pl.BlockSpec(memory_space=pl.ANY)],
        out_specs=[pl.BlockSpec(memory_space=pl.ANY), _row_spec(tm, d)],
        out_shape=[jax.ShapeDtypeStruct(xs_buf.shape, F32), jax.ShapeDtypeStruct((t, d), BF16)],
        input_output_aliases={9: 0},
        scratch_shapes=[pltpu.SemaphoreType.DMA((2,))],
        compiler_params=_cparams(("arbitrary",)),
        name="moe_dispatch_gate",
    )(e_start, meta[RL_E1], meta[RL_E2], meta[RL_R1], meta[RL_R2], x16, xb, w_pg, b_pg, xs_buf)


def _expert_kernel(te_ref, tr_ref, xs_ref, wg_ref, wu_ref, wd_ref, ys_ref, lhs_s, wg_s, wu_s, wd_s, *, tm):
    j = pl.program_id(0)
    rows = tr_ref[j]

    @pl.when(rows == 0)
    def _():
        ys_ref[...] = jnp.zeros_like(ys_ref)

    @pl.when((rows > 0) & ((j == 0) | (te_ref[j] != te_ref[jnp.maximum(j - 1, 0)])))
    def _():
        wg_s[...] = wg_ref[...].astype(BF16)
        wu_s[...] = wu_ref[...].astype(BF16)
        wd_s[...] = wd_ref[...].astype(BF16)

    @pl.when(rows > 0)
    def _():
        valid = lax.broadcasted_iota(jnp.int32, (tm, LANES), 0) < rows
        for s in range(TOK_ROWS):
            blk = jnp.where(valid, _from_token_major(xs_ref, tm, s), 0.0)
            lhs_s[:, s * LANES:(s + 1) * LANES] = blk.astype(BF16)
        x = lhs_s[...]
        hg = jnp.dot(x, wg_s[...], preferred_element_type=F32)
        hu = jnp.dot(x, wu_s[...], preferred_element_type=F32)
        hid = ((hg * jax.nn.sigmoid(hg)) * hu).astype(BF16)
        for c in range(D_MODEL // D_EXPERT):
            out = jnp.dot(hid, wd_s[:, c * D_EXPERT:(c + 1) * D_EXPERT], preferred_element_type=F32)
            for h in range(D_EXPERT // LANES):
                s = c * (D_EXPERT // LANES) + h
                ys_ref[pl.ds(s, tm, stride=TOK_ROWS), :] = out[:, h * LANES:(h + 1) * LANES]


def _moe_n_tiles(t, tm):
    return (2 * t) // tm + N_EXPERTS


def _moe_tiles(counts, t, tm):
    n_tiles = _moe_n_tiles(t, tm)
    per_e = (counts + tm - 1) // tm
    ends = jnp.cumsum(per_e)
    starts = ends - per_e
    j = jnp.arange(n_tiles, dtype=jnp.int32)[:, None]
    e = jnp.minimum(jnp.sum((j >= ends[None, :]).astype(jnp.int32), axis=1), N_EXPERTS - 1)
    mine = (j >= starts[None, :]) & (j < ends[None, :])
    rows = jnp.sum(jnp.where(mine, jnp.clip(counts[None, :] - (j - starts[None, :]) * tm, 0, tm), 0), axis=1)
    return e.astype(jnp.int32), rows.astype(jnp.int32), (starts * tm).astype(jnp.int32)


def _experts(xs, te, tr, w_gate, w_up, w_down, layer, tm):
    d = D_MODEL
    blk = pl.BlockSpec((tm * TOK_ROWS, LANES), lambda j, te, tr: (j, 0))
    grid_spec = pltpu.PrefetchScalarGridSpec(
        num_scalar_prefetch=2,
        grid=(te.shape[0],),
        in_specs=[blk,
                  pl.BlockSpec((None, None, d, D_EXPERT), lambda j, te, tr: (layer, te[j], 0, 0)),
                  pl.BlockSpec((None, None, d, D_EXPERT), lambda j, te, tr: (layer, te[j], 0, 0)),
                  pl.BlockSpec((None, None, D_EXPERT, d), lambda j, te, tr: (layer, te[j], 0, 0))],
        out_specs=blk,
        scratch_shapes=[pltpu.VMEM((tm, d), BF16),
                        pltpu.VMEM((d, D_EXPERT), BF16),
                        pltpu.VMEM((d, D_EXPERT), BF16),
                        pltpu.VMEM((D_EXPERT, d), BF16)],
    )
    return pl.pallas_call(
        functools.partial(_expert_kernel, tm=tm),
        grid_spec=grid_spec,
        out_shape=jax.ShapeDtypeStruct(xs.shape, F32),
        compiler_params=_cparams(("arbitrary",)),
        name="moe_experts",
    )(te, tr, xs, w_gate, w_up, w_down)


def _combine_kernel(es_ref, e1_ref, e2_ref, r1_ref, r2_ref, e1n_ref, e2n_ref, r1n_ref, r2n_ref,
                    gate_ref, x16_ref, r_ref, p_ref, wp_ref, g_ref, b_ref, ys_ref, of_ref, ob_ref,
                    *scratch, sub, n_sub, lead):
    slots = [(scratch[2 * k], scratch[2 * k + 1]) for k in range(n_sub)]
    h_s, sem = scratch[2 * n_sub], scratch[2 * n_sub + 1]
    i = pl.program_id(0)
    this_step = (e1_ref, e2_ref, r1_ref, r2_ref)
    next_step = (e1n_ref, e2n_ref, r1n_ref, r2n_ref)

    def issue(slot, fields, off):
        e1, e2, r1, r2 = fields
        y1_s, y2_s = slots[slot]
        for t in range(sub):
            _token_copy(ys_ref, _sorted_row(es_ref, e1, r1, off + t), y1_s, t, sem.at[slot, 0],
                        GATHER_PITCH).start()
            _token_copy(ys_ref, _sorted_row(es_ref, e2, r2, off + t), y2_s, t, sem.at[slot, 1],
                        GATHER_PITCH).start()

    def wait(slot):
        whole = pl.ds(0, sub * TOK_ROWS)
        for k in range(2):
            pltpu.make_async_copy(ys_ref.at[whole, :], slots[slot][k].at[whole, :], sem.at[slot, k]).wait()

    def combine(slot, off):
        y1_s, y2_s = slots[slot]
        rows = pl.ds(off, sub)
        ple = (jnp.dot(p_ref[rows, :].astype(BF16), wp_ref[...], preferred_element_type=F32)
               * gate_ref[rows, :].astype(F32))
        g1 = r_ref[rows, RL_G1:RL_G1 + 1]
        g2 = r_ref[rows, RL_G2:RL_G2 + 1]
        for s in range(TOK_ROWS):
            sl = slice(s * LANES, (s + 1) * LANES)
            f = (g1 * y1_s[pl.ds(s, sub, stride=GATHER_PITCH), :]
                 + g2 * y2_s[pl.ds(s, sub, stride=GATHER_PITCH), :])
            resid = x16_ref[pl.ds(off * TOK_ROWS + s, sub, stride=TOK_ROWS), :]
            h_s[:, sl] = ALPHA * resid + f + ple[:, sl]
        xn = _layer_norm(h_s[...], g_ref[...], b_ref[...])
        of_ref[rows, :] = xn
        ob_ref[rows, :] = xn.astype(BF16)

    @pl.when(i == 0)
    def _():
        for k in range(lead):
            issue(k, this_step, k * sub)

    for k in range(n_sub):
        wait(k)
        ahead = k + lead
        if ahead < n_sub:
            issue(ahead, this_step, ahead * sub)
        else:
            issue(ahead - n_sub, next_step, (ahead - n_sub) * sub)
        combine(k, k * sub)

    @pl.when(i == pl.num_programs(0) - 1)
    def _():
        for k in range(lead):
            wait(k)


COMBINE_SUB = 128
COMBINE_SLOTS = 4
COMBINE_LEAD = 2


def _combine_ple_ln(gate, x16, route, e_start, meta, ys, p, w_ple, ln_g, ln_b, layer):
    t, d = gate.shape
    n_sub, lead = COMBINE_SLOTS, COMBINE_LEAD
    sub = min(COMBINE_SUB, t // n_sub)
    tm = n_sub * sub
    steps = t // tm
    ahead = lead * sub
    fields = (meta[RL_E1], meta[RL_E2], meta[RL_R1], meta[RL_R2])
    next_spec = pl.BlockSpec((ahead,), lambda m: (jnp.minimum(m + 1, steps - 1) * (tm // ahead),),
                             memory_space=pltpu.SMEM)
    return pl.pallas_call(
        functools.partial(_combine_kernel, sub=sub, n_sub=n_sub, lead=lead),
        grid=(steps,),
        in_specs=_token_meta_specs(tm) + [next_spec] * 4 + [
            _row_spec(tm, d), _row_spec(tm * TOK_ROWS, LANES), _row_spec(tm, LANES),
            pl.BlockSpec((None, tm, PLE_DIM), lambda m: (layer, m, 0)),
            _resident_spec((None, PLE_DIM, d), lambda m: (layer, 0, 0)),
            _layer_vec_spec(layer, d), _layer_vec_spec(layer, d),
            pl.BlockSpec(memory_space=pl.ANY)],
        out_specs=[_row_spec(tm, d), _row_spec(tm, d)],
        out_shape=[jax.ShapeDtypeStruct((t, d), F32), jax.ShapeDtypeStruct((t, d), BF16)],
        scratch_shapes=[*[pltpu.VMEM((sub * GATHER_PITCH, LANES), F32) for _ in range(2 * n_sub)],
                        pltpu.VMEM((sub, d), F32),
                        pltpu.SemaphoreType.DMA((n_sub, 2))],
        compiler_params=_cparams(("arbitrary",)),
        name="moe_combine_ple_ln",
    )(e_start, *fields, *fields, gate, x16, route, p, w_ple, ln_g, ln_b, ys)


def _rope_tables(seq_len):
    rows = seq_len // GRID_W
    row = jnp.repeat(jnp.arange(rows, dtype=F32), GRID_W)
    col = jnp.tile(jnp.arange(GRID_W, dtype=F32), rows)
    inv_freq = ROPE_THETA ** (-jnp.arange(ROPE_AXIS_PAIRS, dtype=F32) / ROPE_AXIS_PAIRS)
    ang = jnp.concatenate([row[:, None] * inv_freq, col[:, None] * inv_freq], axis=-1)
    cos, sin = jnp.cos(ang), jnp.sin(ang)
    return jnp.concatenate([cos, cos], axis=-1), jnp.concatenate([-sin, sin], axis=-1)


def kernel(x, p, ln0_g, ln0_b, w_in, q_norm_g, k_norm_g, ssm_a_re, ssm_a_im, ssm_log_dt, ssm_b_re, ssm_b_im, ssm_c_re, ssm_c_im, ssm_d, w_glu, b_glu, attn_out_g, ssm_out_g, w_out, ln1_g, ln1_b, router_group_w, router_group_b, router_expert_w, router_expert_b, w_gate_e, w_up_e, w_down_e, w_ple, w_ple_gate, b_ple_gate, ln2_g, ln2_b):
    bsz, seq_len, d = x.shape
    t = bsz * seq_len
    n_layers = w_in.shape[0]
    vec = lambda a: a.reshape(n_layers, 1, a.shape[-1])

    cos2, sin2 = _rope_tables(seq_len)
    w_in_b, w_glu_b, w_out_b = w_in.astype(BF16), w_glu.astype(BF16), w_out.astype(BF16)
    w_ple_b, w_pg_b = w_ple.astype(BF16), w_ple_gate.astype(BF16)
    pad = LANES - N_EXPERTS - N_EXPERT_GROUPS
    w_route = jnp.pad(jnp.concatenate([router_expert_w, router_group_w], axis=-1), ((0, 0), (0, 0), (0, pad)))
    w_route_hi = w_route.astype(BF16)
    w_route_lo = (w_route - w_route_hi.astype(F32)).astype(BF16)
    w_route = jnp.stack([w_route_hi, w_route_lo], axis=1)
    b_route = jnp.pad(jnp.concatenate([router_expert_b, router_group_b], axis=-1), ((0, 0), (0, pad)))
    s5_ops = jax.vmap(_s5_operands)(ssm_a_re, ssm_a_im, ssm_log_dt, ssm_b_re, ssm_b_im,
                                    ssm_c_re, ssm_c_im, ssm_d)
    p2 = p.reshape(n_layers, t, PLE_DIM)
    perm = _s5_perm()

    tm_e = min(MOE_TM, t)
    xs = jnp.zeros((_moe_n_tiles(t, tm_e) * tm_e * TOK_ROWS, LANES), F32)
    xf, xb = _ln0(x.reshape(t, d), ln0_g, ln0_b)
    for i in range(n_layers):
        q, k, v, u = _inproj(xb, w_in_b, vec(q_norm_g), vec(k_norm_g), cos2, sin2, i, seq_len)
        attn_n = _attention(q, k, v, vec(attn_out_g), i, bsz, seq_len)
        y = _s5_mixer(u, perm, *s5_ops, i, bsz)
        y_n = _glu(y, w_glu_b, vec(b_glu), vec(ssm_out_g), i)
        xb, x16, route, meta, counts = _outproj(attn_n, y_n, w_out_b, xf, vec(ln1_g), vec(ln1_b),
                                                w_route, vec(b_route), i)
        te, tr, e_start = _moe_tiles(counts[0, :N_EXPERTS].astype(jnp.int32), t, tm_e)
        xs, gate = _dispatch(x16, e_start, meta, xs, xb, w_pg_b, vec(b_ple_gate), i)
        ys = _experts(xs, te, tr, w_gate_e, w_up_e, w_down_e, i, tm_e)
        xf, xb = _combine_ple_ln(gate, x16, route, e_start, meta, ys, p2, w_ple_b,
                                 vec(ln2_g), vec(ln2_b), i)
    return xf.reshape(bsz, seq_len, d)
```

```python
import functools
import math

import jax
import jax.numpy as jnp
from jax import lax
from jax.experimental import pallas as pl
from jax.experimental.pallas import tpu as pltpu

F32 = jnp.float32
BF16 = jnp.bfloat16

D_MODEL = 2048
DEPTH = 4
GRID_W = 64
N_HEADS = 8
N_KV_HEADS = 2
GQA = N_HEADS // N_KV_HEADS
HEAD_DIM = 128
ATTN_WIDTH = N_HEADS * HEAD_DIM
KV_WIDTH = N_KV_HEADS * HEAD_DIM
QK_WIDTH = ATTN_WIDTH + KV_WIDTH
ROPE_THETA = 10000.0
ROPE_AXIS_PAIRS = HEAD_DIM // 4
SSM_WIDTH = D_MODEL // 2
SSM_CH = 16
SSM_GROUPS = SSM_WIDTH // SSM_CH
SSM_STATE = 64
D_MIX = ATTN_WIDTH + SSM_WIDTH
D_IN_PROJ = ATTN_WIDTH + 2 * KV_WIDTH + SSM_WIDTH
N_EXPERT_GROUPS = 4
EXPERTS_PER_GROUP = 4
N_EXPERTS = N_EXPERT_GROUPS * EXPERTS_PER_GROUP
D_EXPERT = 512
PLE_DIM = 256
ALPHA = (2 * DEPTH) ** 0.25
EPS = 1e-6

LANES = 128
SUBLANES = 8
VMEM_LIMIT = 56 * 1024 * 1024

SSM_CHUNK = 16
SSM_CW = SSM_CHUNK * SSM_CH
SSM_GB = 8


def _cparams(sem):
    return pltpu.CompilerParams(dimension_semantics=sem, vmem_limit_bytes=VMEM_LIMIT)


def _s5_discretize(a_re, a_im, log_dt, b_re, b_im):
    dt = jnp.exp(log_dt)[:, None]
    lam_re, lam_im = dt * a_re, dt * a_im
    mag = jnp.exp(lam_re)
    abar_re, abar_im = mag * jnp.cos(lam_im), mag * jnp.sin(lam_im)
    den = a_re * a_re + a_im * a_im
    nr, ni = abar_re - 1.0, abar_im
    coef_re = (nr * a_re + ni * a_im) / den
    coef_im = (ni * a_re - nr * a_im) / den
    bb_re = coef_re[..., None] * b_re - coef_im[..., None] * b_im
    bb_im = coef_re[..., None] * b_im + coef_im[..., None] * b_re
    return lam_re, lam_im, bb_re, bb_im


def _cpow(lam_re, lam_im, k):
    mag = jnp.exp(lam_re * k)
    return mag * jnp.cos(lam_im * k), mag * jnp.sin(lam_im * k)


def _s5_operands(a_re, a_im, log_dt, b_re, b_im, c_re, c_im, d_skip):
    hp = lax.Precision.HIGHEST
    L, H, G = SSM_CHUNK, SSM_CH, SSM_GROUPS
    lam_f = _s5_discretize(a_re[0], a_im[0], log_dt[0], b_re[0], b_im[0])
    lam_b = _s5_discretize(a_re[1], a_im[1], log_dt[1], b_re[1], b_im[1])
    tau = jnp.arange(L, dtype=F32)

    def powers(lam, expo):
        return _cpow(lam[0][:, None, :], lam[1][:, None, :], expo[None, :, None])

    def conv_kernel(lam, cr, ci):
        pr, pi = powers(lam, tau)
        bbr, bbi = lam[2], lam[3]
        wr = pr[..., None] * bbr[:, None] - pi[..., None] * bbi[:, None]
        wi = pr[..., None] * bbi[:, None] + pi[..., None] * bbr[:, None]
        return (jnp.einsum('gop,glpi->gloi', cr, wr, precision=hp)
                - jnp.einsum('gop,glpi->gloi', ci, wi, precision=hp))

    k_f = conv_kernel(lam_f, c_re[0], c_im[0])
    k_b = conv_kernel(lam_b, c_re[1], c_im[1])
    lag0 = (tau == 0.0).astype(F32)[None, :, None, None]
    k_f = k_f + lag0 * (d_skip[:, None, :, None] * jnp.eye(H, dtype=F32))
    lag = jnp.arange(L)[None, :] - jnp.arange(L)[:, None]
    l_idx = jnp.arange(L)[None, None, :]
    sel = jnp.concatenate([(lag[..., None] == l_idx), (-lag[..., None] == l_idx)], axis=-1).astype(F32)
    wm = jnp.einsum('stl,gloi->gsito', sel, jnp.concatenate([k_f, k_b], axis=1), precision=hp)
    wm = wm.reshape(G, L * H, L * H)

    def end_map(lam, expo):
        pr, pi = powers(lam, expo)
        bbr, bbi = lam[2].transpose(0, 2, 1)[:, None], lam[3].transpose(0, 2, 1)[:, None]
        pr, pi = pr[:, :, None, :], pi[:, :, None, :]
        return pr * bbr - pi * bbi, pr * bbi + pi * bbr

    ef_re, ef_im = end_map(lam_f, (L - 1) - tau)
    eb_re, eb_im = end_map(lam_b, tau)
    we = jnp.concatenate([ef_re, eb_re, ef_im, eb_im], axis=-1).reshape(G, L * H, 4 * SSM_STATE)

    def out_map(lam, cr, ci, expo):
        pr, pi = powers(lam, expo)
        pr, pi = pr.transpose(0, 2, 1)[..., None], pi.transpose(0, 2, 1)[..., None]
        cr, ci = cr.transpose(0, 2, 1)[:, :, None, :], ci.transpose(0, 2, 1)[:, :, None, :]
        return cr * pr - ci * pi, -(cr * pi + ci * pr)

    of_re, of_im = out_map(lam_f, c_re[0], c_im[0], tau + 1.0)
    ob_re, ob_im = out_map(lam_b, c_re[1], c_im[1], L - tau)
    w2 = jnp.concatenate([of_re, ob_re, of_im, ob_im], axis=1).reshape(G, 4 * SSM_STATE, L * H)

    afr, afi = _cpow(lam_f[0], lam_f[1], float(L))
    abr, abi = _cpow(lam_b[0], lam_b[1], float(L))
    ar = jnp.concatenate([afr, abr], axis=-1)
    ai = jnp.concatenate([afi, abi], axis=-1)
    return wm.astype(BF16), we.astype(BF16), w2.astype(BF16), ar, ai


_S5_ROWS = 264


def _gelu_tanh(y):
    return 0.5 * y * (1.0 + jnp.tanh(math.sqrt(2.0 / math.pi) * (y + 0.044715 * (y * y * y))))


def _s5_perm():
    n = SSM_GB * SSM_CW
    r = lax.broadcasted_iota(jnp.int32, (n, n), 0)
    c = lax.broadcasted_iota(jnp.int32, (n, n), 1)
    r_step, r_grp, r_ch = r // LANES, (r % LANES) // SSM_CH, r % SSM_CH
    c_grp, c_step, c_ch = c // SSM_CW, (c % SSM_CW) // SSM_CH, c % SSM_CH
    return ((r_step == c_step) & (r_grp == c_grp) & (r_ch == c_ch)).astype(BF16)


def _s5_kernel(u_ref, perm_ref, wm_ref, we_ref, w2_ref, ar_ref, ai_ref, y_ref,
               er_s, ei_s, fr_s, fi_s, br_s, bi_s, yi_s, yg_s, *, n_chunks):
    nc = n_chunks
    zero_tile = jnp.zeros((SUBLANES, LANES), F32)
    ug = jnp.dot(u_ref[...], perm_ref[...], preferred_element_type=F32).astype(BF16)
    for g in range(SSM_GB):
        base = g * _S5_ROWS
        u_g = ug[:, g * SSM_CW:(g + 1) * SSM_CW]
        yi_s[g] = jnp.dot(u_g, wm_ref[g], preferred_element_type=F32)
        ends = jnp.dot(u_g, we_ref[g], preferred_element_type=F32)
        er_s[pl.ds(base, nc), :] = ends[:, :LANES]
        ei_s[pl.ds(base, nc), :] = ends[:, LANES:]
        fr_s[pl.ds(base, SUBLANES), :] = zero_tile
        fi_s[pl.ds(base, SUBLANES), :] = zero_tile
        br_s[pl.ds(base + nc, SUBLANES), :] = zero_tile
        bi_s[pl.ds(base + nc, SUBLANES), :] = zero_tile

    ar = ar_ref[...]
    ai = ai_ref[...]
    fwd_lane = lax.broadcasted_iota(jnp.int32, (SUBLANES, LANES), 1) < SSM_STATE

    def step(i, carry):
        xr, xi = carry
        rows_f = pl.ds(i, SUBLANES, stride=_S5_ROWS)
        rows_b = pl.ds(nc - 1 - i, SUBLANES, stride=_S5_ROWS)
        er = jnp.where(fwd_lane, er_s[rows_f, :], er_s[rows_b, :])
        ei = jnp.where(fwd_lane, ei_s[rows_f, :], ei_s[rows_b, :])
        nr = ar * xr - ai * xi + er
        ni = ar * xi + ai * xr + ei
        to_f = pl.ds(i + 1, SUBLANES, stride=_S5_ROWS)
        to_b = pl.ds(nc + 6 - i, SUBLANES, stride=_S5_ROWS)
        fr_s[to_f, :] = nr
        fi_s[to_f, :] = ni
        br_s[to_b, :] = nr
        bi_s[to_b, :] = ni
        return nr, ni

    lax.fori_loop(0, nc, step, (zero_tile, zero_tile), unroll=16)

    fwd_col = lax.broadcasted_iota(jnp.int32, (nc, LANES), 1) < SSM_STATE
    for g in range(SSM_GB):
        base = g * _S5_ROWS
        cr = jnp.where(fwd_col, fr_s[pl.ds(base, nc), :], br_s[pl.ds(base + SUBLANES, nc), :])
        ci = jnp.where(fwd_col, fi_s[pl.ds(base, nc), :], bi_s[pl.ds(base + SUBLANES, nc), :])
        y = (yi_s[g]
             + jnp.dot(cr.astype(BF16), w2_ref[g, :LANES, :], preferred_element_type=F32)
             + jnp.dot(ci.astype(BF16), w2_ref[g, LANES:, :], preferred_element_type=F32))
        yg_s[:, g * SSM_CW:(g + 1) * SSM_CW] = _gelu_tanh(y).astype(BF16)

    z = lax.dot_general(yg_s[...], perm_ref[...], (((1,), (1,)), ((), ())), preferred_element_type=F32)
    for j in range(SSM_CHUNK):
        y_ref[pl.ds(j, nc, stride=SSM_CHUNK), :] = z[:, j * LANES:(j + 1) * LANES]


def _s5_mixer(u3, perm, wm, we, w2, ar, ai, layer, bsz):
    nblk, rows, width = u3.shape
    nc = rows // bsz
    gb = SSM_GB
    cw = SSM_CW
    kern = functools.partial(_s5_kernel, n_chunks=nc)
    return pl.pallas_call(
        kern,
        grid=(bsz, nblk),
        in_specs=[
            pl.BlockSpec((None, nc, width), lambda b, j: (j, b, 0)),
            pl.BlockSpec((width, width), lambda b, j: (0, 0)),
            pl.BlockSpec((None, gb, cw, cw), lambda b, j: (layer, j, 0, 0)),
            pl.BlockSpec((None, gb, cw, cw), lambda b, j: (layer, j, 0, 0)),
            pl.BlockSpec((None, gb, cw, cw), lambda b, j: (layer, j, 0, 0)),
            pl.BlockSpec((None, gb, LANES), lambda b, j: (layer, j, 0)),
            pl.BlockSpec((None, gb, LANES), lambda b, j: (layer, j, 0)),
        ],
        out_specs=pl.BlockSpec((nc * SSM_CHUNK, LANES), lambda b, j: (b, j)),
        out_shape=jax.ShapeDtypeStruct((bsz * nc * SSM_CHUNK, nblk * LANES), F32),
        scratch_shapes=[
            *[pltpu.VMEM((gb * _S5_ROWS, LANES), F32) for _ in range(6)],
            pltpu.VMEM((gb, nc, cw), F32),
            pltpu.VMEM((nc, width), BF16),
        ],
        compiler_params=_cparams(("parallel", "parallel")),
        name="s5_mixer",
    )(u3, perm, wm, we, w2, ar, ai)


def _layer_norm(h, g, b):
    mu = jnp.mean(h, axis=-1, keepdims=True)
    c = h - mu
    var = jnp.mean(c * c, axis=-1, keepdims=True)
    return c * lax.rsqrt(var + EPS) * g + b


def _ln0_kernel(x_ref, g_ref, b_ref, xf_ref, xb_ref):
    y = _layer_norm(x_ref[...], g_ref[...], b_ref[...])
    xf_ref[...] = y
    xb_ref[...] = y.astype(BF16)


def _row_spec(tm, width):
    return pl.BlockSpec((tm, width), lambda m: (m, 0))


def _resident_spec(block_shape, index_map):
    return pl.BlockSpec(block_shape, index_map, pipeline_mode=pl.Buffered(1))


def _layer_vec_spec(layer, width):
    return pl.BlockSpec((None, 1, width), lambda m: (layer, 0, 0))


def _ln0(x, g, b, tm=256):
    t, d = x.shape
    tm = min(tm, t)
    return pl.pallas_call(
        _ln0_kernel,
        grid=(t // tm,),
        in_specs=[_row_spec(tm, d),
                  pl.BlockSpec((1, d), lambda m: (0, 0)),
                  pl.BlockSpec((1, d), lambda m: (0, 0))],
        out_specs=[_row_spec(tm, d), _row_spec(tm, d)],
        out_shape=[jax.ShapeDtypeStruct((t, d), F32), jax.ShapeDtypeStruct((t, d), BF16)],
        compiler_params=_cparams(("parallel",)),
        name="ln0",
    )(x, g.reshape(1, d), b.reshape(1, d))


def _inproj_kernel(x_ref, w_ref, qg_ref, kg_ref, cos_ref, sin_ref, q_ref, k_ref, v_ref, u_ref, u_s):
    x = x_ref[...]
    cos2 = cos_ref[...]
    sin2 = sin_ref[...]
    scale = HEAD_DIM ** -0.5

    def head(z, gain, mult):
        zn = z * lax.rsqrt(jnp.mean(z * z, axis=-1, keepdims=True) + EPS) * gain
        rot = pltpu.roll(zn, HEAD_DIM // 2, axis=1)
        out = zn * cos2 + rot * sin2
        return out * mult if mult != 1.0 else out

    pair = 2 * HEAD_DIM
    for c in range(D_IN_PROJ // pair):
        acc = jnp.dot(x, w_ref[:, c * pair:(c + 1) * pair], preferred_element_type=F32)
        col = c * pair
        if col < ATTN_WIDTH:
            for h in range(2):
                z = head(acc[:, h * HEAD_DIM:(h + 1) * HEAD_DIM], qg_ref[...], scale)
                q_ref[:, col + h * HEAD_DIM: col + (h + 1) * HEAD_DIM] = z.astype(BF16)
        elif col < QK_WIDTH:
            for h in range(2):
                z = head(acc[:, h * HEAD_DIM:(h + 1) * HEAD_DIM], kg_ref[...], 1.0)
                k_ref[:, h * HEAD_DIM:(h + 1) * HEAD_DIM] = z.astype(BF16)
        elif col < QK_WIDTH + KV_WIDTH:
            v_ref[...] = acc.astype(BF16)
        else:
            tm = acc.shape[0]
            for half in range(2):
                blk = (col - (QK_WIDTH + KV_WIDTH)) // LANES + half
                u_s[...] = acc[:, half * LANES:(half + 1) * LANES]
                for j in range(SSM_CHUNK):
                    step_rows = u_s[pl.ds(j, tm // SSM_CHUNK, stride=SSM_CHUNK), :]
                    u_ref[blk, :, j * LANES:(j + 1) * LANES] = step_rows.astype(BF16)


def _inproj(xb, w_in, q_gain, k_gain, cos2, sin2, layer, seq_len, tm=512):
    t, d = xb.shape
    tm = min(tm, seq_len)
    sblocks = seq_len // tm
    nblk = SSM_GROUPS // SSM_GB
    return pl.pallas_call(
        _inproj_kernel,
        grid=(t // tm,),
        in_specs=[_row_spec(tm, d),
                  _resident_spec((None, d, D_IN_PROJ), lambda m: (layer, 0, 0)),
                  _layer_vec_spec(layer, HEAD_DIM),
                  _layer_vec_spec(layer, HEAD_DIM),
                  pl.BlockSpec((tm, HEAD_DIM), lambda m: (m % sblocks, 0)),
                  pl.BlockSpec((tm, HEAD_DIM), lambda m: (m % sblocks, 0))],
        out_specs=[_row_spec(tm, ATTN_WIDTH), _row_spec(tm, KV_WIDTH), _row_spec(tm, KV_WIDTH),
                   pl.BlockSpec((nblk, tm // SSM_CHUNK, SSM_GB * SSM_CW), lambda m: (0, m, 0))],
        out_shape=[jax.ShapeDtypeStruct((t, ATTN_WIDTH), BF16),
                   jax.ShapeDtypeStruct((t, KV_WIDTH), BF16),
                   jax.ShapeDtypeStruct((t, KV_WIDTH), BF16),
                   jax.ShapeDtypeStruct((nblk, t // SSM_CHUNK, SSM_GB * SSM_CW), BF16)],
        scratch_shapes=[pltpu.VMEM((tm, LANES), F32)],
        compiler_params=_cparams(("parallel",)),
        name="inproj",
    )(xb, w_in, q_gain, k_gain, cos2, sin2)


def _attn_kernel(q_ref, k_ref, v_ref, g_ref, o_ref):
    outs = []
    ssq = None
    for h in range(N_HEADS):
        kv = h // GQA
        q = q_ref[:, h * HEAD_DIM:(h + 1) * HEAD_DIM]
        k = k_ref[:, kv * HEAD_DIM:(kv + 1) * HEAD_DIM]
        v = v_ref[:, kv * HEAD_DIM:(kv + 1) * HEAD_DIM]
        s = lax.dot_general(q, k, (((1,), (1,)), ((), ())), preferred_element_type=F32)
        m = jnp.max(s, axis=-1, keepdims=True)
        p = jnp.exp(s - m)
        l = jnp.sum(p, axis=-1, keepdims=True)
        o = jnp.dot(p.astype(BF16), v, preferred_element_type=F32) / l
        outs.append(o)
        sq = jnp.sum(o * o, axis=-1, keepdims=True)
        ssq = sq if ssq is None else ssq + sq
    r = lax.rsqrt(ssq * (1.0 / ATTN_WIDTH) + EPS)
    for h in range(N_HEADS):
        sl = slice(h * HEAD_DIM, (h + 1) * HEAD_DIM)
        o_ref[:, sl] = (outs[h] * r * g_ref[:, sl]).astype(BF16)


def _attention(q, k, v, gain, layer, bsz, seq_len, tq=512):
    t = q.shape[0]
    tq = min(tq, seq_len)
    qblocks = seq_len // tq
    return pl.pallas_call(
        _attn_kernel,
        grid=(bsz, qblocks),
        in_specs=[pl.BlockSpec((tq, ATTN_WIDTH), lambda b, i: (b * qblocks + i, 0)),
                  pl.BlockSpec((seq_len, KV_WIDTH), lambda b, i: (b, 0)),
                  pl.BlockSpec((seq_len, KV_WIDTH), lambda b, i: (b, 0)),
                  pl.BlockSpec((None, 1, ATTN_WIDTH), lambda b, i: (layer, 0, 0))],
        out_specs=pl.BlockSpec((tq, ATTN_WIDTH), lambda b, i: (b * qblocks + i, 0)),
        out_shape=jax.ShapeDtypeStruct((t, ATTN_WIDTH), BF16),
        compiler_params=_cparams(("parallel", "parallel")),
        name="attention",
    )(q, k, v, gain)


def _glu_kernel(y_ref, w_ref, b_ref, g_ref, o_ref):
    y = y_ref[...]
    gate = jnp.dot(y.astype(BF16), w_ref[...], preferred_element_type=F32) + b_ref[...]
    z = y * jax.nn.sigmoid(gate)
    zn = z * lax.rsqrt(jnp.mean(z * z, axis=-1, keepdims=True) + EPS) * g_ref[...]
    o_ref[...] = zn.astype(BF16)


def _glu(y, w_glu, b_glu, gain, layer, tm=512):
    t, w = y.shape
    tm = min(tm, t)
    return pl.pallas_call(
        _glu_kernel,
        grid=(t // tm,),
        in_specs=[_row_spec(tm, w),
                  pl.BlockSpec((None, w, w), lambda m: (layer, 0, 0)),
                  _layer_vec_spec(layer, w),
                  _layer_vec_spec(layer, w)],
        out_specs=_row_spec(tm, w),
        out_shape=jax.ShapeDtypeStruct((t, w), BF16),
        compiler_params=_cparams(("parallel",)),
        name="glu",
    )(y, w_glu, b_glu, gain)


ROUTE_LANE_GROUP = N_EXPERTS


def _route(logits):
    lane = lax.broadcasted_iota(jnp.int32, logits.shape, 1)
    neg = jnp.float32(-1e30)
    big = jnp.int32(LANES)

    def masked_softmax(mask):
        z = jnp.where(mask, logits, neg)
        zmax = jnp.max(z, axis=-1, keepdims=True)
        e = jnp.where(mask, jnp.exp(z - zmax), 0.0)
        return e / jnp.sum(e, axis=-1, keepdims=True)

    def first_argmax(vals, mask):
        top = jnp.max(jnp.where(mask, vals, -1.0), axis=-1, keepdims=True)
        idx = jnp.min(jnp.where(mask & (vals == top), lane, big), axis=-1, keepdims=True)
        return top, idx

    gmask = (lane >= ROUTE_LANE_GROUP) & (lane < ROUTE_LANE_GROUP + N_EXPERT_GROUPS)
    g_top, g_lane = first_argmax(masked_softmax(gmask), gmask)
    e_lo = (g_lane - ROUTE_LANE_GROUP) * EXPERTS_PER_GROUP
    emask = (lane >= e_lo) & (lane < e_lo + EXPERTS_PER_GROUP)
    e_prob = masked_softmax(emask)
    v1, i1 = first_argmax(e_prob, emask)
    v2, i2 = first_argmax(e_prob, emask & (lane != i1))
    denom = v1 + v2
    return i1, i2, g_top * (v1 / denom), g_top * (v2 / denom)


TOK_ROWS = D_MODEL // LANES
MOE_TM = 256
GATHER_PITCH = 24
RL_E1, RL_E2, RL_G1, RL_G2, RL_R1, RL_R2 = 0, 1, 2, 3, 4, 5


def _to_token_major(ref, x):
    tm = x.shape[0]
    for s in range(TOK_ROWS):
        ref[pl.ds(s, tm, stride=TOK_ROWS), :] = x[:, s * LANES:(s + 1) * LANES]


def _from_token_major(ref, tm, s):
    return ref[pl.ds(s, tm, stride=TOK_ROWS), :]


def _outproj_kernel(a_ref, y_ref, wa_ref, wy_ref, x_ref, g_ref, b_ref, wr_ref, br_ref,
                    xb_ref, x16_ref, r_ref, meta_ref, cnt_ref, run_s):
    tm = a_ref.shape[0]

    @pl.when(pl.program_id(0) == 0)
    def _():
        run_s[...] = jnp.zeros_like(run_s)

    m = (jnp.dot(a_ref[...], wa_ref[...], preferred_element_type=F32)
         + jnp.dot(y_ref[...], wy_ref[...], preferred_element_type=F32))
    xn = _layer_norm(ALPHA * x_ref[...] + m, g_ref[...], b_ref[...])
    xn_hi = xn.astype(BF16)
    xb_ref[...] = xn_hi
    _to_token_major(x16_ref, xn)
    xn_lo = (xn - xn_hi.astype(F32)).astype(BF16)
    logits = (jnp.dot(xn_hi, wr_ref[0], preferred_element_type=F32)
              + jnp.dot(xn_hi, wr_ref[1], preferred_element_type=F32)
              + jnp.dot(xn_lo, wr_ref[0], preferred_element_type=F32)) + br_ref[...]
    i1, i2, g1, g2 = _route(logits)

    lane = lax.broadcasted_iota(jnp.int32, (tm, LANES), 1)
    hit1, hit2 = lane == i1, lane == i2
    onehot = (hit1 | hit2).astype(BF16)
    earlier = (lax.broadcasted_iota(jnp.int32, (tm, tm), 0)
               > lax.broadcasted_iota(jnp.int32, (tm, tm), 1)).astype(BF16)
    rank = jnp.dot(earlier, onehot, preferred_element_type=F32) + run_s[0:1, :]
    r1 = jnp.sum(jnp.where(hit1, rank, 0.0), axis=-1, keepdims=True)
    r2 = jnp.sum(jnp.where(hit2, rank, 0.0), axis=-1, keepdims=True)
    total = run_s[0:1, :] + jnp.sum(onehot.astype(F32), axis=0, keepdims=True)
    run_s[...] = jnp.broadcast_to(total, run_s.shape)
    cnt_ref[...] = jnp.broadcast_to(total, cnt_ref.shape)
    out = jnp.zeros((tm, LANES), F32)
    for col, val in ((RL_E1, i1.astype(F32)), (RL_E2, i2.astype(F32)), (RL_G1, g1), (RL_G2, g2),
                     (RL_R1, r1), (RL_R2, r2)):
        out = jnp.where(lane == col, val, out)
    r_ref[...] = out
    meta_ref[...] = out.T[:SUBLANES, :].astype(jnp.int32)


def _outproj(attn_n, y_n, w_out, xf, ln_g, ln_b, w_route, b_route, layer, tm=512):
    t, d = xf.shape
    tm = min(tm, t)
    half = D_MIX // 2
    return pl.pallas_call(
        _outproj_kernel,
        grid=(t // tm,),
        in_specs=[_row_spec(tm, half), _row_spec(tm, half),
                  _resident_spec((None, half, d), lambda m: (layer, 0, 0)),
                  _resident_spec((None, half, d), lambda m: (layer, 1, 0)),
                  _row_spec(tm, d),
                  _layer_vec_spec(layer, d), _layer_vec_spec(layer, d),
                  _resident_spec((None, 2, d, LANES), lambda m: (layer, 0, 0, 0)),
                  _layer_vec_spec(layer, LANES)],
        out_specs=[_row_spec(tm, d), _row_spec(tm * TOK_ROWS, LANES), _row_spec(tm, LANES),
                   pl.BlockSpec((SUBLANES, tm), lambda m: (0, m)),
                   pl.BlockSpec((SUBLANES, LANES), lambda m: (0, 0))],
        out_shape=[jax.ShapeDtypeStruct((t, d), BF16),
                   jax.ShapeDtypeStruct((t * TOK_ROWS, LANES), F32),
                   jax.ShapeDtypeStruct((t, LANES), F32),
                   jax.ShapeDtypeStruct((SUBLANES, t), jnp.int32),
                   jax.ShapeDtypeStruct((SUBLANES, LANES), F32)],
        scratch_shapes=[pltpu.VMEM((SUBLANES, LANES), F32)],
        compiler_params=_cparams(("arbitrary",)),
        name="outproj_ln_route",
    )(attn_n, y_n, w_out, w_out, xf, ln_g, ln_b, w_route, b_route)


def _token_copy(src, src_tok, dst, dst_tok, sem, dst_pitch=TOK_ROWS):
    return pltpu.make_async_copy(
        src.at[pl.ds(pl.multiple_of(src_tok * TOK_ROWS, TOK_ROWS), TOK_ROWS), :],
        dst.at[pl.ds(pl.multiple_of(dst_tok * dst_pitch, SUBLANES), TOK_ROWS), :],
        sem)


def _sorted_row(es_ref, e_ref, r_ref, t):
    return es_ref[e_ref[t]] + r_ref[t]


def _dispatch_kernel(es_ref, e1_ref, e2_ref, r1_ref, r2_ref, x16_ref, xb_ref, wg_ref, bg_ref, xs_in_ref,
                     xs_ref, gate_ref, sem, *, tm):
    del xs_in_ref
    for t in range(tm):
        _token_copy(x16_ref, t, xs_ref, _sorted_row(es_ref, e1_ref, r1_ref, t), sem.at[0]).start()
        _token_copy(x16_ref, t, xs_ref, _sorted_row(es_ref, e2_ref, r2_ref, t), sem.at[1]).start()
    gate = jnp.dot(xb_ref[...], wg_ref[...], preferred_element_type=F32) + bg_ref[...]
    gate_ref[...] = jax.nn.sigmoid(gate).astype(BF16)
    whole = pl.ds(0, tm * TOK_ROWS)
    pltpu.make_async_copy(x16_ref, xs_ref.at[whole, :], sem.at[0]).wait()
    pltpu.make_async_copy(x16_ref, xs_ref.at[whole, :], sem.at[1]).wait()


def _smem_tile_spec(tm):
    return pl.BlockSpec((tm,), lambda m: (m,), memory_space=pltpu.SMEM)


def _token_meta_specs(tm):
    return [pl.BlockSpec(memory_space=pltpu.SMEM)] + [_smem_tile_spec(tm)] * 4


def _dispatch(x16, e_start, meta, xs_buf, xb, w_pg, b_pg, layer, tm=512):
    t, d = xb.shape
    tm = min(tm, t)
    return pl.pallas_call(
        functools.partial(_dispatch_kernel, tm=tm),
        grid=(t // tm,),
        in_specs=_token_meta_specs(tm) + [_row_spec(tm * TOK_ROWS, LANES),
                                          _row_spec(tm, d),
                                          _resident_spec((None, d, d), lambda m: (layer, 0, 0)),
                                          _layer_vec_spec(layer, d),
                                          pl.BlockSpec(memory_space=pl.ANY)],
        out_specs=[pl.BlockSpec(memory_space=pl.ANY), _row_spec(tm, d)],
        out_shape=[jax.ShapeDtypeStruct(xs_buf.shape, F32), jax.ShapeDtypeStruct((t, d), BF16)],
        input_output_aliases={9: 0},
        scratch_shapes=[pltpu.SemaphoreType.DMA((2,))],
        compiler_params=_cparams(("arbitrary",)),
        name="moe_dispatch_gate",
    )(e_start, meta[RL_E1], meta[RL_E2], meta[RL_R1], meta[RL_R2], x16, xb, w_pg, b_pg, xs_buf)


def _expert_kernel(te_ref, tr_ref, xs_ref, wg_ref, wu_ref, wd_ref, ys_ref, lhs_s, wg_s, wu_s, wd_s, *, tm):
    j = pl.program_id(0)
    rows = tr_ref[j]

    @pl.when(rows == 0)
    def _():
        ys_ref[...] = jnp.zeros_like(ys_ref)

    @pl.when((rows > 0) & ((j == 0) | (te_ref[j] != te_ref[jnp.maximum(j - 1, 0)])))
    def _():
        wg_s[...] = wg_ref[...].astype(BF16)
        wu_s[...] = wu_ref[...].astype(BF16)
        wd_s[...] = wd_ref[...].astype(BF16)

    @pl.when(rows > 0)
    def _():
        valid = lax.broadcasted_iota(jnp.int32, (tm, LANES), 0) < rows
        for s in range(TOK_ROWS):
            blk = jnp.where(valid, _from_token_major(xs_ref, tm, s), 0.0)
            lhs_s[:, s * LANES:(s + 1) * LANES] = blk.astype(BF16)
        x = lhs_s[...]
        hg = jnp.dot(x, wg_s[...], preferred_element_type=F32)
        hu = jnp.dot(x, wu_s[...], preferred_element_type=F32)
        hid = ((hg * jax.nn.sigmoid(hg)) * hu).astype(BF16)
        for c in range(D_MODEL // D_EXPERT):
            out = jnp.dot(hid, wd_s[:, c * D_EXPERT:(c + 1) * D_EXPERT], preferred_element_type=F32)
            for h in range(D_EXPERT // LANES):
                s = c * (D_EXPERT // LANES) + h
                ys_ref[pl.ds(s, tm, stride=TOK_ROWS), :] = out[:, h * LANES:(h + 1) * LANES]


def _moe_n_tiles(t, tm):
    return (2 * t) // tm + N_EXPERTS


def _moe_tiles(counts, t, tm):
    n_tiles = _moe_n_tiles(t, tm)
    per_e = (counts + tm - 1) // tm
    ends = jnp.cumsum(per_e)
    starts = ends - per_e
    j = jnp.arange(n_tiles, dtype=jnp.int32)[:, None]
    e = jnp.minimum(jnp.sum((j >= ends[None, :]).astype(jnp.int32), axis=1), N_EXPERTS - 1)
    mine = (j >= starts[None, :]) & (j < ends[None, :])
    rows = jnp.sum(jnp.where(mine, jnp.clip(counts[None, :] - (j - starts[None, :]) * tm, 0, tm), 0), axis=1)
    return e.astype(jnp.int32), rows.astype(jnp.int32), (starts * tm).astype(jnp.int32)


def _experts(xs, te, tr, w_gate, w_up, w_down, layer, tm):
    d = D_MODEL
    blk = pl.BlockSpec((tm * TOK_ROWS, LANES), lambda j, te, tr: (j, 0))
    grid_spec = pltpu.PrefetchScalarGridSpec(
        num_scalar_prefetch=2,
        grid=(te.shape[0],),
        in_specs=[blk,
                  pl.BlockSpec((None, None, d, D_EXPERT), lambda j, te, tr: (layer, te[j], 0, 0)),
                  pl.BlockSpec((None, None, d, D_EXPERT), lambda j, te, tr: (layer, te[j], 0, 0)),
                  pl.BlockSpec((None, None, D_EXPERT, d), lambda j, te, tr: (layer, te[j], 0, 0))],
        out_specs=blk,
        scratch_shapes=[pltpu.VMEM((tm, d), BF16),
                        pltpu.VMEM((d, D_EXPERT), BF16),
                        pltpu.VMEM((d, D_EXPERT), BF16),
                        pltpu.VMEM((D_EXPERT, d), BF16)],
    )
    return pl.pallas_call(
        functools.partial(_expert_kernel, tm=tm),
        grid_spec=grid_spec,
        out_shape=jax.ShapeDtypeStruct(xs.shape, F32),
        compiler_params=_cparams(("arbitrary",)),
        name="moe_experts",
    )(te, tr, xs, w_gate, w_up, w_down)


def _combine_kernel(es_ref, e1_ref, e2_ref, r1_ref, r2_ref, e1n_ref, e2n_ref, r1n_ref, r2n_ref,
                    gate_ref, x16_ref, r_ref, p_ref, wp_ref, g_ref, b_ref, ys_ref, of_ref, ob_ref,
                    *scratch, sub, n_sub, lead):
    slots = [(scratch[2 * k], scratch[2 * k + 1]) for k in range(n_sub)]
    h_s, sem = scratch[2 * n_sub], scratch[2 * n_sub + 1]
    i = pl.program_id(0)
    this_step = (e1_ref, e2_ref, r1_ref, r2_ref)
    next_step = (e1n_ref, e2n_ref, r1n_ref, r2n_ref)

    def issue(slot, fields, off):
        e1, e2, r1, r2 = fields
        y1_s, y2_s = slots[slot]
        for t in range(sub):
            _token_copy(ys_ref, _sorted_row(es_ref, e1, r1, off + t), y1_s, t, sem.at[slot, 0],
                        GATHER_PITCH).start()
            _token_copy(ys_ref, _sorted_row(es_ref, e2, r2, off + t), y2_s, t, sem.at[slot, 1],
                        GATHER_PITCH).start()

    def wait(slot):
        whole = pl.ds(0, sub * TOK_ROWS)
        for k in range(2):
            pltpu.make_async_copy(ys_ref.at[whole, :], slots[slot][k].at[whole, :], sem.at[slot, k]).wait()

    def combine(slot, off):
        y1_s, y2_s = slots[slot]
        rows = pl.ds(off, sub)
        ple = (jnp.dot(p_ref[rows, :].astype(BF16), wp_ref[...], preferred_element_type=F32)
               * gate_ref[rows, :].astype(F32))
        g1 = r_ref[rows, RL_G1:RL_G1 + 1]
        g2 = r_ref[rows, RL_G2:RL_G2 + 1]
        for s in range(TOK_ROWS):
            sl = slice(s * LANES, (s + 1) * LANES)
            f = (g1 * y1_s[pl.ds(s, sub, stride=GATHER_PITCH), :]
                 + g2 * y2_s[pl.ds(s, sub, stride=GATHER_PITCH), :])
            resid = x16_ref[pl.ds(off * TOK_ROWS + s, sub, stride=TOK_ROWS), :]
            h_s[:, sl] = ALPHA * resid + f + ple[:, sl]
        xn = _layer_norm(h_s[...], g_ref[...], b_ref[...])
        of_ref[rows, :] = xn
        ob_ref[rows, :] = xn.astype(BF16)

    @pl.when(i == 0)
    def _():
        for k in range(lead):
            issue(k, this_step, k * sub)

    for k in range(n_sub):
        wait(k)
        ahead = k + lead
        if ahead < n_sub:
            issue(ahead, this_step, ahead * sub)
        else:
            issue(ahead - n_sub, next_step, (ahead - n_sub) * sub)
        combine(k, k * sub)

    @pl.when(i == pl.num_programs(0) - 1)
    def _():
        for k in range(lead):
            wait(k)


COMBINE_SUB = 128
COMBINE_SLOTS = 4
COMBINE_LEAD = 2


def _combine_ple_ln(gate, x16, route, e_start, meta, ys, p, w_ple, ln_g, ln_b, layer):
    t, d = gate.shape
    n_sub, lead = COMBINE_SLOTS, COMBINE_LEAD
    sub = min(COMBINE_SUB, t // n_sub)
    tm = n_sub * sub
    steps = t // tm
    ahead = lead * sub
    fields = (meta[RL_E1], meta[RL_E2], meta[RL_R1], meta[RL_R2])
    next_spec = pl.BlockSpec((ahead,), lambda m: (jnp.minimum(m + 1, steps - 1) * (tm // ahead),),
                             memory_space=pltpu.SMEM)
    return pl.pallas_call(
        functools.partial(_combine_kernel, sub=sub, n_sub=n_sub, lead=lead),
        grid=(steps,),
        in_specs=_token_meta_specs(tm) + [next_spec] * 4 + [
            _row_spec(tm, d), _row_spec(tm * TOK_ROWS, LANES), _row_spec(tm, LANES),
            pl.BlockSpec((None, tm, PLE_DIM), lambda m: (layer, m, 0)),
            _resident_spec((None, PLE_DIM, d), lambda m: (layer, 0, 0)),
            _layer_vec_spec(layer, d), _layer_vec_spec(layer, d),
            pl.BlockSpec(memory_space=pl.ANY)],
        out_specs=[_row_spec(tm, d), _row_spec(tm, d)],
        out_shape=[jax.ShapeDtypeStruct((t, d), F32), jax.ShapeDtypeStruct((t, d), BF16)],
        scratch_shapes=[*[pltpu.VMEM((sub * GATHER_PITCH, LANES), F32) for _ in range(2 * n_sub)],
                        pltpu.VMEM((sub, d), F32),
                        pltpu.SemaphoreType.DMA((n_sub, 2))],
        compiler_params=_cparams(("arbitrary",)),
        name="moe_combine_ple_ln",
    )(e_start, *fields, *fields, gate, x16, route, p, w_ple, ln_g, ln_b, ys)


def _rope_tables(seq_len):
    rows = seq_len // GRID_W
    row = jnp.repeat(jnp.arange(rows, dtype=F32), GRID_W)
    col = jnp.tile(jnp.arange(GRID_W, dtype=F32), rows)
    inv_freq = ROPE_THETA ** (-jnp.arange(ROPE_AXIS_PAIRS, dtype=F32) / ROPE_AXIS_PAIRS)
    ang = jnp.concatenate([row[:, None] * inv_freq, col[:, None] * inv_freq], axis=-1)
    cos, sin = jnp.cos(ang), jnp.sin(ang)
    return jnp.concatenate([cos, cos], axis=-1), jnp.concatenate([-sin, sin], axis=-1)


def kernel(x, p, ln0_g, ln0_b, w_in, q_norm_g, k_norm_g, ssm_a_re, ssm_a_im, ssm_log_dt, ssm_b_re, ssm_b_im, ssm_c_re, ssm_c_im, ssm_d, w_glu, b_glu, attn_out_g, ssm_out_g, w_out, ln1_g, ln1_b, router_group_w, router_group_b, router_expert_w, router_expert_b, w_gate_e, w_up_e, w_down_e, w_ple, w_ple_gate, b_ple_gate, ln2_g, ln2_b):
    bsz, seq_len, d = x.shape
    t = bsz * seq_len
    n_layers = w_in.shape[0]
    vec = lambda a: a.reshape(n_layers, 1, a.shape[-1])

    cos2, sin2 = _rope_tables(seq_len)
    w_in_b, w_glu_b, w_out_b = w_in.astype(BF16), w_glu.astype(BF16), w_out.astype(BF16)
    w_ple_b, w_pg_b = w_ple.astype(BF16), w_ple_gate.astype(BF16)
    pad = LANES - N_EXPERTS - N_EXPERT_GROUPS
    w_route = jnp.pad(jnp.concatenate([router_expert_w, router_group_w], axis=-1), ((0, 0), (0, 0), (0, pad)))
    w_route_hi = w_route.astype(BF16)
    w_route_lo = (w_route - w_route_hi.astype(F32)).astype(BF16)
    w_route = jnp.stack([w_route_hi, w_route_lo], axis=1)
    b_route = jnp.pad(jnp.concatenate([router_expert_b, router_group_b], axis=-1), ((0, 0), (0, pad)))
    s5_ops = jax.vmap(_s5_operands)(ssm_a_re, ssm_a_im, ssm_log_dt, ssm_b_re, ssm_b_im,
                                    ssm_c_re, ssm_c_im, ssm_d)
    p2 = p.reshape(n_layers, t, PLE_DIM)
    perm = _s5_perm()

    tm_e = min(MOE_TM, t)
    xs = jnp.zeros((_moe_n_tiles(t, tm_e) * tm_e * TOK_ROWS, LANES), F32)
    xf, xb = _ln0(x.reshape(t, d), ln0_g, ln0_b)
    for i in range(n_layers):
        q, k, v, u = _inproj(xb, w_in_b, vec(q_norm_g), vec(k_norm_g), cos2, sin2, i, seq_len)
        attn_n = _attention(q, k, v, vec(attn_out_g), i, bsz, seq_len)
        y = _s5_mixer(u, perm, *s5_ops, i, bsz)
        y_n = _glu(y, w_glu_b, vec(b_glu), vec(ssm_out_g), i)
        xb, x16, route, meta, counts = _outproj(attn_n, y_n, w_out_b, xf, vec(ln1_g), vec(ln1_b),
                                                w_route, vec(b_route), i)
        te, tr, e_start = _moe_tiles(counts[0, :N_EXPERTS].astype(jnp.int32), t, tm_e)
        xs, gate = _dispatch(x16, e_start, meta, xs, xb, w_pg_b, vec(b_ple_gate), i)
        ys = _experts(xs, te, tr, w_gate_e, w_up_e, w_down_e, i, tm_e)
        xf, xb = _combine_ple_ln(gate, x16, route, e_start, meta, ys, p2, w_ple_b,
                                 vec(ln2_g), vec(ln2_b), i)
    return xf.reshape(bsz, seq_len, d)
```

```python
import functools
import math

import jax
import jax.numpy as jnp
from jax import lax
from jax.experimental import pallas as pl
from jax.experimental.pallas import tpu as pltpu

F32 = jnp.float32
BF16 = jnp.bfloat16

D_MODEL = 2048
DEPTH = 4
GRID_W = 64
N_HEADS = 8
N_KV_HEADS = 2
GQA = N_HEADS // N_KV_HEADS
HEAD_DIM = 128
ATTN_WIDTH = N_HEADS * HEAD_DIM
KV_WIDTH = N_KV_HEADS * HEAD_DIM
QK_WIDTH = ATTN_WIDTH + KV_WIDTH
ROPE_THETA = 10000.0
ROPE_AXIS_PAIRS = HEAD_DIM // 4
SSM_WIDTH = D_MODEL // 2
SSM_CH = 16
SSM_GROUPS = SSM_WIDTH // SSM_CH
SSM_STATE = 64
D_MIX = ATTN_WIDTH + SSM_WIDTH
D_IN_PROJ = ATTN_WIDTH + 2 * KV_WIDTH + SSM_WIDTH
N_EXPERT_GROUPS = 4
EXPERTS_PER_GROUP = 4
N_EXPERTS = N_EXPERT_GROUPS * EXPERTS_PER_GROUP
D_EXPERT = 512
PLE_DIM = 256
ALPHA = (2 * DEPTH) ** 0.25
EPS = 1e-6

LANES = 128
SUBLANES = 8
VMEM_LIMIT = 56 * 1024 * 1024

SSM_CHUNK = 16
SSM_CW = SSM_CHUNK * SSM_CH
SSM_GB = 8


def _cparams(sem):
    return pltpu.CompilerParams(dimension_semantics=sem, vmem_limit_bytes=VMEM_LIMIT)


def _s5_discretize(a_re, a_im, log_dt, b_re, b_im):
    dt = jnp.exp(log_dt)[:, None]
    lam_re, lam_im = dt * a_re, dt * a_im
    mag = jnp.exp(lam_re)
    abar_re, abar_im = mag * jnp.cos(lam_im), mag * jnp.sin(lam_im)
    den = a_re * a_re + a_im * a_im
    nr, ni = abar_re - 1.0, abar_im
    coef_re = (nr * a_re + ni * a_im) / den
    coef_im = (ni * a_re - nr * a_im) / den
    bb_re = coef_re[..., None] * b_re - coef_im[..., None] * b_im
    bb_im = coef_re[..., None] * b_im + coef_im[..., None] * b_re
    return lam_re, lam_im, bb_re, bb_im


def _cpow(lam_re, lam_im, k):
    mag = jnp.exp(lam_re * k)
    return mag * jnp.cos(lam_im * k), mag * jnp.sin(lam_im * k)


def _s5_operands(a_re, a_im, log_dt, b_re, b_im, c_re, c_im, d_skip):
    hp = lax.Precision.HIGHEST
    L, H, G = SSM_CHUNK, SSM_CH, SSM_GROUPS
    lam_f = _s5_discretize(a_re[0], a_im[0], log_dt[0], b_re[0], b_im[0])
    lam_b = _s5_discretize(a_re[1], a_im[1], log_dt[1], b_re[1], b_im[1])
    tau = jnp.arange(L, dtype=F32)

    def powers(lam, expo):
        return _cpow(lam[0][:, None, :], lam[1][:, None, :], expo[None, :, None])

    def conv_kernel(lam, cr, ci):
        pr, pi = powers(lam, tau)
        bbr, bbi = lam[2], lam[3]
        wr = pr[..., None] * bbr[:, None] - pi[..., None] * bbi[:, None]
        wi = pr[..., None] * bbi[:, None] + pi[..., None] * bbr[:, None]
        return (jnp.einsum('gop,glpi->gloi', cr, wr, precision=hp)
                - jnp.einsum('gop,glpi->gloi', ci, wi, precision=hp))

    k_f = conv_kernel(lam_f, c_re[0], c_im[0])
    k_b = conv_kernel(lam_b, c_re[1], c_im[1])
    lag0 = (tau == 0.0).astype(F32)[None, :, None, None]
    k_f = k_f + lag0 * (d_skip[:, None, :, None] * jnp.eye(H, dtype=F32))
    lag = jnp.arange(L)[None, :] - jnp.arange(L)[:, None]
    l_idx = jnp.arange(L)[None, None, :]
    sel = jnp.concatenate([(lag[..., None] == l_idx), (-lag[..., None] == l_idx)], axis=-1).astype(F32)
    wm = jnp.einsum('stl,gloi->gsito', sel, jnp.concatenate([k_f, k_b], axis=1), precision=hp)
    wm = wm.reshape(G, L * H, L * H)

    def end_map(lam, expo):
        pr, pi = powers(lam, expo)
        bbr, bbi = lam[2].transpose(0, 2, 1)[:, None], lam[3].transpose(0, 2, 1)[:, None]
        pr, pi = pr[:, :, None, :], pi[:, :, None, :]
        return pr * bbr - pi * bbi, pr * bbi + pi * bbr

    ef_re, ef_im = end_map(lam_f, (L - 1) - tau)
    eb_re, eb_im = end_map(lam_b, tau)
    we = jnp.concatenate([ef_re, eb_re, ef_im, eb_im], axis=-1).reshape(G, L * H, 4 * SSM_STATE)

    def out_map(lam, cr, ci, expo):
        pr, pi = powers(lam, expo)
        pr, pi = pr.transpose(0, 2, 1)[..., None], pi.transpose(0, 2, 1)[..., None]
        cr, ci = cr.transpose(0, 2, 1)[:, :, None, :], ci.transpose(0, 2, 1)[:, :, None, :]
        return cr * pr - ci * pi, -(cr * pi + ci * pr)

    of_re, of_im = out_map(lam_f, c_re[0], c_im[0], tau + 1.0)
    ob_re, ob_im = out_map(lam_b, c_re[1], c_im[1], L - tau)
    w2 = jnp.concatenate([of_re, ob_re, of_im, ob_im], axis=1).reshape(G, 4 * SSM_STATE, L * H)

    afr, afi = _cpow(lam_f[0], lam_f[1], float(L))
    abr, abi = _cpow(lam_b[0], lam_b[1], float(L))
    ar = jnp.concatenate([afr, abr], axis=-1)
    ai = jnp.concatenate([afi, abi], axis=-1)
    return wm.astype(BF16), we.astype(BF16), w2.astype(BF16), ar, ai


_S5_ROWS = 264


def _gelu_tanh(y):
    return 0.5 * y * (1.0 + jnp.tanh(math.sqrt(2.0 / math.pi) * (y + 0.044715 * (y * y * y))))


def _s5_perm():
    n = SSM_GB * SSM_CW
    r = lax.broadcasted_iota(jnp.int32, (n, n), 0)
    c = lax.broadcasted_iota(jnp.int32, (n, n), 1)
    r_step, r_grp, r_ch = r // LANES, (r % LANES) // SSM_CH, r % SSM_CH
    c_grp, c_step, c_ch = c // SSM_CW, (c % SSM_CW) // SSM_CH, c % SSM_CH
    return ((r_step == c_step) & (r_grp == c_grp) & (r_ch == c_ch)).astype(BF16)


def _s5_kernel(u_ref, perm_ref, wm_ref, we_ref, w2_ref, ar_ref, ai_ref, y_ref,
               er_s, ei_s, fr_s, fi_s, br_s, bi_s, yi_s, yg_s, *, n_chunks):
    nc = n_chunks
    zero_tile = jnp.zeros((SUBLANES, LANES), F32)
    ug = jnp.dot(u_ref[...], perm_ref[...], preferred_element_type=F32).astype(BF16)
    for g in range(SSM_GB):
        base = g * _S5_ROWS
        u_g = ug[:, g * SSM_CW:(g + 1) * SSM_CW]
        yi_s[g] = jnp.dot(u_g, wm_ref[g], preferred_element_type=F32)
        ends = jnp.dot(u_g, we_ref[g], preferred_element_type=F32)
        er_s[pl.ds(base, nc), :] = ends[:, :LANES]
        ei_s[pl.ds(base, nc), :] = ends[:, LANES:]
        fr_s[pl.ds(base, SUBLANES), :] = zero_tile
        fi_s[pl.ds(base, SUBLANES), :] = zero_tile
        br_s[pl.ds(base + nc, SUBLANES), :] = zero_tile
        bi_s[pl.ds(base + nc, SUBLANES), :] = zero_tile

    ar = ar_ref[...]
    ai = ai_ref[...]
    fwd_lane = lax.broadcasted_iota(jnp.int32, (SUBLANES, LANES), 1) < SSM_STATE

    def step(i, carry):
        xr, xi = carry
        rows_f = pl.ds(i, SUBLANES, stride=_S5_ROWS)
        rows_b = pl.ds(nc - 1 - i, SUBLANES, stride=_S5_ROWS)
        er = jnp.where(fwd_lane, er_s[rows_f, :], er_s[rows_b, :])
        ei = jnp.where(fwd_lane, ei_s[rows_f, :], ei_s[rows_b, :])
        nr = ar * xr - ai * xi + er
        ni = ar * xi + ai * xr + ei
        to_f = pl.ds(i + 1, SUBLANES, stride=_S5_ROWS)
        to_b = pl.ds(nc + 6 - i, SUBLANES, stride=_S5_ROWS)
        fr_s[to_f, :] = nr
        fi_s[to_f, :] = ni
        br_s[to_b, :] = nr
        bi_s[to_b, :] = ni
        return nr, ni

    lax.fori_loop(0, nc, step, (zero_tile, zero_tile), unroll=16)

    fwd_col = lax.broadcasted_iota(jnp.int32, (nc, LANES), 1) < SSM_STATE
    for g in range(SSM_GB):
        base = g * _S5_ROWS
        cr = jnp.where(fwd_col, fr_s[pl.ds(base, nc), :], br_s[pl.ds(base + SUBLANES, nc), :])
        ci = jnp.where(fwd_col, fi_s[pl.ds(base, nc), :], bi_s[pl.ds(base + SUBLANES, nc), :])
        y = (yi_s[g]
             + jnp.dot(cr.astype(BF16), w2_ref[g, :LANES, :], preferred_element_type=F32)
             + jnp.dot(ci.astype(BF16), w2_ref[g, LANES:, :], preferred_element_type=F32))
        yg_s[:, g * SSM_CW:(g + 1) * SSM_CW] = _gelu_tanh(y).astype(BF16)

    z = lax.dot_general(yg_s[...], perm_ref[...], (((1,), (1,)), ((), ())), preferred_element_type=F32)
    for j in range(SSM_CHUNK):
        y_ref[pl.ds(j, nc, stride=SSM_CHUNK), :] = z[:, j * LANES:(j + 1) * LANES]


def _s5_mixer(u3, perm, wm, we, w2, ar, ai, layer, bsz):
    nblk, rows, width = u3.shape
    nc = rows // bsz
    gb = SSM_GB
    cw = SSM_CW
    kern = functools.partial(_s5_kernel, n_chunks=nc)
    return pl.pallas_call(
        kern,
        grid=(bsz, nblk),
        in_specs=[
            pl.BlockSpec((None, nc, width), lambda b, j: (j, b, 0)),
            pl.BlockSpec((width, width), lambda b, j: (0, 0)),
            pl.BlockSpec((None, gb, cw, cw), lambda b, j: (layer, j, 0, 0)),
            pl.BlockSpec((None, gb, cw, cw), lambda b, j: (layer, j, 0, 0)),
            pl.BlockSpec((None, gb, cw, cw), lambda b, j: (layer, j, 0, 0)),
            pl.BlockSpec((None, gb, LANES), lambda b, j: (layer, j, 0)),
            pl.BlockSpec((None, gb, LANES), lambda b, j: (layer, j, 0)),
        ],
        out_specs=pl.BlockSpec((nc * SSM_CHUNK, LANES), lambda b, j: (b, j)),
        out_shape=jax.ShapeDtypeStruct((bsz * nc * SSM_CHUNK, nblk * LANES), F32),
        scratch_shapes=[
            *[pltpu.VMEM((gb * _S5_ROWS, LANES), F32) for _ in range(6)],
            pltpu.VMEM((gb, nc, cw), F32),
            pltpu.VMEM((nc, width), BF16),
        ],
        compiler_params=_cparams(("parallel", "parallel")),
        name="s5_mixer",
    )(u3, perm, wm, we, w2, ar, ai)


def _layer_norm(h, g, b):
    mu = jnp.mean(h, axis=-1, keepdims=True)
    c = h - mu
    var = jnp.mean(c * c, axis=-1, keepdims=True)
    return c * lax.rsqrt(var + EPS) * g + b


def _ln0_kernel(x_ref, g_ref, b_ref, xf_ref, xb_ref):
    y = _layer_norm(x_ref[...], g_ref[...], b_ref[...])
    xf_ref[...] = y
    xb_ref[...] = y.astype(BF16)


def _row_spec(tm, width):
    return pl.BlockSpec((tm, width), lambda m: (m, 0))


def _resident_spec(block_shape, index_map):
    return pl.BlockSpec(block_shape, index_map, pipeline_mode=pl.Buffered(1))


def _layer_vec_spec(layer, width):
    return pl.BlockSpec((None, 1, width), lambda m: (layer, 0, 0))


def _ln0(x, g, b, tm=256):
    t, d = x.shape
    tm = min(tm, t)
    return pl.pallas_call(
        _ln0_kernel,
        grid=(t // tm,),
        in_specs=[_row_spec(tm, d),
                  pl.BlockSpec((1, d), lambda m: (0, 0)),
                  pl.BlockSpec((1, d), lambda m: (0, 0))],
        out_specs=[_row_spec(tm, d), _row_spec(tm, d)],
        out_shape=[jax.ShapeDtypeStruct((t, d), F32), jax.ShapeDtypeStruct((t, d), BF16)],
        compiler_params=_cparams(("parallel",)),
        name="ln0",
    )(x, g.reshape(1, d), b.reshape(1, d))


def _inproj_kernel(x_ref, w_ref, qg_ref, kg_ref, cos_ref, sin_ref, q_ref, k_ref, v_ref, u_ref, u_s):
    x = x_ref[...]
    cos2 = cos_ref[...]
    sin2 = sin_ref[...]
    scale = HEAD_DIM ** -0.5

    def head(z, gain, mult):
        zn = z * lax.rsqrt(jnp.mean(z * z, axis=-1, keepdims=True) + EPS) * gain
        rot = pltpu.roll(zn, HEAD_DIM // 2, axis=1)
        out = zn * cos2 + rot * sin2
        return out * mult if mult != 1.0 else out

    pair = 2 * HEAD_DIM
    for c in range(D_IN_PROJ // pair):
        acc = jnp.dot(x, w_ref[:, c * pair:(c + 1) * pair], preferred_element_type=F32)
        col = c * pair
        if col < ATTN_WIDTH:
            for h in range(2):
                z = head(acc[:, h * HEAD_DIM:(h + 1) * HEAD_DIM], qg_ref[...], scale)
                q_ref[:, col + h * HEAD_DIM: col + (h + 1) * HEAD_DIM] = z.astype(BF16)
        elif col < QK_WIDTH:
            for h in range(2):
                z = head(acc[:, h * HEAD_DIM:(h + 1) * HEAD_DIM], kg_ref[...], 1.0)
                k_ref[:, h * HEAD_DIM:(h + 1) * HEAD_DIM] = z.astype(BF16)
        elif col < QK_WIDTH + KV_WIDTH:
            v_ref[...] = acc.astype(BF16)
        else:
            tm = acc.shape[0]
            for half in range(2):
                blk = (col - (QK_WIDTH + KV_WIDTH)) // LANES + half
                u_s[...] = acc[:, half * LANES:(half + 1) * LANES]
                for j in range(SSM_CHUNK):
                    step_rows = u_s[pl.ds(j, tm // SSM_CHUNK, stride=SSM_CHUNK), :]
                    u_ref[blk, :, j * LANES:(j + 1) * LANES] = step_rows.astype(BF16)


def _inproj(xb, w_in, q_gain, k_gain, cos2, sin2, layer, seq_len, tm=512):
    t, d = xb.shape
    tm = min(tm, seq_len)
    sblocks = seq_len // tm
    nblk = SSM_GROUPS // SSM_GB
    return pl.pallas_call(
        _inproj_kernel,
        grid=(t // tm,),
        in_specs=[_row_spec(tm, d),
                  _resident_spec((None, d, D_IN_PROJ), lambda m: (layer, 0, 0)),
                  _layer_vec_spec(layer, HEAD_DIM),
                  _layer_vec_spec(layer, HEAD_DIM),
                  pl.BlockSpec((tm, HEAD_DIM), lambda m: (m % sblocks, 0)),
                  pl.BlockSpec((tm, HEAD_DIM), lambda m: (m % sblocks, 0))],
        out_specs=[_row_spec(tm, ATTN_WIDTH), _row_spec(tm, KV_WIDTH), _row_spec(tm, KV_WIDTH),
                   pl.BlockSpec((nblk, tm // SSM_CHUNK, SSM_GB * SSM_CW), lambda m: (0, m, 0))],
        out_shape=[jax.ShapeDtypeStruct((t, ATTN_WIDTH), BF16),
                   jax.ShapeDtypeStruct((t, KV_WIDTH), BF16),
                   jax.ShapeDtypeStruct((t, KV_WIDTH), BF16),
                   jax.ShapeDtypeStruct((nblk, t // SSM_CHUNK, SSM_GB * SSM_CW), BF16)],
        scratch_shapes=[pltpu.VMEM((tm, LANES), F32)],
        compiler_params=_cparams(("parallel",)),
        name="inproj",
    )(xb, w_in, q_gain, k_gain, cos2, sin2)


def _attn_kernel(q_ref, k_ref, v_ref, g_ref, o_ref):
    outs = []
    ssq = None
    for h in range(N_HEADS):
        kv = h // GQA
        q = q_ref[:, h * HEAD_DIM:(h + 1) * HEAD_DIM]
        k = k_ref[:, kv * HEAD_DIM:(kv + 1) * HEAD_DIM]
        v = v_ref[:, kv * HEAD_DIM:(kv + 1) * HEAD_DIM]
        s = lax.dot_general(q, k, (((1,), (1,)), ((), ())), preferred_element_type=F32)
        m = jnp.max(s, axis=-1, keepdims=True)
        p = jnp.exp(s - m)
        l = jnp.sum(p, axis=-1, keepdims=True)
        o = jnp.dot(p.astype(BF16), v, preferred_element_type=F32) / l
        outs.append(o)
        sq = jnp.sum(o * o, axis=-1, keepdims=True)
        ssq = sq if ssq is None else ssq + sq
    r = lax.rsqrt(ssq * (1.0 / ATTN_WIDTH) + EPS)
    for h in range(N_HEADS):
        sl = slice(h * HEAD_DIM, (h + 1) * HEAD_DIM)
        o_ref[:, sl] = (outs[h] * r * g_ref[:, sl]).astype(BF16)


def _attention(q, k, v, gain, layer, bsz, seq_len, tq=512):
    t = q.shape[0]
    tq = min(tq, seq_len)
    qblocks = seq_len // tq
    return pl.pallas_call(
        _attn_kernel,
        grid=(bsz, qblocks),
        in_specs=[pl.BlockSpec((tq, ATTN_WIDTH), lambda b, i: (b * qblocks + i, 0)),
                  pl.BlockSpec((seq_len, KV_WIDTH), lambda b, i: (b, 0)),
                  pl.BlockSpec((seq_len, KV_WIDTH), lambda b, i: (b, 0)),
                  pl.BlockSpec((None, 1, ATTN_WIDTH), lambda b, i: (layer, 0, 0))],
        out_specs=pl.BlockSpec((tq, ATTN_WIDTH), lambda b, i: (b * qblocks + i, 0)),
        out_shape=jax.ShapeDtypeStruct((t, ATTN_WIDTH), BF16),
        compiler_params=_cparams(("parallel", "parallel")),
        name="attention",
    )(q, k, v, gain)


def _glu_kernel(y_ref, w_ref, b_ref, g_ref, o_ref):
    y = y_ref[...]
    gate = jnp.dot(y.astype(BF16), w_ref[...], preferred_element_type=F32) + b_ref[...]
    z = y * jax.nn.sigmoid(gate)
    zn = z * lax.rsqrt(jnp.mean(z * z, axis=-1, keepdims=True) + EPS) * g_ref[...]
    o_ref[...] = zn.astype(BF16)


def _glu(y, w_glu, b_glu, gain, layer, tm=512):
    t, w = y.shape
    tm = min(tm, t)
    return pl.pallas_call(
        _glu_kernel,
        grid=(t // tm,),
        in_specs=[_row_spec(tm, w),
                  pl.BlockSpec((None, w, w), lambda m: (layer, 0, 0)),
                  _layer_vec_spec(layer, w),
                  _layer_vec_spec(layer, w)],
        out_specs=_row_spec(tm, w),
        out_shape=jax.ShapeDtypeStruct((t, w), BF16),
        compiler_params=_cparams(("parallel",)),
        name="glu",
    )(y, w_glu, b_glu, gain)


ROUTE_LANE_GROUP = N_EXPERTS


def _route(logits):
    lane = lax.broadcasted_iota(jnp.int32, logits.shape, 1)
    neg = jnp.float32(-1e30)
    big = jnp.int32(LANES)

    def masked_softmax(mask):
        z = jnp.where(mask, logits, neg)
        zmax = jnp.max(z, axis=-1, keepdims=True)
        e = jnp.where(mask, jnp.exp(z - zmax), 0.0)
        return e / jnp.sum(e, axis=-1, keepdims=True)

    def first_argmax(vals, mask):
        top = jnp.max(jnp.where(mask, vals, -1.0), axis=-1, keepdims=True)
        idx = jnp.min(jnp.where(mask & (vals == top), lane, big), axis=-1, keepdims=True)
        return top, idx

    gmask = (lane >= ROUTE_LANE_GROUP) & (lane < ROUTE_LANE_GROUP + N_EXPERT_GROUPS)
    g_top, g_lane = first_argmax(masked_softmax(gmask), gmask)
    e_lo = (g_lane - ROUTE_LANE_GROUP) * EXPERTS_PER_GROUP
    emask = (lane >= e_lo) & (lane < e_lo + EXPERTS_PER_GROUP)
    e_prob = masked_softmax(emask)
    v1, i1 = first_argmax(e_prob, emask)
    v2, i2 = first_argmax(e_prob, emask & (lane != i1))
    denom = v1 + v2
    return i1, i2, g_top * (v1 / denom), g_top * (v2 / denom)


TOK_ROWS = D_MODEL // LANES
MOE_TM = 256
GATHER_PITCH = 24
RL_E1, RL_E2, RL_G1, RL_G2, RL_R1, RL_R2 = 0, 1, 2, 3, 4, 5


def _to_token_major(ref, x):
    tm = x.shape[0]
    for s in range(TOK_ROWS):
        ref[pl.ds(s, tm, stride=TOK_ROWS), :] = x[:, s * LANES:(s + 1) * LANES]


def _from_token_major(ref, tm, s):
    return ref[pl.ds(s, tm, stride=TOK_ROWS), :]


def _outproj_kernel(a_ref, y_ref, wa_ref, wy_ref, x_ref, g_ref, b_ref, wr_ref, br_ref,
                    xb_ref, x16_ref, r_ref, meta_ref, cnt_ref, run_s):
    tm = a_ref.shape[0]

    @pl.when(pl.program_id(0) == 0)
    def _():
        run_s[...] = jnp.zeros_like(run_s)

    m = (jnp.dot(a_ref[...], wa_ref[...], preferred_element_type=F32)
         + jnp.dot(y_ref[...], wy_ref[...], preferred_element_type=F32))
    xn = _layer_norm(ALPHA * x_ref[...] + m, g_ref[...], b_ref[...])
    xn_hi = xn.astype(BF16)
    xb_ref[...] = xn_hi
    _to_token_major(x16_ref, xn)
    xn_lo = (xn - xn_hi.astype(F32)).astype(BF16)
    logits = (jnp.dot(xn_hi, wr_ref[0], preferred_element_type=F32)
              + jnp.dot(xn_hi, wr_ref[1], preferred_element_type=F32)
              + jnp.dot(xn_lo, wr_ref[0], preferred_element_type=F32)) + br_ref[...]
    i1, i2, g1, g2 = _route(logits)

    lane = lax.broadcasted_iota(jnp.int32, (tm, LANES), 1)
    hit1, hit2 = lane == i1, lane == i2
    onehot = (hit1 | hit2).astype(BF16)
    earlier = (lax.broadcasted_iota(jnp.int32, (tm, tm), 0)
               > lax.broadcasted_iota(jnp.int32, (tm, tm), 1)).astype(BF16)
    rank = jnp.dot(earlier, onehot, preferred_element_type=F32) + run_s[0:1, :]
    r1 = jnp.sum(jnp.where(hit1, rank, 0.0), axis=-1, keepdims=True)
    r2 = jnp.sum(jnp.where(hit2, rank, 0.0), axis=-1, keepdims=True)
    total = run_s[0:1, :] + jnp.sum(onehot.astype(F32), axis=0, keepdims=True)
    run_s[...] = jnp.broadcast_to(total, run_s.shape)
    cnt_ref[...] = jnp.broadcast_to(total, cnt_ref.shape)
    out = jnp.zeros((tm, LANES), F32)
    for col, val in ((RL_E1, i1.astype(F32)), (RL_E2, i2.astype(F32)), (RL_G1, g1), (RL_G2, g2),
                     (RL_R1, r1), (RL_R2, r2)):
        out = jnp.where(lane == col, val, out)
    r_ref[...] = out
    meta_ref[...] = out.T[:SUBLANES, :].astype(jnp.int32)


def _outproj(attn_n, y_n, w_out, xf, ln_g, ln_b, w_route, b_route, layer, tm=512):
    t, d = xf.shape
    tm = min(tm, t)
    half = D_MIX // 2
    return pl.pallas_call(
        _outproj_kernel,
        grid=(t // tm,),
        in_specs=[_row_spec(tm, half), _row_spec(tm, half),
                  _resident_spec((None, half, d), lambda m: (layer, 0, 0)),
                  _resident_spec((None, half, d), lambda m: (layer, 1, 0)),
                  _row_spec(tm, d),
                  _layer_vec_spec(layer, d), _layer_vec_spec(layer, d),
                  _resident_spec((None, 2, d, LANES), lambda m: (layer, 0, 0, 0)),
                  _layer_vec_spec(layer, LANES)],
        out_specs=[_row_spec(tm, d), _row_spec(tm * TOK_ROWS, LANES), _row_spec(tm, LANES),
                   pl.BlockSpec((SUBLANES, tm), lambda m: (0, m)),
                   pl.BlockSpec((SUBLANES, LANES), lambda m: (0, 0))],
        out_shape=[jax.ShapeDtypeStruct((t, d), BF16),
                   jax.ShapeDtypeStruct((t * TOK_ROWS, LANES), F32),
                   jax.ShapeDtypeStruct((t, LANES), F32),
                   jax.ShapeDtypeStruct((SUBLANES, t), jnp.int32),
                   jax.ShapeDtypeStruct((SUBLANES, LANES), F32)],
        scratch_shapes=[pltpu.VMEM((SUBLANES, LANES), F32)],
        compiler_params=_cparams(("arbitrary",)),
        name="outproj_ln_route",
    )(attn_n, y_n, w_out, w_out, xf, ln_g, ln_b, w_route, b_route)


def _token_copy(src, src_tok, dst, dst_tok, sem, dst_pitch=TOK_ROWS):
    return pltpu.make_async_copy(
        src.at[pl.ds(pl.multiple_of(src_tok * TOK_ROWS, TOK_ROWS), TOK_ROWS), :],
        dst.at[pl.ds(pl.multiple_of(dst_tok * dst_pitch, SUBLANES), TOK_ROWS), :],
        sem)


def _sorted_row(es_ref, e_ref, r_ref, t):
    return es_ref[e_ref[t]] + r_ref[t]


def _dispatch_kernel(es_ref, e1_ref, e2_ref, r1_ref, r2_ref, x16_ref, xb_ref, wg_ref, bg_ref, xs_in_ref,
                     xs_ref, gate_ref, sem, *, tm):
    del xs_in_ref
    for t in range(tm):
        _token_copy(x16_ref, t, xs_ref, _sorted_row(es_ref, e1_ref, r1_ref, t), sem.at[0]).start(priority=0)
        _token_copy(x16_ref, t, xs_ref, _sorted_row(es_ref, e2_ref, r2_ref, t), sem.at[1]).start(priority=1)
    gate = jnp.dot(xb_ref[...], wg_ref[...], preferred_element_type=F32) + bg_ref[...]
    gate_ref[...] = jax.nn.sigmoid(gate).astype(BF16)
    whole = pl.ds(0, tm * TOK_ROWS)
    pltpu.make_async_copy(x16_ref, xs_ref.at[whole, :], sem.at[0]).wait()
    pltpu.make_async_copy(x16_ref, xs_ref.at[whole, :], sem.at[1]).wait()


def _smem_tile_spec(tm):
    return pl.BlockSpec((tm,), lambda m: (m,), memory_space=pltpu.SMEM)


def _token_meta_specs(tm):
    return [pl.BlockSpec(memory_space=pltpu.SMEM)] + [_smem_tile_spec(tm)] * 4


def _dispatch(x16, e_start, meta, xs_buf, xb, w_pg, b_pg, layer, tm=512):
    t, d = xb.shape
    tm = min(tm, t)
    return pl.pallas_call(
        functools.partial(_dispatch_kernel, tm=tm),
        grid=(t // tm,),
        in_specs=_token_meta_specs(tm) + [_row_spec(tm * TOK_ROWS, LANES),
                                          _row_spec(tm, d),
                                          _resident_spec((None, d, d), lambda m: (layer, 0, 0)),
                                          _layer_vec_spec(layer, d),
                                          pl.BlockSpec(memory_space=pl.ANY)],
        out_specs=[pl.BlockSpec(memory_space=pl.ANY), _row_spec(tm, d)],
        out_shape=[jax.ShapeDtypeStruct(xs_buf.shape, F32), jax.ShapeDtypeStruct((t, d), BF16)],
        input_output_aliases={9: 0},
        scratch_shapes=[pltpu.SemaphoreType.DMA((2,))],
        compiler_params=_cparams(("arbitrary",)),
        name="moe_dispatch_gate",
    )(e_start, meta[RL_E1], meta[RL_E2], meta[RL_R1], meta[RL_R2], x16, xb, w_pg, b_pg, xs_buf)


def _expert_kernel(te_ref, tr_ref, xs_ref, wg_ref, wu_ref, wd_ref, ys_ref, lhs_s, wg_s, wu_s, wd_s, *, tm):
    j = pl.program_id(0)
    rows = tr_ref[j]

    @pl.when(rows == 0)
    def _():
        ys_ref[...] = jnp.zeros_like(ys_ref)

    @pl.when((rows > 0) & ((j == 0) | (te_ref[j] != te_ref[jnp.maximum(j - 1, 0)])))
    def _():
        wg_s[...] = wg_ref[...].astype(BF16)
        wu_s[...] = wu_ref[...].astype(BF16)
        wd_s[...] = wd_ref[...].astype(BF16)

    @pl.when(rows > 0)
    def _():
        valid = lax.broadcasted_iota(jnp.int32, (tm, LANES), 0) < rows
        for s in range(TOK_ROWS):
            blk = jnp.where(valid, _from_token_major(xs_ref, tm, s), 0.0)
            lhs_s[:, s * LANES:(s + 1) * LANES] = blk.astype(BF16)
        x = lhs_s[...]
        hg = jnp.dot(x, wg_s[...], preferred_element_type=F32)
        hu = jnp.dot(x, wu_s[...], preferred_element_type=F32)
        hid = ((hg * jax.nn.sigmoid(hg)) * hu).astype(BF16)
        for c in range(D_MODEL // D_EXPERT):
            out = jnp.dot(hid, wd_s[:, c * D_EXPERT:(c + 1) * D_EXPERT], preferred_element_type=F32)
            for h in range(D_EXPERT // LANES):
                s = c * (D_EXPERT // LANES) + h
                ys_ref[pl.ds(s, tm, stride=TOK_ROWS), :] = out[:, h * LANES:(h + 1) * LANES]


def _moe_n_tiles(t, tm):
    return (2 * t) // tm + N_EXPERTS


def _moe_tiles(counts, t, tm):
    n_tiles = _moe_n_tiles(t, tm)
    per_e = (counts + tm - 1) // tm
    ends = jnp.cumsum(per_e)
    starts = ends - per_e
    j = jnp.arange(n_tiles, dtype=jnp.int32)[:, None]
    e = jnp.minimum(jnp.sum((j >= ends[None, :]).astype(jnp.int32), axis=1), N_EXPERTS - 1)
    mine = (j >= starts[None, :]) & (j < ends[None, :])
    rows = jnp.sum(jnp.where(mine, jnp.clip(counts[None, :] - (j - starts[None, :]) * tm, 0, tm), 0), axis=1)
    return e.astype(jnp.int32), rows.astype(jnp.int32), (starts * tm).astype(jnp.int32)


def _experts(xs, te, tr, w_gate, w_up, w_down, layer, tm):
    d = D_MODEL
    blk = pl.BlockSpec((tm * TOK_ROWS, LANES), lambda j, te, tr: (j, 0))
    grid_spec = pltpu.PrefetchScalarGridSpec(
        num_scalar_prefetch=2,
        grid=(te.shape[0],),
        in_specs=[blk,
                  pl.BlockSpec((None, None, d, D_EXPERT), lambda j, te, tr: (layer, te[j], 0, 0)),
                  pl.BlockSpec((None, None, d, D_EXPERT), lambda j, te, tr: (layer, te[j], 0, 0)),
                  pl.BlockSpec((None, None, D_EXPERT, d), lambda j, te, tr: (layer, te[j], 0, 0))],
        out_specs=blk,
        scratch_shapes=[pltpu.VMEM((tm, d), BF16),
                        pltpu.VMEM((d, D_EXPERT), BF16),
                        pltpu.VMEM((d, D_EXPERT), BF16),
                        pltpu.VMEM((D_EXPERT, d), BF16)],
    )
    return pl.pallas_call(
        functools.partial(_expert_kernel, tm=tm),
        grid_spec=grid_spec,
        out_shape=jax.ShapeDtypeStruct(xs.shape, F32),
        compiler_params=_cparams(("arbitrary",)),
        name="moe_experts",
    )(te, tr, xs, w_gate, w_up, w_down)


def _combine_kernel(es_ref, e1_ref, e2_ref, r1_ref, r2_ref, e1n_ref, e2n_ref, r1n_ref, r2n_ref,
                    gate_ref, x16_ref, r_ref, p_ref, wp_ref, g_ref, b_ref, ys_ref, of_ref, ob_ref,
                    *scratch, sub, n_sub, lead):
    slots = [(scratch[2 * k], scratch[2 * k + 1]) for k in range(n_sub)]
    h_s, sem = scratch[2 * n_sub], scratch[2 * n_sub + 1]
    i = pl.program_id(0)
    this_step = (e1_ref, e2_ref, r1_ref, r2_ref)
    next_step = (e1n_ref, e2n_ref, r1n_ref, r2n_ref)

    def issue(slot, fields, off):
        e1, e2, r1, r2 = fields
        y1_s, y2_s = slots[slot]
        for t in range(sub):
            _token_copy(ys_ref, _sorted_row(es_ref, e1, r1, off + t), y1_s, t, sem.at[slot, 0],
                        GATHER_PITCH).start(priority=0)
            _token_copy(ys_ref, _sorted_row(es_ref, e2, r2, off + t), y2_s, t, sem.at[slot, 1],
                        GATHER_PITCH).start(priority=1)

    def wait(slot):
        whole = pl.ds(0, sub * TOK_ROWS)
        for k in range(2):
            pltpu.make_async_copy(ys_ref.at[whole, :], slots[slot][k].at[whole, :], sem.at[slot, k]).wait()

    def combine(slot, off):
        y1_s, y2_s = slots[slot]
        rows = pl.ds(off, sub)
        ple = (jnp.dot(p_ref[rows, :].astype(BF16), wp_ref[...], preferred_element_type=F32)
               * gate_ref[rows, :].astype(F32))
        g1 = r_ref[rows, RL_G1:RL_G1 + 1]
        g2 = r_ref[rows, RL_G2:RL_G2 + 1]
        for s in range(TOK_ROWS):
            sl = slice(s * LANES, (s + 1) * LANES)
            f = (g1 * y1_s[pl.ds(s, sub, stride=GATHER_PITCH), :]
                 + g2 * y2_s[pl.ds(s, sub, stride=GATHER_PITCH), :])
            resid = x16_ref[pl.ds(off * TOK_ROWS + s, sub, stride=TOK_ROWS), :]
            h_s[:, sl] = ALPHA * resid + f + ple[:, sl]
        xn = _layer_norm(h_s[...], g_ref[...], b_ref[...])
        of_ref[rows, :] = xn
        ob_ref[rows, :] = xn.astype(BF16)

    @pl.when(i == 0)
    def _():
        for k in range(lead):
            issue(k, this_step, k * sub)

    for k in range(n_sub):
        wait(k)
        ahead = k + lead
        if ahead < n_sub:
            issue(ahead, this_step, ahead * sub)
        else:
            issue(ahead - n_sub, next_step, (ahead - n_sub) * sub)
        combine(k, k * sub)

    @pl.when(i == pl.num_programs(0) - 1)
    def _():
        for k in range(lead):
            wait(k)


COMBINE_SUB = 128
COMBINE_SLOTS = 4
COMBINE_LEAD = 2


def _combine_ple_ln(gate, x16, route, e_start, meta, ys, p, w_ple, ln_g, ln_b, layer):
    t, d = gate.shape
    n_sub, lead = COMBINE_SLOTS, COMBINE_LEAD
    sub = min(COMBINE_SUB, t // n_sub)
    tm = n_sub * sub
    steps = t // tm
    ahead = lead * sub
    fields = (meta[RL_E1], meta[RL_E2], meta[RL_R1], meta[RL_R2])
    next_spec = pl.BlockSpec((ahead,), lambda m: (jnp.minimum(m + 1, steps - 1) * (tm // ahead),),
                             memory_space=pltpu.SMEM)
    return pl.pallas_call(
        functools.partial(_combine_kernel, sub=sub, n_sub=n_sub, lead=lead),
        grid=(steps,),
        in_specs=_token_meta_specs(tm) + [next_spec] * 4 + [
            _row_spec(tm, d), _row_spec(tm * TOK_ROWS, LANES), _row_spec(tm, LANES),
            pl.BlockSpec((None, tm, PLE_DIM), lambda m: (layer, m, 0)),
            _resident_spec((None, PLE_DIM, d), lambda m: (layer, 0, 0)),
            _layer_vec_spec(layer, d), _layer_vec_spec(layer, d),
            pl.BlockSpec(memory_space=pl.ANY)],
        out_specs=[_row_spec(tm, d), _row_spec(tm, d)],
        out_shape=[jax.ShapeDtypeStruct((t, d), F32), jax.ShapeDtypeStruct((t, d), BF16)],
        scratch_shapes=[*[pltpu.VMEM((sub * GATHER_PITCH, LANES), F32) for _ in range(2 * n_sub)],
                        pltpu.VMEM((sub, d), F32),
                        pltpu.SemaphoreType.DMA((n_sub, 2))],
        compiler_params=_cparams(("arbitrary",)),
        name="moe_combine_ple_ln",
    )(e_start, *fields, *fields, gate, x16, route, p, w_ple, ln_g, ln_b, ys)


def _rope_tables(seq_len):
    rows = seq_len // GRID_W
    row = jnp.repeat(jnp.arange(rows, dtype=F32), GRID_W)
    col = jnp.tile(jnp.arange(GRID_W, dtype=F32), rows)
    inv_freq = ROPE_THETA ** (-jnp.arange(ROPE_AXIS_PAIRS, dtype=F32) / ROPE_AXIS_PAIRS)
    ang = jnp.concatenate([row[:, None] * inv_freq, col[:, None] * inv_freq], axis=-1)
    cos, sin = jnp.cos(ang), jnp.sin(ang)
    return jnp.concatenate([cos, cos], axis=-1), jnp.concatenate([-sin, sin], axis=-1)


def kernel(x, p, ln0_g, ln0_b, w_in, q_norm_g, k_norm_g, ssm_a_re, ssm_a_im, ssm_log_dt, ssm_b_re, ssm_b_im, ssm_c_re, ssm_c_im, ssm_d, w_glu, b_glu, attn_out_g, ssm_out_g, w_out, ln1_g, ln1_b, router_group_w, router_group_b, router_expert_w, router_expert_b, w_gate_e, w_up_e, w_down_e, w_ple, w_ple_gate, b_ple_gate, ln2_g, ln2_b):
    bsz, seq_len, d = x.shape
    t = bsz * seq_len
    n_layers = w_in.shape[0]
    vec = lambda a: a.reshape(n_layers, 1, a.shape[-1])

    cos2, sin2 = _rope_tables(seq_len)
    w_in_b, w_glu_b, w_out_b = w_in.astype(BF16), w_glu.astype(BF16), w_out.astype(BF16)
    w_ple_b, w_pg_b = w_ple.astype(BF16), w_ple_gate.astype(BF16)
    pad = LANES - N_EXPERTS - N_EXPERT_GROUPS
    w_route = jnp.pad(jnp.concatenate([router_expert_w, router_group_w], axis=-1), ((0, 0), (0, 0), (0, pad)))
    w_route_hi = w_route.astype(BF16)
    w_route_lo = (w_route - w_route_hi.astype(F32)).astype(BF16)
    w_route = jnp.stack([w_route_hi, w_route_lo], axis=1)
    b_route = jnp.pad(jnp.concatenate([router_expert_b, router_group_b], axis=-1), ((0, 0), (0, pad)))
    s5_ops = jax.vmap(_s5_operands)(ssm_a_re, ssm_a_im, ssm_log_dt, ssm_b_re, ssm_b_im,
                                    ssm_c_re, ssm_c_im, ssm_d)
    p2 = p.reshape(n_layers, t, PLE_DIM)
    perm = _s5_perm()

    tm_e = min(MOE_TM, t)
    xs = jnp.zeros((_moe_n_tiles(t, tm_e) * tm_e * TOK_ROWS, LANES), F32)
    xf, xb = _ln0(x.reshape(t, d), ln0_g, ln0_b)
    for i in range(n_layers):
        q, k, v, u = _inproj(xb, w_in_b, vec(q_norm_g), vec(k_norm_g), cos2, sin2, i, seq_len)
        attn_n = _attention(q, k, v, vec(attn_out_g), i, bsz, seq_len)
        y = _s5_mixer(u, perm, *s5_ops, i, bsz)
        y_n = _glu(y, w_glu_b, vec(b_glu), vec(ssm_out_g), i)
        xb, x16, route, meta, counts = _outproj(attn_n, y_n, w_out_b, xf, vec(ln1_g), vec(ln1_b),
                                                w_route, vec(b_route), i)
        te, tr, e_start = _moe_tiles(counts[0, :N_EXPERTS].astype(jnp.int32), t, tm_e)
        xs, gate = _dispatch(x16, e_start, meta, xs, xb, w_pg_b, vec(b_ple_gate), i)
        ys = _experts(xs, te, tr, w_gate_e, w_up_e, w_down_e, i, tm_e)
        xf, xb = _combine_ple_ln(gate, x16, route, e_start, meta, ys, p2, w_ple_b,
                                 vec(ln2_g), vec(ln2_b), i)
    return xf.reshape(bsz, seq_len, d)
```
